```python
import math
import jax, jax.numpy as jnp
from jax import lax
import numpy as np

D_MODEL = 2048
BATCH = 4
SEQ = 2048
DEPTH = 1

N_META = 16
Q_BLOCK = 128
HEAD_DIM = 128
FOX_HEADS = 8
DIFF_HEADS = 4
DIFF_V_DIM = 2 * HEAD_DIM
FOX_WIDTH = FOX_HEADS * HEAD_DIM
DIFF_QK_WIDTH = DIFF_HEADS * 2 * HEAD_DIM
DIFF_WIDTH = DIFF_HEADS * DIFF_V_DIM
IN_SIZES = (FOX_WIDTH, FOX_WIDTH, FOX_WIDTH, FOX_HEADS,
            DIFF_QK_WIDTH, DIFF_QK_WIDTH, DIFF_WIDTH,
            D_MODEL, D_MODEL)
D_IN = 3 * FOX_WIDTH + FOX_HEADS + 2 * DIFF_QK_WIDTH + DIFF_WIDTH + 2 * D_MODEL
N_BUCKETS = 32
MAX_DISTANCE = 128
N_EXPERTS = 32
TOP_K = 4
D_EXPERT = D_MODEL
SWIGLU_LIMIT = 7.0
SWIGLU_ALPHA = 1.702
EXPERT_BLOCK = 128
RMS_EPS = 1e-5
NEG_INF = -1e30

kernel_name = "hybrid_fox_diffattn_gated_moe"


def rmsnorm(x, g):
    xf = x.astype(jnp.float32)
    y = xf * lax.rsqrt(jnp.mean(xf * xf, axis=-1, keepdims=True) + RMS_EPS) * g.astype(jnp.float32)
    return y.astype(x.dtype)


def split_heads(t, n_heads):
    b, l, _ = t.shape
    return t.reshape(b, l, n_heads, -1).transpose(0, 2, 1, 3)


def merge_heads(t):
    b, h, l, d = t.shape
    return t.transpose(0, 2, 1, 3).reshape(b, l, h * d)


def sweep_query_blocks(attend, total_len):
    meta = attend(0, N_META)
    n_blk = (total_len - N_META) // Q_BLOCK
    real = lax.map(lambda r: attend(N_META + r * Q_BLOCK, Q_BLOCK), jnp.arange(n_blk, dtype=jnp.int32))
    nb, b, h, qb, dv = real.shape
    real = jnp.moveaxis(real, 0, 2).reshape(b, h, nb * qb, dv)
    return jnp.concatenate([meta, real], axis=2)


def t5_bucket(dist):
    n = jnp.maximum(dist, 0)
    max_exact = N_BUCKETS // 2
    log_part = jnp.log(jnp.maximum(n, 1).astype(jnp.float32) / max_exact) / math.log(MAX_DISTANCE / max_exact)
    large = max_exact + (log_part * (N_BUCKETS - max_exact)).astype(jnp.int32)
    large = jnp.minimum(large, N_BUCKETS - 1)
    return jnp.where(n < max_exact, n, large)


def token_mixers(h, w_in, fox_forget_bias, lam_q1, lam_k1, lam_q2, lam_k2, diff_subln,
                 w_branch_fox, w_branch_diff, w_out, rel_bias_table, lambda_init):
    b, l, _ = h.shape
    proj = h @ w_in
    offs = np.cumsum(IN_SIZES)[:-1].tolist()
    fq, fk, fv, ff, dq, dk, dv, g_fox, g_diff = jnp.split(proj, offs, axis=-1)
    kpos = jnp.arange(l, dtype=jnp.int32)
    scale = HEAD_DIM ** -0.5

    fq, fk, fv = split_heads(fq, FOX_HEADS), split_heads(fk, FOX_HEADS), split_heads(fv, FOX_HEADS)
    log_f = jax.nn.log_sigmoid(ff.astype(jnp.float32) + fox_forget_bias.astype(jnp.float32))
    cum = jnp.cumsum(log_f, axis=1).transpose(0, 2, 1)

    def fox_block(start, size):
        q = lax.dynamic_slice_in_dim(fq, start, size, axis=2)
        cq = lax.dynamic_slice_in_dim(cum, start, size, axis=2)
        qpos = start + jnp.arange(size, dtype=jnp.int32)
        s = jnp.einsum('bhqd,bhkd->bhqk', q, fk).astype(jnp.float32) * scale
        s = s + (cq[..., :, None] - cum[..., None, :])
        s = jnp.where(kpos[None, :] <= qpos[:, None], s, NEG_INF)
        p = jax.nn.softmax(s, axis=-1).astype(fv.dtype)
        return jnp.einsum('bhqk,bhkd->bhqd', p, fv)

    o_fox = merge_heads(sweep_query_blocks(fox_block, l))

    dq = dq.reshape(b, l, DIFF_HEADS, 2, HEAD_DIM).transpose(3, 0, 2, 1, 4)
    dk = dk.reshape(b, l, DIFF_HEADS, 2, HEAD_DIM).transpose(3, 0, 2, 1, 4)
    dq1, dq2, dk1, dk2 = dq[0], dq[1], dk[0], dk[1]
    dv = split_heads(dv, DIFF_HEADS)
    lam = (jnp.exp(jnp.sum(lam_q1.astype(jnp.float32) * lam_k1.astype(jnp.float32)))
           - jnp.exp(jnp.sum(lam_q2.astype(jnp.float32) * lam_k2.astype(jnp.float32)))
           + lambda_init)

    def diff_block(start, size):
        q1 = lax.dynamic_slice_in_dim(dq1, start, size, axis=2)
        q2 = lax.dynamic_slice_in_dim(dq2, start, size, axis=2)
        qpos = start + jnp.arange(size, dtype=jnp.int32)
        bucket = t5_bucket(qpos[:, None] - kpos[None, :])
        bias = rel_bias_table[bucket].transpose(2, 0, 1).astype(jnp.float32)
        mask = kpos[None, :] <= qpos[:, None]
        s1 = jnp.einsum('bhqd,bhkd->bhqk', q1, dk1).astype(jnp.float32) * scale + bias
        s2 = jnp.einsum('bhqd,bhkd->bhqk', q2, dk2).astype(jnp.float32) * scale + bias
        p = (jax.nn.softmax(jnp.where(mask, s1, NEG_INF), axis=-1)
             - lam * jax.nn.softmax(jnp.where(mask, s2, NEG_INF), axis=-1))
        return jnp.einsum('bhqk,bhkd->bhqd', p.astype(dv.dtype), dv)

    o_diff = sweep_query_blocks(diff_block, l)
    o_diff = merge_heads(rmsnorm(o_diff, diff_subln) * (1.0 - lambda_init))

    merged = (jax.nn.sigmoid(g_fox) * (o_fox @ w_branch_fox)
              + jax.nn.sigmoid(g_diff) * (o_diff @ w_branch_diff))
    return merged @ w_out


def moe_ffn(h2d, w_router, b_router, w_gate_up, b_gate_up, w_down, b_down):
    n, d = h2d.shape
    logits = (h2d @ w_router + b_router).astype(jnp.float32)
    top_vals, top_idx = lax.top_k(logits, TOP_K)
    gates = jax.nn.softmax(top_vals, axis=-1)
    n_assign = n * TOP_K
    expert_flat = top_idx.reshape(-1)
    token_flat = jnp.arange(n_assign, dtype=jnp.int32) // TOP_K
    gate_flat = gates.reshape(-1)
    order = jnp.argsort(expert_flat, stable=True)
    sorted_expert = expert_flat[order]
    counts = jnp.zeros((N_EXPERTS,), jnp.int32).at[expert_flat].add(1)
    padded = ((counts + EXPERT_BLOCK - 1) // EXPERT_BLOCK) * EXPERT_BLOCK
    starts = jnp.cumsum(counts) - counts
    pends = jnp.cumsum(padded)
    pstarts = pends - padded
    dest = pstarts[sorted_expert] + (jnp.arange(n_assign, dtype=jnp.int32) - starts[sorted_expert])
    n_rows = (-(-n_assign // EXPERT_BLOCK)) * EXPERT_BLOCK + N_EXPERTS * EXPERT_BLOCK
    n_blk = n_rows // EXPERT_BLOCK
    row_token = jnp.zeros((n_rows,), jnp.int32).at[dest].set(token_flat[order])
    row_gate = jnp.zeros((n_rows,), jnp.float32).at[dest].set(gate_flat[order])
    block_start = jnp.arange(n_blk, dtype=jnp.int32) * EXPERT_BLOCK
    block_expert = jnp.minimum(jnp.sum(block_start[:, None] >= pends[None, :], axis=1), N_EXPERTS - 1)
    xs = h2d[row_token].reshape(n_blk, EXPERT_BLOCK, d)

    def expert_block(args):
        xb, e = args
        gu = xb @ w_gate_up[e] + b_gate_up[e]
        x_glu, x_lin = gu[:, :D_EXPERT], gu[:, D_EXPERT:]
        x_glu = jnp.minimum(x_glu, SWIGLU_LIMIT)
        x_lin = jnp.clip(x_lin, -SWIGLU_LIMIT, SWIGLU_LIMIT)
        act = x_glu * jax.nn.sigmoid(SWIGLU_ALPHA * x_glu) * (x_lin + 1.0)
        return act @ w_down[e] + b_down[e]

    ys = lax.map(expert_block, (xs, block_expert)).reshape(n_rows, d)
    return jnp.zeros_like(h2d).at[row_token].add(ys * row_gate[:, None].astype(ys.dtype))


def setup_inputs(seed: int = 0) -> dict:
    key = jax.random.key(seed)
    ks = jax.random.split(key, 24)
    nrm = jax.random.normal
    f32 = jnp.float32
    L = DEPTH
    return {
        "x": nrm(ks[0], (BATCH, SEQ, D_MODEL), f32),
        "meta_tokens": nrm(ks[1], (N_META, D_MODEL), f32),
        "rel_bias_table": 0.5 * nrm(ks[2], (N_BUCKETS, DIFF_HEADS), f32),
        "attn_norm": 1.0 + 0.02 * nrm(ks[3], (L, D_MODEL), f32),
        "w_in": nrm(ks[4], (L, D_MODEL, D_IN), f32) * D_MODEL ** -0.5,
        "fox_forget_bias": jnp.linspace(1.0, 6.0, FOX_HEADS, dtype=f32)[None, :] + 0.1 * nrm(ks[5], (L, FOX_HEADS), f32),
        "lam_q1": 0.1 * nrm(ks[6], (L, HEAD_DIM), f32),
        "lam_k1": 0.1 * nrm(ks[7], (L, HEAD_DIM), f32),
        "lam_q2": 0.1 * nrm(ks[8], (L, HEAD_DIM), f32),
        "lam_k2": 0.1 * nrm(ks[9], (L, HEAD_DIM), f32),
        "diff_subln": 1.0 + 0.02 * nrm(ks[10], (L, DIFF_V_DIM), f32),
        "w_branch_fox": nrm(ks[11], (L, FOX_WIDTH, D_MODEL), f32) * FOX_WIDTH ** -0.5,
        "w_branch_diff": nrm(ks[12], (L, DIFF_WIDTH, D_MODEL), f32) * DIFF_WIDTH ** -0.5,
        "w_out": nrm(ks[13], (L, D_MODEL, D_MODEL), f32) * D_MODEL ** -0.5,
        "ffn_norm": 1.0 + 0.02 * nrm(ks[14], (L, D_MODEL), f32),
        "w_router": nrm(ks[15], (L, D_MODEL, N_EXPERTS), f32) * D_MODEL ** -0.5,
        "b_router": 0.01 * nrm(ks[16], (L, N_EXPERTS), f32),
        "w_gate_up": nrm(ks[17], (L, N_EXPERTS, D_MODEL, 2 * D_EXPERT), f32) * D_MODEL ** -0.5,
        "b_gate_up": 0.01 * nrm(ks[18], (L, N_EXPERTS, 2 * D_EXPERT), f32),
        "w_down": nrm(ks[19], (L, N_EXPERTS, D_EXPERT, D_MODEL), f32) * D_EXPERT ** -0.5,
        "b_down": 0.01 * nrm(ks[20], (L, N_EXPERTS, D_MODEL), f32),
        "final_norm": 1.0 + 0.02 * nrm(ks[21], (D_MODEL,), f32),
    }


def reference(x, meta_tokens, rel_bias_table, attn_norm, w_in, fox_forget_bias,
              lam_q1, lam_k1, lam_q2, lam_k2, diff_subln, w_branch_fox, w_branch_diff, w_out,
              ffn_norm, w_router, b_router, w_gate_up, b_gate_up, w_down, b_down, final_norm):
    b = x.shape[0]
    meta = jnp.broadcast_to(meta_tokens.astype(x.dtype)[None], (b, N_META, x.shape[-1]))
    h_res = jnp.concatenate([meta, x], axis=1)
    total = h_res.shape[1]
    for layer in range(DEPTH):
        lambda_init = 0.8 - 0.6 * math.exp(-0.3 * layer)
        hn = rmsnorm(h_res, attn_norm[layer])
        h_res = h_res + token_mixers(hn, w_in[layer], fox_forget_bias[layer],
                                     lam_q1[layer], lam_k1[layer], lam_q2[layer], lam_k2[layer],
                                     diff_subln[layer], w_branch_fox[layer], w_branch_diff[layer],
                                     w_out[layer], rel_bias_table, lambda_init)
        hn = rmsnorm(h_res, ffn_norm[layer])
        ffn = moe_ffn(hn.reshape(b * total, -1), w_router[layer], b_router[layer],
                      w_gate_up[layer], b_gate_up[layer], w_down[layer], b_down[layer])
        h_res = h_res + ffn.reshape(b, total, -1)
    return rmsnorm(h_res, final_norm)[:, N_META:, :]
```

```python
import functools
import math

import jax
import jax.numpy as jnp
from jax import lax
from jax.experimental import pallas as pl
from jax.experimental.pallas import tpu as pltpu

D_MODEL = 2048
BATCH = 4
SEQ = 2048
N_TOK = BATCH * SEQ
N_META = 16
N_META_PAD = 128
HEAD_DIM = 128
FOX_HEADS = 8
DIFF_HEADS = 4
DIFF_V_DIM = 2 * HEAD_DIM
FOX_WIDTH = FOX_HEADS * HEAD_DIM
DIFF_QK_WIDTH = DIFF_HEADS * 2 * HEAD_DIM
DIFF_WIDTH = DIFF_HEADS * DIFF_V_DIM
N_BUCKETS = 32
MAX_DISTANCE = 128
N_EXPERTS = 32
TOP_K = 4
D_EXPERT = D_MODEL
SWIGLU_LIMIT = 7.0
SWIGLU_ALPHA = 1.702
RMS_EPS = 1e-5
NEG_INF = -1e30
LAMBDA_INIT = 0.8 - 0.6 * math.exp(-0.3 * 0)
ATTN_SCALE = HEAD_DIM ** -0.5

LANES = 128
SUBLANES = 8
VMEM_LIMIT_BYTES = 56 * 1024 * 1024

COL_FQ = 0
COL_FK = COL_FQ + FOX_WIDTH
COL_FV = COL_FK + FOX_WIDTH
COL_DQ = COL_FV + FOX_WIDTH
COL_DK = COL_DQ + DIFF_QK_WIDTH
COL_DV = COL_DK + DIFF_QK_WIDTH
COL_GF = COL_DV + DIFF_WIDTH
COL_GD = COL_GF + D_MODEL
D_PROJ = COL_GD + D_MODEL

ATTN_TILE = 256
N_QBLK = SEQ // ATTN_TILE

MOE_ROWS = 1280
MOE_SUB = 256
MOE_TF = 512
MOE_NF = D_EXPERT // MOE_TF
N_ASSIGN = N_TOK * TOP_K
MOE_ITEMS = N_EXPERTS + N_ASSIGN // MOE_ROWS

f32 = jnp.float32
bf16 = jnp.bfloat16


def _params(sem, vmem=VMEM_LIMIT_BYTES):
    return pltpu.CompilerParams(dimension_semantics=sem, vmem_limit_bytes=vmem)


def _inproj_kernel(x_ref, g_ref, w_ref, wff_ref, o_ref, ff_ref, xn_ref):
    @pl.when(pl.program_id(1) == 0)
    def _():
        x = x_ref[...]
        ms = jnp.mean(x * x, axis=-1, keepdims=True)
        xn_ref[...] = (x * lax.rsqrt(ms + RMS_EPS) * g_ref[...]).astype(bf16)
        ff_ref[...] = jnp.dot(xn_ref[...], wff_ref[...], preferred_element_type=f32)

    o_ref[...] = jnp.dot(xn_ref[...], w_ref[...], preferred_element_type=f32).astype(bf16)


def _inproj(x2d, gain, w_main, w_ff, tm, tn):
    n = x2d.shape[0]
    return pl.pallas_call(
        _inproj_kernel,
        grid=(n // tm, D_PROJ // tn),
        in_specs=[
            pl.BlockSpec((tm, D_MODEL), lambda i, j: (i, 0)),
            pl.BlockSpec((1, D_MODEL), lambda i, j: (0, 0)),
            pl.BlockSpec((D_MODEL, tn), lambda i, j: (0, j)),
            pl.BlockSpec((D_MODEL, LANES), lambda i, j: (0, 0)),
        ],
        out_specs=[
            pl.BlockSpec((tm, tn), lambda i, j: (i, j)),
            pl.BlockSpec((tm, LANES), lambda i, j: (i, 0)),
        ],
        out_shape=[
            jax.ShapeDtypeStruct((n, D_PROJ), bf16),
            jax.ShapeDtypeStruct((n, LANES), f32),
        ],
        scratch_shapes=[pltpu.VMEM((tm, D_MODEL), bf16)],
        compiler_params=_params(("parallel", "arbitrary")),
        name="inproj",
    )(x2d, gain, w_main, w_ff)


def _log_sigmoid(x):
    return jnp.minimum(x, 0.0) - jnp.log(1.0 + jnp.exp(-jnp.abs(x)))


def _lane_cumsum(x):
    n = x.shape[-1]
    lane = lax.broadcasted_iota(jnp.int32, x.shape, x.ndim - 1)
    s = 1
    while s < n:
        x = x + jnp.where(lane >= s, pltpu.roll(x, s, x.ndim - 1), 0.0)
        s *= 2
    return x


def _cum_kernel(ff_ref, ffm_ref, fb_ref, ck_ref, ckm_ref):
    fb = fb_ref[...]
    row = lax.broadcasted_iota(jnp.int32, (LANES, LANES), 0)
    lfm = jnp.where(row < N_META, _log_sigmoid(ffm_ref[...] + fb), 0.0)
    cm = _lane_cumsum(lfm.T)
    m_total = cm[:, N_META - 1:N_META]
    lf = _log_sigmoid(ff_ref[...] + fb)
    cr = _lane_cumsum(lf.T) + m_total
    ck_ref[...] = cr[:SUBLANES]
    ckm_ref[...] = cm[:SUBLANES]


def _forget_cumsum(ff_real, ff_meta_pad, fb_pad):
    return pl.pallas_call(
        _cum_kernel,
        grid=(BATCH,),
        in_specs=[
            pl.BlockSpec((SEQ, LANES), lambda b: (b, 0)),
            pl.BlockSpec((LANES, LANES), lambda b: (0, 0)),
            pl.BlockSpec((1, LANES), lambda b: (0, 0)),
        ],
        out_specs=[
            pl.BlockSpec((None, SUBLANES, SEQ), lambda b: (b, 0, 0)),
            pl.BlockSpec((None, SUBLANES, LANES), lambda b: (b, 0, 0)),
        ],
        out_shape=[
            jax.ShapeDtypeStruct((BATCH, FOX_HEADS, SEQ), f32),
            jax.ShapeDtypeStruct((BATCH, FOX_HEADS, LANES), f32),
        ],
        compiler_params=_params(("parallel",)),
        name="forget_cumsum",
    )(ff_real, ff_meta_pad, fb_pad)


def _qk(q, k):
    return lax.dot_general(q, k, (((1,), (1,)), ((), ())), preferred_element_type=f32)


def _row_to_col(row):
    t = row.shape[-1]
    r = lax.broadcasted_iota(jnp.int32, (t, t), 0)
    c = lax.broadcasted_iota(jnp.int32, (t, t), 1)
    return jnp.sum(jnp.where(r == c, row, 0.0), axis=1, keepdims=True)


def _online_update(carry, s, v):
    m, l, acc = carry
    m_new = jnp.maximum(m, jnp.max(s, axis=-1, keepdims=True))
    alpha = jnp.exp(m - m_new)
    p = jnp.exp(s - m_new)
    l = alpha * l + jnp.sum(p, axis=-1, keepdims=True)
    acc = alpha * acc + jnp.dot(p.astype(bf16), v, preferred_element_type=f32)
    return m_new, l, acc


def _first_update(s, v):
    m = jnp.max(s, axis=-1, keepdims=True)
    p = jnp.exp(s - m)
    l = jnp.sum(p, axis=-1, keepdims=True)
    acc = jnp.dot(p.astype(bf16), v, preferred_element_type=f32)
    return m, l, acc


def _fox_kernel(q_ref, k_ref, v_ref, km_ref, vm_ref, ck_ref, ckm_ref, o_ref):
    h = pl.program_id(1)
    i = pl.program_id(2)
    t = ATTN_TILE
    q = q_ref[...]
    q0 = pl.multiple_of(i * t, t)
    cq = _row_to_col(ck_ref[pl.ds(h, 1), pl.ds(q0, t)])

    ckm = ckm_ref[pl.ds(h, 1), :]
    meta_ok = lax.broadcasted_iota(jnp.int32, (t, N_META_PAD), 1) < N_META
    s = _qk(q, km_ref[...]) * ATTN_SCALE + (cq - ckm)
    carry = _first_update(jnp.where(meta_ok, s, NEG_INF), vm_ref[...])

    def body(j, carry):
        k0 = pl.multiple_of(j * t, t)
        ck = ck_ref[pl.ds(h, 1), pl.ds(k0, t)]
        s = _qk(q, k_ref[pl.ds(k0, t), :]) * ATTN_SCALE + (cq - ck)
        return _online_update(carry, s, v_ref[pl.ds(k0, t), :])

    carry = lax.fori_loop(0, i, body, carry)

    ck = ck_ref[pl.ds(h, 1), pl.ds(q0, t)]
    s = _qk(q, k_ref[pl.ds(q0, t), :]) * ATTN_SCALE + (cq - ck)
    r = lax.broadcasted_iota(jnp.int32, (t, t), 0)
    c = lax.broadcasted_iota(jnp.int32, (t, t), 1)
    s = jnp.where(c <= r, s, NEG_INF)
    m, l, acc = _online_update(carry, s, v_ref[pl.ds(q0, t), :])
    o_ref[...] = (acc / l).astype(bf16)


def _fox_attention(proj, proj_meta, ck, ckm):
    t = ATTN_TILE
    cb = lambda col: col // HEAD_DIM
    return pl.pallas_call(
        _fox_kernel,
        grid=(BATCH, FOX_HEADS, N_QBLK),
        in_specs=[
            pl.BlockSpec((t, HEAD_DIM), lambda b, h, i: (b * N_QBLK + i, cb(COL_FQ) + h)),
            pl.BlockSpec((SEQ, HEAD_DIM), lambda b, h, i: (b, cb(COL_FK) + h)),
            pl.BlockSpec((SEQ, HEAD_DIM), lambda b, h, i: (b, cb(COL_FV) + h)),
            pl.BlockSpec((N_META_PAD, HEAD_DIM), lambda b, h, i: (0, cb(COL_FK) + h)),
            pl.BlockSpec((N_META_PAD, HEAD_DIM), lambda b, h, i: (0, cb(COL_FV) + h)),
            pl.BlockSpec((None, FOX_HEADS, SEQ), lambda b, h, i: (b, 0, 0)),
            pl.BlockSpec((None, FOX_HEADS, LANES), lambda b, h, i: (b, 0, 0)),
        ],
        out_specs=pl.BlockSpec((t, HEAD_DIM), lambda b, h, i: (b * N_QBLK + i, h)),
        out_shape=jax.ShapeDtypeStruct((N_TOK, FOX_WIDTH), bf16),
        compiler_params=_params(("parallel", "parallel", "arbitrary")),
        name="fox_attention",
    )(proj, proj, proj, proj_meta, proj_meta, ck, ckm)


def _t5_bias(dist, table_ref, h):
    n = jnp.maximum(dist, 0)
    max_exact = N_BUCKETS // 2
    log_part = jnp.log(jnp.maximum(n, 1).astype(f32) / max_exact) / math.log(MAX_DISTANCE / max_exact)
    large = max_exact + (log_part * (N_BUCKETS - max_exact)).astype(jnp.int32)
    large = jnp.minimum(large, N_BUCKETS - 1)
    bucket = jnp.where(n < max_exact, n, large)
    far = table_ref[N_BUCKETS - 1, h]
    out = jnp.zeros(dist.shape, f32)
    for b in range(N_BUCKETS - 1):
        out = jnp.where(bucket == b, table_ref[b, h] - far, out)
    return out


def _bias_kernel(table_ref, near_ref, meta_ref):
    h = pl.program_id(0)
    t = ATTN_TILE
    r = lax.broadcasted_iota(jnp.int32, (t, 2 * t), 0)
    c = lax.broadcasted_iota(jnp.int32, (t, 2 * t), 1)
    near_ref[...] = _t5_bias(r + t - c, table_ref, h)
    r = lax.broadcasted_iota(jnp.int32, (t, LANES), 0)
    c = lax.broadcasted_iota(jnp.int32, (t, LANES), 1)
    meta_ref[...] = _t5_bias(N_META + r - c, table_ref, h)


def _bias_tiles(table):
    t = ATTN_TILE
    return pl.pallas_call(
        _bias_kernel,
        grid=(DIFF_HEADS,),
        in_specs=[pl.BlockSpec(memory_space=pltpu.SMEM)],
        out_specs=[
            pl.BlockSpec((None, t, 2 * t), lambda h: (h, 0, 0)),
            pl.BlockSpec((None, t, LANES), lambda h: (h, 0, 0)),
        ],
        out_shape=[
            jax.ShapeDtypeStruct((DIFF_HEADS, t, 2 * t), f32),
            jax.ShapeDtypeStruct((DIFF_HEADS, t, LANES), f32),
        ],
        compiler_params=_params(("arbitrary",)),
        name="t5_bias_tiles",
    )(table)


def _diff_kernel(q1_ref, q2_ref, k1_ref, k2_ref, v_ref, k1m_ref, k2m_ref, vm_ref,
                 near_ref, mbias_ref, lam_ref, subln_ref, o_ref):
    i = pl.program_id(2)
    t = ATTN_TILE
    q1 = q1_ref[...]
    q2 = q2_ref[...]
    q0 = pl.multiple_of(i * t, t)
    lam = (jnp.exp(jnp.sum(lam_ref[0:1, :] * lam_ref[1:2, :], axis=-1, keepdims=True))
           - jnp.exp(jnp.sum(lam_ref[2:3, :] * lam_ref[3:4, :], axis=-1, keepdims=True))
           + LAMBDA_INIT)

    mb = jnp.where(i == 0, mbias_ref[...], 0.0)
    meta_ok = lax.broadcasted_iota(jnp.int32, (t, N_META_PAD), 1) < N_META
    vm = vm_ref[...]
    c1 = _first_update(jnp.where(meta_ok, _qk(q1, k1m_ref[...]) * ATTN_SCALE + mb, NEG_INF), vm)
    c2 = _first_update(jnp.where(meta_ok, _qk(q2, k2m_ref[...]) * ATTN_SCALE + mb, NEG_INF), vm)

    def far_body(j, carry):
        c1, c2 = carry
        k0 = pl.multiple_of(j * t, t)
        v = v_ref[pl.ds(k0, t), :]
        c1 = _online_update(c1, _qk(q1, k1_ref[pl.ds(k0, t), :]) * ATTN_SCALE, v)
        c2 = _online_update(c2, _qk(q2, k2_ref[pl.ds(k0, t), :]) * ATTN_SCALE, v)
        return c1, c2

    c1, c2 = lax.fori_loop(0, jnp.maximum(i - 1, 0), far_body, (c1, c2))

    def prev_block(carry):
        c1, c2 = carry
        k0 = pl.multiple_of((i - 1) * t, t)
        v = v_ref[pl.ds(k0, t), :]
        bias = near_ref[:, :t]
        c1 = _online_update(c1, _qk(q1, k1_ref[pl.ds(k0, t), :]) * ATTN_SCALE + bias, v)
        c2 = _online_update(c2, _qk(q2, k2_ref[pl.ds(k0, t), :]) * ATTN_SCALE + bias, v)
        return c1, c2

    c1, c2 = lax.cond(i > 0, prev_block, lambda carry: carry, (c1, c2))

    r = lax.broadcasted_iota(jnp.int32, (t, t), 0)
    c = lax.broadcasted_iota(jnp.int32, (t, t), 1)
    mask = c <= r
    bias = near_ref[:, t:]
    v = v_ref[pl.ds(q0, t), :]
    s1 = jnp.where(mask, _qk(q1, k1_ref[pl.ds(q0, t), :]) * ATTN_SCALE + bias, NEG_INF)
    s2 = jnp.where(mask, _qk(q2, k2_ref[pl.ds(q0, t), :]) * ATTN_SCALE + bias, NEG_INF)
    _, l1, a1 = _online_update(c1, s1, v)
    _, l2, a2 = _online_update(c2, s2, v)

    o = a1 / l1 - lam * (a2 / l2)
    y = o * lax.rsqrt(jnp.mean(o * o, axis=-1, keepdims=True) + RMS_EPS) * subln_ref[...]
    o_ref[...] = (y * (1.0 - LAMBDA_INIT)).astype(bf16)


def _diff_attention(proj, proj_meta, near, mbias, lam_vecs, subln):
    t = ATTN_TILE
    cb = lambda col: col // HEAD_DIM
    vb = lambda col: col // DIFF_V_DIM
    row = lambda b, h, i: b * N_QBLK + i
    return pl.pallas_call(
        _diff_kernel,
        grid=(BATCH, DIFF_HEADS, N_QBLK),
        in_specs=[
            pl.BlockSpec((t, HEAD_DIM), lambda b, h, i: (row(b, h, i), cb(COL_DQ) + 2 * h)),
            pl.BlockSpec((t, HEAD_DIM), lambda b, h, i: (row(b, h, i), cb(COL_DQ) + 2 * h + 1)),
            pl.BlockSpec((SEQ, HEAD_DIM), lambda b, h, i: (b, cb(COL_DK) + 2 * h)),
            pl.BlockSpec((SEQ, HEAD_DIM), lambda b, h, i: (b, cb(COL_DK) + 2 * h + 1)),
            pl.BlockSpec((SEQ, DIFF_V_DIM), lambda b, h, i: (b, vb(COL_DV) + h)),
            pl.BlockSpec((N_META_PAD, HEAD_DIM), lambda b, h, i: (0, cb(COL_DK) + 2 * h)),
            pl.BlockSpec((N_META_PAD, HEAD_DIM), lambda b, h, i: (0, cb(COL_DK) + 2 * h + 1)),
            pl.BlockSpec((N_META_PAD, DIFF_V_DIM), lambda b, h, i: (0, vb(COL_DV) + h)),
            pl.BlockSpec((None, t, 2 * t), lambda b, h, i: (h, 0, 0)),
            pl.BlockSpec((None, t, LANES), lambda b, h, i: (h, 0, 0)),
            pl.BlockSpec((4, HEAD_DIM), lambda b, h, i: (0, 0)),
            pl.BlockSpec((1, DIFF_V_DIM), lambda b, h, i: (0, 0)),
        ],
        out_specs=pl.BlockSpec((t, DIFF_V_DIM), lambda b, h, i: (row(b, h, i), h)),
        out_shape=jax.ShapeDtypeStruct((N_TOK, DIFF_WIDTH), bf16),
        compiler_params=_params(("parallel", "parallel", "arbitrary")),
        name="diff_attention",
    )(proj, proj, proj, proj, proj, proj_meta, proj_meta, proj_meta, near, mbias, lam_vecs, subln)


def _merge_kernel(of_ref, od_ref, wf_ref, wd_ref, gf_ref, gd_ref, o_ref):
    yf = jnp.dot(of_ref[...], wf_ref[...], preferred_element_type=f32)
    yd = jnp.dot(od_ref[...], wd_ref[...], preferred_element_type=f32)
    gf = jax.nn.sigmoid(gf_ref[...].astype(f32))
    gd = jax.nn.sigmoid(gd_ref[...].astype(f32))
    o_ref[...] = (gf * yf + gd * yd).astype(bf16)


def _gated_merge(o_fox, o_diff, w_bf, w_bd, proj, tm=1024, tn=512):
    return pl.pallas_call(
        _merge_kernel,
        grid=(D_MODEL // tn, N_TOK // tm),
        in_specs=[
            pl.BlockSpec((tm, FOX_WIDTH), lambda j, i: (i, 0)),
            pl.BlockSpec((tm, DIFF_WIDTH), lambda j, i: (i, 0)),
            pl.BlockSpec((FOX_WIDTH, tn), lambda j, i: (0, j)),
            pl.BlockSpec((DIFF_WIDTH, tn), lambda j, i: (0, j)),
            pl.BlockSpec((tm, tn), lambda j, i: (i, COL_GF // tn + j)),
            pl.BlockSpec((tm, tn), lambda j, i: (i, COL_GD // tn + j)),
        ],
        out_specs=pl.BlockSpec((tm, tn), lambda j, i: (i, j)),
        out_shape=jax.ShapeDtypeStruct((N_TOK, D_MODEL), bf16),
        compiler_params=_params(("parallel", "arbitrary")),
        name="gated_merge",
    )(o_fox, o_diff, w_bf, w_bd, proj, proj)


def _outproj_kernel(m_ref, w_ref, x_ref, g_ref, wr_ref, br_ref, h_ref, hn_ref, lg_ref):
    h1 = x_ref[...] + jnp.dot(m_ref[...], w_ref[...], preferred_element_type=f32)
    h_ref[...] = h1
    hn = h1 * lax.rsqrt(jnp.mean(h1 * h1, axis=-1, keepdims=True) + RMS_EPS) * g_ref[...]
    hn_ref[...] = hn
    lg_ref[...] = jnp.dot(hn, wr_ref[...], preferred_element_type=f32,
                          precision=lax.Precision.HIGHEST) + br_ref[...]


def _outproj(merged, w_out, x2d, gain, w_router_pad, b_router_pad, tm=512):
    return pl.pallas_call(
        _outproj_kernel,
        grid=(N_TOK // tm,),
        in_specs=[
            pl.BlockSpec((tm, D_MODEL), lambda i: (i, 0)),
            pl.BlockSpec((D_MODEL, D_MODEL), lambda i: (0, 0)),
            pl.BlockSpec((tm, D_MODEL), lambda i: (i, 0)),
            pl.BlockSpec((1, D_MODEL), lambda i: (0, 0)),
            pl.BlockSpec((D_MODEL, LANES), lambda i: (0, 0)),
            pl.BlockSpec((1, LANES), lambda i: (0, 0)),
        ],
        out_specs=[
            pl.BlockSpec((tm, D_MODEL), lambda i: (i, 0)),
            pl.BlockSpec((tm, D_MODEL), lambda i: (i, 0)),
            pl.BlockSpec((tm, LANES), lambda i: (i, 0)),
        ],
        out_shape=[
            jax.ShapeDtypeStruct((N_TOK, D_MODEL), f32),
            jax.ShapeDtypeStruct((N_TOK, D_MODEL), f32),
            jax.ShapeDtypeStruct((N_TOK, LANES), f32),
        ],
        compiler_params=_params(("parallel",)),
        name="outproj_router",
    )(merged, w_out, x2d, gain, w_router_pad, b_router_pad)


def _moe_kernel(item_e_ref, item_start_ref, item_n_ref, tok_ref, dest_ref,
                hn_hbm, wg_ref, wl_ref, bg_ref, bl_ref, wd_ref, bd_ref, y_hbm,
                acc_ref, xb_ref, wgb_ref, wlb_ref, wdb_ref, gsem, ssem):
    w = pl.program_id(0)
    t = pl.program_id(1)
    n = item_n_ref[w]
    start = item_start_ref[w]

    def row_in(r):
        return pltpu.make_async_copy(hn_hbm.at[pl.ds(tok_ref[start + r], 1), :],
                                     acc_ref.at[pl.ds(r, 1), :], gsem)

    def row_out(r):
        return pltpu.make_async_copy(acc_ref.at[pl.ds(r, 1), :],
                                     y_hbm.at[pl.ds(dest_ref[start + r], 1), :], ssem)

    @pl.when(jnp.logical_and(w == 0, t == 0))
    def _():
        acc_ref[...] = jnp.zeros_like(acc_ref)

    @pl.when(jnp.logical_and(t == 0, n > 0))
    def _():
        def issue(r, c):
            row_in(r).start()
            return c
        lax.fori_loop(0, n, issue, 0)

        def drain(r, c):
            row_in(r).wait()
            return c
        lax.fori_loop(0, n, drain, 0)
        xb_ref[...] = acc_ref[...].astype(bf16)
        acc_ref[...] = jnp.zeros_like(acc_ref)

    @pl.when(n > 0)
    def _():
        wgb_ref[...] = wg_ref[...].astype(bf16)
        wlb_ref[...] = wl_ref[...].astype(bf16)
        wdb_ref[...] = wd_ref[...].astype(bf16)

        def sub(r, c):
            r0 = pl.multiple_of(r * MOE_SUB, MOE_SUB)
            xs = xb_ref[pl.ds(r0, MOE_SUB), :]
            hg = jnp.dot(xs, wgb_ref[...], preferred_element_type=f32) + bg_ref[...]
            hl = jnp.dot(xs, wlb_ref[...], preferred_element_type=f32) + bl_ref[...]
            hg = jnp.minimum(hg, SWIGLU_LIMIT)
            hl = jnp.clip(hl, -SWIGLU_LIMIT, SWIGLU_LIMIT)
            act = hg * jax.nn.sigmoid(SWIGLU_ALPHA * hg) * (hl + 1.0)
            acc_ref[pl.ds(r0, MOE_SUB), :] += jnp.dot(act.astype(bf16), wdb_ref[...],
                                                      preferred_element_type=f32)
            return c
        lax.fori_loop(0, (n + MOE_SUB - 1) // MOE_SUB, sub, 0)

    @pl.when(jnp.logical_and(t == MOE_NF - 1, n > 0))
    def _():
        acc_ref[...] = acc_ref[...] + bd_ref[...]

        def issue(r, c):
            row_out(r).start()
            return c
        lax.fori_loop(0, n, issue, 0)

        def drain(r, c):
            row_out(r).wait()
            return c
        lax.fori_loop(0, n, drain, 0)


def _moe(item_e, item_start, item_n, tok_sorted, dest_sorted, hn2, w_gate_up, b_gate_up, w_down, b_down):
    tf = MOE_TF
    live_t = lambda w, t, n: jnp.where(n[w] > 0, t, MOE_NF - 1)
    grid_spec = pltpu.PrefetchScalarGridSpec(
        num_scalar_prefetch=5,
        grid=(MOE_ITEMS, MOE_NF),
        in_specs=[
            pl.BlockSpec(memory_space=pl.ANY),
            pl.BlockSpec((None, D_MODEL, tf), lambda w, t, e, s, n, tk, ds: (e[w], 0, live_t(w, t, n))),
            pl.BlockSpec((None, D_MODEL, tf), lambda w, t, e, s, n, tk, ds: (e[w], 0, MOE_NF + live_t(w, t, n))),
            pl.BlockSpec((None, 1, tf), lambda w, t, e, s, n, tk, ds: (e[w], 0, live_t(w, t, n))),
            pl.BlockSpec((None, 1, tf), lambda w, t, e, s, n, tk, ds: (e[w], 0, MOE_NF + live_t(w, t, n))),
            pl.BlockSpec((None, tf, D_MODEL), lambda w, t, e, s, n, tk, ds: (e[w], live_t(w, t, n), 0)),
            pl.BlockSpec((None, 1, D_MODEL), lambda w, t, e, s, n, tk, ds: (e[w], 0, 0)),
        ],
        out_specs=pl.BlockSpec(memory_space=pl.ANY),
        scratch_shapes=[
            pltpu.VMEM((MOE_ROWS, D_MODEL), f32),
            pltpu.VMEM((MOE_ROWS, D_MODEL), bf16),
            pltpu.VMEM((D_MODEL, tf), bf16),
            pltpu.VMEM((D_MODEL, tf), bf16),
            pltpu.VMEM((tf, D_MODEL), bf16),
            pltpu.SemaphoreType.DMA,
            pltpu.SemaphoreType.DMA,
        ],
    )
    return pl.pallas_call(
        _moe_kernel,
        grid_spec=grid_spec,
        out_shape=jax.ShapeDtypeStruct((TOP_K * N_TOK, D_MODEL), f32),
        compiler_params=_params(("arbitrary", "arbitrary")),
        name="moe_experts",
    )(item_e, item_start, item_n, tok_sorted, dest_sorted,
      hn2, w_gate_up, w_gate_up, b_gate_up, b_gate_up, w_down, b_down)


def _combine_kernel(y_ref, h_ref, gate_ref, g_ref, o_ref):
    h = h_ref[...]
    gates = gate_ref[...]
    for k in range(TOP_K):
        h = h + y_ref[k] * gates[:, k:k + 1]
    o_ref[...] = h * lax.rsqrt(jnp.mean(h * h, axis=-1, keepdims=True) + RMS_EPS) * g_ref[...]


def _combine(y, h1, gates, gain, tm=256):
    return pl.pallas_call(
        _combine_kernel,
        grid=(N_TOK // tm,),
        in_specs=[
            pl.BlockSpec((TOP_K, tm, D_MODEL), lambda i: (0, i, 0)),
            pl.BlockSpec((tm, D_MODEL), lambda i: (i, 0)),
            pl.BlockSpec((tm, TOP_K), lambda i: (i, 0)),
            pl.BlockSpec((1, D_MODEL), lambda i: (0, 0)),
        ],
        out_specs=pl.BlockSpec((tm, D_MODEL), lambda i: (i, 0)),
        out_shape=jax.ShapeDtypeStruct((N_TOK, D_MODEL), f32),
        compiler_params=_params(("parallel",)),
        name="combine_norm",
    )(y, h1, gates, gain)


def _route(logits):
    top_vals, top_idx = lax.top_k(logits, TOP_K)
    gates = jax.nn.softmax(top_vals, axis=-1)
    expert_flat = top_idx.reshape(-1).astype(jnp.int32)
    order = jnp.argsort(expert_flat, stable=True).astype(jnp.int32)
    counts = jnp.zeros((N_EXPERTS,), jnp.int32).at[expert_flat].add(1)
    starts = jnp.cumsum(counts) - counts
    tok_sorted = order // TOP_K
    dest_sorted = (order % TOP_K) * N_TOK + tok_sorted
    chunks = (counts + MOE_ROWS - 1) // MOE_ROWS
    chunk_end = jnp.cumsum(chunks)
    n_items = chunk_end[-1]
    w = jnp.arange(MOE_ITEMS, dtype=jnp.int32)
    w_eff = jnp.minimum(w, n_items - 1)
    e = jnp.sum(w_eff[:, None] >= chunk_end[None, :], axis=1).astype(jnp.int32)
    c = w_eff - (chunk_end - chunks)[e]
    item_start = starts[e] + c * MOE_ROWS
    item_n = jnp.where(w < n_items, jnp.clip(counts[e] - c * MOE_ROWS, 0, MOE_ROWS), 0)
    return gates, e, item_start.astype(jnp.int32), item_n.astype(jnp.int32), tok_sorted, dest_sorted


def kernel(x, meta_tokens, rel_bias_table, attn_norm, w_in, fox_forget_bias, lam_q1, lam_k1, lam_q2, lam_k2,
           diff_subln, w_branch_fox, w_branch_diff, w_out, ffn_norm, w_router, b_router, w_gate_up, b_gate_up,
           w_down, b_down, final_norm):
    x2d = x.reshape(N_TOK, D_MODEL)
    w_in0 = w_in[0]
    ff_lo = 3 * FOX_WIDTH
    w_main = jnp.concatenate([w_in0[:, :ff_lo], w_in0[:, ff_lo + FOX_HEADS:]], axis=1).astype(bf16)
    w_ff = jnp.pad(w_in0[:, ff_lo:ff_lo + FOX_HEADS], ((0, 0), (0, LANES - FOX_HEADS))).astype(bf16)
    fb_pad = jnp.pad(fox_forget_bias[0], (0, LANES - FOX_HEADS)).reshape(1, LANES)

    proj, ff_real = _inproj(x2d, attn_norm, w_main, w_ff, tm=1024, tn=512)
    meta_pad = jnp.pad(meta_tokens, ((0, N_META_PAD - N_META), (0, 0)))
    proj_meta, ff_meta = _inproj(meta_pad, attn_norm, w_main, w_ff, tm=N_META_PAD, tn=2048)

    ck, ckm = _forget_cumsum(ff_real, ff_meta, fb_pad)
    o_fox = _fox_attention(proj, proj_meta, ck, ckm)

    near, mbias = _bias_tiles(rel_bias_table)
    lam_vecs = jnp.concatenate([lam_q1, lam_k1, lam_q2, lam_k2], axis=0)
    o_diff = _diff_attention(proj, proj_meta, near, mbias, lam_vecs, diff_subln)

    merged = _gated_merge(o_fox, o_diff, w_branch_fox[0].astype(bf16), w_branch_diff[0].astype(bf16), proj)
    w_router_pad = jnp.pad(w_router[0], ((0, 0), (0, LANES - N_EXPERTS)))
    b_router_pad = jnp.pad(b_router[0], (0, LANES - N_EXPERTS)).reshape(1, LANES)
    h1, hn2, logits = _outproj(merged, w_out[0].astype(bf16), x2d, ffn_norm, w_router_pad, b_router_pad)

    gates, item_e, item_start, item_n, tok_sorted, dest_sorted = _route(logits[:, :N_EXPERTS])
    y = _moe(item_e, item_start, item_n, tok_sorted, dest_sorted, hn2,
             w_gate_up[0], b_gate_up[0].reshape(N_EXPERTS, 1, 2 * D_EXPERT),
             w_down[0], b_down[0].reshape(N_EXPERTS, 1, D_MODEL))
    out = _combine(y.reshape(TOP_K, N_TOK, D_MODEL), h1, gates, final_norm.reshape(1, D_MODEL))
    return out.reshape(BATCH, SEQ, D_MODEL)
```

```python
import functools
import math

import jax
import jax.numpy as jnp
from jax import lax
from jax.experimental import pallas as pl
from jax.experimental.pallas import tpu as pltpu

D_MODEL = 2048
BATCH = 4
SEQ = 2048
N_TOK = BATCH * SEQ
N_META = 16
N_META_PAD = 128
HEAD_DIM = 128
FOX_HEADS = 8
DIFF_HEADS = 4
DIFF_V_DIM = 2 * HEAD_DIM
FOX_WIDTH = FOX_HEADS * HEAD_DIM
DIFF_QK_WIDTH = DIFF_HEADS * 2 * HEAD_DIM
DIFF_WIDTH = DIFF_HEADS * DIFF_V_DIM
N_BUCKETS = 32
MAX_DISTANCE = 128
N_EXPERTS = 32
TOP_K = 4
D_EXPERT = D_MODEL
SWIGLU_LIMIT = 7.0
SWIGLU_ALPHA = 1.702
RMS_EPS = 1e-5
NEG_INF = -1e30
LAMBDA_INIT = 0.8 - 0.6 * math.exp(-0.3 * 0)
ATTN_SCALE = HEAD_DIM ** -0.5

LANES = 128
SUBLANES = 8
VMEM_LIMIT_BYTES = 56 * 1024 * 1024
MOE_VMEM_LIMIT_BYTES = 60 * 1024 * 1024

COL_FQ = 0
COL_FK = COL_FQ + FOX_WIDTH
COL_FV = COL_FK + FOX_WIDTH
COL_DQ = COL_FV + FOX_WIDTH
COL_DK = COL_DQ + DIFF_QK_WIDTH
COL_DV = COL_DK + DIFF_QK_WIDTH
COL_GF = COL_DV + DIFF_WIDTH
COL_GD = COL_GF + D_MODEL
D_PROJ = COL_GD + D_MODEL

ATTN_TILE = 512
N_QBLK = SEQ // ATTN_TILE

MOE_ROWS = 1280
MOE_SUB = 256
MOE_HALF = MOE_SUB // 2
MOE_NSUB = MOE_ROWS // MOE_SUB
MOE_TF = 256
MOE_NF = D_EXPERT // MOE_TF
MOE_CHUNK = MOE_ROWS // (MOE_NF * MOE_NSUB)
N_ASSIGN = N_TOK * TOP_K
MOE_ITEMS = N_EXPERTS + N_ASSIGN // MOE_ROWS
Y_ROWS = N_ASSIGN + MOE_ROWS

f32 = jnp.float32
bf16 = jnp.bfloat16


def _params(sem, vmem=VMEM_LIMIT_BYTES):
    return pltpu.CompilerParams(dimension_semantics=sem, vmem_limit_bytes=vmem)


def _inproj_kernel(x_ref, g_ref, w_ref, wff_ref, o_ref, ff_ref, xn_ref):
    @pl.when(pl.program_id(1) == 0)
    def _():
        x = x_ref[...]
        ms = jnp.mean(x * x, axis=-1, keepdims=True)
        xn_ref[...] = (x * lax.rsqrt(ms + RMS_EPS) * g_ref[...]).astype(bf16)
        ff_ref[...] = jnp.dot(xn_ref[...], wff_ref[...], preferred_element_type=f32)

    o_ref[...] = jnp.dot(xn_ref[...], w_ref[...], preferred_element_type=f32).astype(bf16)


def _inproj(x2d, gain, w_main, w_ff, tm, tn):
    n = x2d.shape[0]
    return pl.pallas_call(
        _inproj_kernel,
        grid=(n // tm, D_PROJ // tn),
        in_specs=[
            pl.BlockSpec((tm, D_MODEL), lambda i, j: (i, 0)),
            pl.BlockSpec((1, D_MODEL), lambda i, j: (0, 0)),
            pl.BlockSpec((D_MODEL, tn), lambda i, j: (0, j)),
            pl.BlockSpec((D_MODEL, LANES), lambda i, j: (0, 0)),
        ],
        out_specs=[
            pl.BlockSpec((tm, tn), lambda i, j: (i, j)),
            pl.BlockSpec((tm, LANES), lambda i, j: (i, 0)),
        ],
        out_shape=[
            jax.ShapeDtypeStruct((n, D_PROJ), bf16),
            jax.ShapeDtypeStruct((n, LANES), f32),
        ],
        scratch_shapes=[pltpu.VMEM((tm, D_MODEL), bf16)],
        compiler_params=_params(("parallel", "arbitrary")),
        name="inproj",
    )(x2d, gain, w_main, w_ff)


def _log_sigmoid(x):
    return jnp.minimum(x, 0.0) - jnp.log(1.0 + jnp.exp(-jnp.abs(x)))


def _lane_cumsum(x):
    n = x.shape[-1]
    lane = lax.broadcasted_iota(jnp.int32, x.shape, x.ndim - 1)
    s = 1
    while s < n:
        x = x + jnp.where(lane >= s, pltpu.roll(x, s, x.ndim - 1), 0.0)
        s *= 2
    return x


def _cum_kernel(ff_ref, ffm_ref, fb_ref, ck_ref, ckm_ref):
    fb = fb_ref[...]
    row = lax.broadcasted_iota(jnp.int32, (LANES, LANES), 0)
    lfm = jnp.where(row < N_META, _log_sigmoid(ffm_ref[...] + fb), 0.0)
    cm = _lane_cumsum(lfm.T)
    m_total = cm[:, N_META - 1:N_META]
    lf = _log_sigmoid(ff_ref[...] + fb)
    cr = _lane_cumsum(lf.T) + m_total
    ck_ref[...] = cr[:SUBLANES]
    ckm_ref[...] = cm[:SUBLANES]


def _forget_cumsum(ff_real, ff_meta_pad, fb_pad):
    return pl.pallas_call(
        _cum_kernel,
        grid=(BATCH,),
        in_specs=[
            pl.BlockSpec((SEQ, LANES), lambda b: (b, 0)),
            pl.BlockSpec((LANES, LANES), lambda b: (0, 0)),
            pl.BlockSpec((1, LANES), lambda b: (0, 0)),
        ],
        out_specs=[
            pl.BlockSpec((None, SUBLANES, SEQ), lambda b: (b, 0, 0)),
            pl.BlockSpec((None, SUBLANES, LANES), lambda b: (b, 0, 0)),
        ],
        out_shape=[
            jax.ShapeDtypeStruct((BATCH, FOX_HEADS, SEQ), f32),
            jax.ShapeDtypeStruct((BATCH, FOX_HEADS, LANES), f32),
        ],
        compiler_params=_params(("parallel",)),
        name="forget_cumsum",
    )(ff_real, ff_meta_pad, fb_pad)


def _qk(q, k):
    return lax.dot_general(q, k, (((1,), (1,)), ((), ())), preferred_element_type=f32)


def _row_to_col(row):
    t = row.shape[-1]
    r = lax.broadcasted_iota(jnp.int32, (t, t), 0)
    c = lax.broadcasted_iota(jnp.int32, (t, t), 1)
    return jnp.sum(jnp.where(r == c, row, 0.0), axis=1, keepdims=True)


def _online_update(carry, s, v):
    m, l, acc = carry
    m_new = jnp.maximum(m, jnp.max(s, axis=-1, keepdims=True))
    alpha = jnp.exp(m - m_new)
    p = jnp.exp(s - m_new)
    l = alpha * l + jnp.sum(p, axis=-1, keepdims=True)
    acc = alpha * acc + jnp.dot(p.astype(bf16), v, preferred_element_type=f32)
    return m_new, l, acc


def _first_update(s, v):
    m = jnp.max(s, axis=-1, keepdims=True)
    p = jnp.exp(s - m)
    l = jnp.sum(p, axis=-1, keepdims=True)
    acc = jnp.dot(p.astype(bf16), v, preferred_element_type=f32)
    return m, l, acc


def _fox_kernel(q_ref, k_ref, v_ref, km_ref, vm_ref, ck_ref, ckm_ref, o_ref):
    h = pl.program_id(1)
    i = pl.program_id(2)
    t = ATTN_TILE
    q = q_ref[...]
    q0 = pl.multiple_of(i * t, t)
    cq = _row_to_col(ck_ref[pl.ds(h, 1), pl.ds(q0, t)])

    ckm = ckm_ref[pl.ds(h, 1), :]
    meta_ok = lax.broadcasted_iota(jnp.int32, (t, N_META_PAD), 1) < N_META
    s = _qk(q, km_ref[...]) * ATTN_SCALE + (cq - ckm)
    carry = _first_update(jnp.where(meta_ok, s, NEG_INF), vm_ref[...])

    def body(j, carry):
        k0 = pl.multiple_of(j * t, t)
        ck = ck_ref[pl.ds(h, 1), pl.ds(k0, t)]
        s = _qk(q, k_ref[pl.ds(k0, t), :]) * ATTN_SCALE + (cq - ck)
        return _online_update(carry, s, v_ref[pl.ds(k0, t), :])

    carry = lax.fori_loop(0, i, body, carry)

    ck = ck_ref[pl.ds(h, 1), pl.ds(q0, t)]
    s = _qk(q, k_ref[pl.ds(q0, t), :]) * ATTN_SCALE + (cq - ck)
    r = lax.broadcasted_iota(jnp.int32, (t, t), 0)
    c = lax.broadcasted_iota(jnp.int32, (t, t), 1)
    s = jnp.where(c <= r, s, NEG_INF)
    m, l, acc = _online_update(carry, s, v_ref[pl.ds(q0, t), :])
    o_ref[...] = (acc / l).astype(bf16)


def _fox_attention(proj, proj_meta, ck, ckm):
    t = ATTN_TILE
    cb = lambda col: col // HEAD_DIM
    return pl.pallas_call(
        _fox_kernel,
        grid=(BATCH, FOX_HEADS, N_QBLK),
        in_specs=[
            pl.BlockSpec((t, HEAD_DIM), lambda b, h, i: (b * N_QBLK + i, cb(COL_FQ) + h)),
            pl.BlockSpec((SEQ, HEAD_DIM), lambda b, h, i: (b, cb(COL_FK) + h)),
            pl.BlockSpec((SEQ, HEAD_DIM), lambda b, h, i: (b, cb(COL_FV) + h)),
            pl.BlockSpec((N_META_PAD, HEAD_DIM), lambda b, h, i: (0, cb(COL_FK) + h)),
            pl.BlockSpec((N_META_PAD, HEAD_DIM), lambda b, h, i: (0, cb(COL_FV) + h)),
            pl.BlockSpec((None, FOX_HEADS, SEQ), lambda b, h, i: (b, 0, 0)),
            pl.BlockSpec((None, FOX_HEADS, LANES), lambda b, h, i: (b, 0, 0)),
        ],
        out_specs=pl.BlockSpec((t, HEAD_DIM), lambda b, h, i: (b * N_QBLK + i, h)),
        out_shape=jax.ShapeDtypeStruct((N_TOK, FOX_WIDTH), bf16),
        compiler_params=_params(("parallel", "parallel", "arbitrary")),
        name="fox_attention",
    )(proj, proj, proj, proj_meta, proj_meta, ck, ckm)


def _t5_bias(dist, table_ref, h):
    n = jnp.maximum(dist, 0)
    max_exact = N_BUCKETS // 2
    log_part = jnp.log(jnp.maximum(n, 1).astype(f32) / max_exact) / math.log(MAX_DISTANCE / max_exact)
    large = max_exact + (log_part * (N_BUCKETS - max_exact)).astype(jnp.int32)
    large = jnp.minimum(large, N_BUCKETS - 1)
    bucket = jnp.where(n < max_exact, n, large)
    far = table_ref[N_BUCKETS - 1, h]
    out = jnp.zeros(dist.shape, f32)
    for b in range(N_BUCKETS - 1):
        out = jnp.where(bucket == b, table_ref[b, h] - far, out)
    return out


def _bias_kernel(table_ref, near_ref, meta_ref):
    h = pl.program_id(0)
    t = ATTN_TILE
    r = lax.broadcasted_iota(jnp.int32, (t, 2 * t), 0)
    c = lax.broadcasted_iota(jnp.int32, (t, 2 * t), 1)
    near_ref[...] = _t5_bias(r + t - c, table_ref, h)
    r = lax.broadcasted_iota(jnp.int32, (t, LANES), 0)
    c = lax.broadcasted_iota(jnp.int32, (t, LANES), 1)
    meta_ref[...] = _t5_bias(N_META + r - c, table_ref, h)


def _bias_tiles(table):
    t = ATTN_TILE
    return pl.pallas_call(
        _bias_kernel,
        grid=(DIFF_HEADS,),
        in_specs=[pl.BlockSpec(memory_space=pltpu.SMEM)],
        out_specs=[
            pl.BlockSpec((None, t, 2 * t), lambda h: (h, 0, 0)),
            pl.BlockSpec((None, t, LANES), lambda h: (h, 0, 0)),
        ],
        out_shape=[
            jax.ShapeDtypeStruct((DIFF_HEADS, t, 2 * t), f32),
            jax.ShapeDtypeStruct((DIFF_HEADS, t, LANES), f32),
        ],
        compiler_params=_params(("arbitrary",)),
        name="t5_bias_tiles",
    )(table)


def _diff_kernel(q1_ref, q2_ref, k1_ref, k2_ref, v_ref, k1m_ref, k2m_ref, vm_ref,
                 near_ref, mbias_ref, lam_ref, subln_ref, o_ref):
    i = pl.program_id(2)
    t = ATTN_TILE
    q1 = q1_ref[...]
    q2 = q2_ref[...]
    q0 = pl.multiple_of(i * t, t)
    lam = (jnp.exp(jnp.sum(lam_ref[0:1, :] * lam_ref[1:2, :], axis=-1, keepdims=True))
           - jnp.exp(jnp.sum(lam_ref[2:3, :] * lam_ref[3:4, :], axis=-1, keepdims=True))
           + LAMBDA_INIT)

    mb = jnp.where(i == 0, mbias_ref[...], 0.0)
    meta_ok = lax.broadcasted_iota(jnp.int32, (t, N_META_PAD), 1) < N_META
    vm = vm_ref[...]
    c1 = _first_update(jnp.where(meta_ok, _qk(q1, k1m_ref[...]) * ATTN_SCALE + mb, NEG_INF), vm)
    c2 = _first_update(jnp.where(meta_ok, _qk(q2, k2m_ref[...]) * ATTN_SCALE + mb, NEG_INF), vm)

    def far_body(j, carry):
        c1, c2 = carry
        k0 = pl.multiple_of(j * t, t)
        v = v_ref[pl.ds(k0, t), :]
        c1 = _online_update(c1, _qk(q1, k1_ref[pl.ds(k0, t), :]) * ATTN_SCALE, v)
        c2 = _online_update(c2, _qk(q2, k2_ref[pl.ds(k0, t), :]) * ATTN_SCALE, v)
        return c1, c2

    c1, c2 = lax.fori_loop(0, jnp.maximum(i - 1, 0), far_body, (c1, c2))

    def prev_block(carry):
        c1, c2 = carry
        k0 = pl.multiple_of((i - 1) * t, t)
        v = v_ref[pl.ds(k0, t), :]
        bias = near_ref[:, :t]
        c1 = _online_update(c1, _qk(q1, k1_ref[pl.ds(k0, t), :]) * ATTN_SCALE + bias, v)
        c2 = _online_update(c2, _qk(q2, k2_ref[pl.ds(k0, t), :]) * ATTN_SCALE + bias, v)
        return c1, c2

    c1, c2 = lax.cond(i > 0, prev_block, lambda carry: carry, (c1, c2))

    r = lax.broadcasted_iota(jnp.int32, (t, t), 0)
    c = lax.broadcasted_iota(jnp.int32, (t, t), 1)
    mask = c <= r
    bias = near_ref[:, t:]
    v = v_ref[pl.ds(q0, t), :]
    s1 = jnp.where(mask, _qk(q1, k1_ref[pl.ds(q0, t), :]) * ATTN_SCALE + bias, NEG_INF)
    s2 = jnp.where(mask, _qk(q2, k2_ref[pl.ds(q0, t), :]) * ATTN_SCALE + bias, NEG_INF)
    _, l1, a1 = _online_update(c1, s1, v)
    _, l2, a2 = _online_update(c2, s2, v)

    o = a1 / l1 - lam * (a2 / l2)
    y = o * lax.rsqrt(jnp.mean(o * o, axis=-1, keepdims=True) + RMS_EPS) * subln_ref[...]
    o_ref[...] = (y * (1.0 - LAMBDA_INIT)).astype(bf16)


def _diff_attention(proj, proj_meta, near, mbias, lam_vecs, subln):
    t = ATTN_TILE
    cb = lambda col: col // HEAD_DIM
    vb = lambda col: col // DIFF_V_DIM
    row = lambda b, h, i: b * N_QBLK + i
    return pl.pallas_call(
        _diff_kernel,
        grid=(BATCH, DIFF_HEADS, N_QBLK),
        in_specs=[
            pl.BlockSpec((t, HEAD_DIM), lambda b, h, i: (row(b, h, i), cb(COL_DQ) + 2 * h)),
            pl.BlockSpec((t, HEAD_DIM), lambda b, h, i: (row(b, h, i), cb(COL_DQ) + 2 * h + 1)),
            pl.BlockSpec((SEQ, HEAD_DIM), lambda b, h, i: (b, cb(COL_DK) + 2 * h)),
            pl.BlockSpec((SEQ, HEAD_DIM), lambda b, h, i: (b, cb(COL_DK) + 2 * h + 1)),
            pl.BlockSpec((SEQ, DIFF_V_DIM), lambda b, h, i: (b, vb(COL_DV) + h)),
            pl.BlockSpec((N_META_PAD, HEAD_DIM), lambda b, h, i: (0, cb(COL_DK) + 2 * h)),
            pl.BlockSpec((N_META_PAD, HEAD_DIM), lambda b, h, i: (0, cb(COL_DK) + 2 * h + 1)),
            pl.BlockSpec((N_META_PAD, DIFF_V_DIM), lambda b, h, i: (0, vb(COL_DV) + h)),
            pl.BlockSpec((None, t, 2 * t), lambda b, h, i: (h, 0, 0)),
            pl.BlockSpec((None, t, LANES), lambda b, h, i: (h, 0, 0)),
            pl.BlockSpec((4, HEAD_DIM), lambda b, h, i: (0, 0)),
            pl.BlockSpec((1, DIFF_V_DIM), lambda b, h, i: (0, 0)),
        ],
        out_specs=pl.BlockSpec((t, DIFF_V_DIM), lambda b, h, i: (row(b, h, i), h)),
        out_shape=jax.ShapeDtypeStruct((N_TOK, DIFF_WIDTH), bf16),
        compiler_params=_params(("parallel", "parallel", "arbitrary")),
        name="diff_attention",
    )(proj, proj, proj, proj, proj, proj_meta, proj_meta, proj_meta, near, mbias, lam_vecs, subln)


def _merge_kernel(of_ref, od_ref, wf_ref, wd_ref, gf_ref, gd_ref, o_ref):
    yf = jnp.dot(of_ref[...], wf_ref[...], preferred_element_type=f32)
    yd = jnp.dot(od_ref[...], wd_ref[...], preferred_element_type=f32)
    gf = jax.nn.sigmoid(gf_ref[...].astype(f32))
    gd = jax.nn.sigmoid(gd_ref[...].astype(f32))
    o_ref[...] = (gf * yf + gd * yd).astype(bf16)


def _gated_merge(o_fox, o_diff, w_bf, w_bd, proj, tm=1024, tn=512):
    return pl.pallas_call(
        _merge_kernel,
        grid=(D_MODEL // tn, N_TOK // tm),
        in_specs=[
            pl.BlockSpec((tm, FOX_WIDTH), lambda j, i: (i, 0)),
            pl.BlockSpec((tm, DIFF_WIDTH), lambda j, i: (i, 0)),
            pl.BlockSpec((FOX_WIDTH, tn), lambda j, i: (0, j)),
            pl.BlockSpec((DIFF_WIDTH, tn), lambda j, i: (0, j)),
            pl.BlockSpec((tm, tn), lambda j, i: (i, COL_GF // tn + j)),
            pl.BlockSpec((tm, tn), lambda j, i: (i, COL_GD // tn + j)),
        ],
        out_specs=pl.BlockSpec((tm, tn), lambda j, i: (i, j)),
        out_shape=jax.ShapeDtypeStruct((N_TOK, D_MODEL), bf16),
        compiler_params=_params(("parallel", "arbitrary")),
        name="gated_merge",
    )(o_fox, o_diff, w_bf, w_bd, proj, proj)


def _outproj_kernel(m_ref, w_ref, x_ref, g_ref, wr_ref, br_ref, h_ref, hn_ref, lg_ref):
    h1 = x_ref[...] + jnp.dot(m_ref[...], w_ref[...], preferred_element_type=f32)
    h_ref[...] = h1
    hn = h1 * lax.rsqrt(jnp.mean(h1 * h1, axis=-1, keepdims=True) + RMS_EPS) * g_ref[...]
    hn_ref[...] = hn
    hn_hi = hn.astype(bf16)
    hn_lo = (hn - hn_hi.astype(f32)).astype(bf16)
    a = jnp.dot(hn_hi, wr_ref[...], preferred_element_type=f32)
    b = jnp.dot(hn_lo, wr_ref[:, :LANES], preferred_element_type=f32)
    lg_ref[...] = a[:, :LANES] + a[:, LANES:] + b + br_ref[...]


def _outproj(merged, w_out, x2d, gain, w_router_split, b_router_pad, tm=512):
    return pl.pallas_call(
        _outproj_kernel,
        grid=(N_TOK // tm,),
        in_specs=[
            pl.BlockSpec((tm, D_MODEL), lambda i: (i, 0)),
            pl.BlockSpec((D_MODEL, D_MODEL), lambda i: (0, 0)),
            pl.BlockSpec((tm, D_MODEL), lambda i: (i, 0)),
            pl.BlockSpec((1, D_MODEL), lambda i: (0, 0)),
            pl.BlockSpec((D_MODEL, 2 * LANES), lambda i: (0, 0)),
            pl.BlockSpec((1, LANES), lambda i: (0, 0)),
        ],
        out_specs=[
            pl.BlockSpec((tm, D_MODEL), lambda i: (i, 0)),
            pl.BlockSpec((tm, D_MODEL), lambda i: (i, 0)),
            pl.BlockSpec((tm, LANES), lambda i: (i, 0)),
        ],
        out_shape=[
            jax.ShapeDtypeStruct((N_TOK, D_MODEL), f32),
            jax.ShapeDtypeStruct((N_TOK, D_MODEL), f32),
            jax.ShapeDtypeStruct((N_TOK, LANES), f32),
        ],
        compiler_params=_params(("parallel",)),
        name="outproj_router",
    )(merged, w_out, x2d, gain, w_router_split, b_router_pad)


def _moe_kernel(item_e_ref, item_start_ref, item_n_ref, tok_ref, dest_ref,
                hn_hbm, wg_ref, wl_ref, bg_ref, bl_ref, wd_ref, bd_ref, y_hbm,
                acc_ref, xg_ref, xb_ref, wgb_ref, wlb_ref, wdb_ref, gsem, ssem):
    w = pl.program_id(0)
    t = pl.program_id(1)
    last_w = MOE_ITEMS - 1
    n = item_n_ref[w]
    start = item_start_ref[w]
    slot = lax.rem(w, 2)
    w_next = jnp.minimum(w + 1, last_w)
    start_next = item_start_ref[w_next]
    w_prev = jnp.maximum(w - 1, 0)
    start_prev = item_start_ref[w_prev]
    n_prev = jnp.where(w > 0, item_n_ref[w_prev], 0)
    is_last_live = jnp.logical_or(w == last_w, item_n_ref[w_next] == 0)

    def gather_row(row, base):
        tk = tok_ref[base + row]
        pltpu.make_async_copy(hn_hbm.at[pl.ds(tk, 1), :], xg_ref.at[pl.ds(row, 1), :], gsem).start()

    def gather_wait():
        pltpu.make_async_copy(hn_hbm.at[pl.ds(0, MOE_ROWS), :], xg_ref, gsem).wait()

    def scatter_row(row, base, cnt, sl):
        d = jnp.where(row < cnt, dest_ref[base + row], N_ASSIGN + row)
        pltpu.make_async_copy(acc_ref.at[sl, pl.ds(row, 1), :], y_hbm.at[pl.ds(d, 1), :], ssem).start()

    def scatter_wait(sl):
        pltpu.make_async_copy(acc_ref.at[sl], y_hbm.at[pl.ds(0, MOE_ROWS), :], ssem).wait()

    @pl.when(jnp.logical_and(w == 0, t == 0))
    def _():
        acc_ref[1] = jnp.zeros((MOE_ROWS, D_MODEL), f32)

        def issue(r, c):
            gather_row(r, start)
            return c
        lax.fori_loop(0, MOE_ROWS, issue, 0, unroll=8)

    @pl.when(jnp.logical_and(t == 0, n > 0))
    def _():
        gather_wait()
        xb_ref[...] = xg_ref[...].astype(bf16)
        acc_ref[slot] = jnp.zeros((MOE_ROWS, D_MODEL), f32)

    def dma_chunk(r):
        row0 = t * (MOE_NSUB * MOE_CHUNK) + r * MOE_CHUNK
        for k in range(MOE_CHUNK):
            gather_row(row0 + k, start_next)
        for k in range(MOE_CHUNK):
            scatter_row(row0 + k, start_prev, n_prev, 1 - slot)

    def ffn_up(r0, rows):
        xs = xb_ref[r0:r0 + rows, :]
        hg = jnp.dot(xs, wgb_ref[...], preferred_element_type=f32) + bg_ref[...]
        hl = jnp.dot(xs, wlb_ref[...], preferred_element_type=f32) + bl_ref[...]
        return hg, hl

    def ffn_down(r0, rows, hg, hl):
        hg = jnp.minimum(hg, SWIGLU_LIMIT)
        hl = jnp.clip(hl, -SWIGLU_LIMIT, SWIGLU_LIMIT)
        act = hg * jax.nn.sigmoid(SWIGLU_ALPHA * hg) * (hl + 1.0)
        acc_ref[slot, r0:r0 + rows, :] += jnp.dot(act.astype(bf16), wdb_ref[...], preferred_element_type=f32)

    def ffn_rows(r0, rows):
        ffn_down(r0, rows, *ffn_up(r0, rows))

    @pl.when(n > 0)
    def _():
        wgb_ref[...] = wg_ref[...].astype(bf16)
        wlb_ref[...] = wl_ref[...].astype(bf16)
        wdb_ref[...] = wd_ref[...].astype(bf16)
        def one_block(r):
            r0 = r * MOE_SUB

            def full():
                dma_chunk(r)
                ffn_rows(r0, MOE_SUB)

            def half():
                dma_chunk(r)
                ffn_rows(r0, MOE_HALF)

            def idle():
                dma_chunk(r)

            lax.cond(n > r0 + MOE_HALF, full, lambda: lax.cond(n > r0, half, idle))

        for r in range(0, MOE_NSUB - 1, 2):
            def pair(r=r):
                dma_chunk(r)
                dma_chunk(r + 1)
                ra, rb = r * MOE_SUB, (r + 1) * MOE_SUB
                ha = ffn_up(ra, MOE_SUB)
                hb = ffn_up(rb, MOE_SUB)
                ffn_down(ra, MOE_SUB, *ha)
                ffn_down(rb, MOE_SUB, *hb)

            def singles(r=r):
                one_block(r)
                one_block(r + 1)

            lax.cond(n > (r + 1) * MOE_SUB + MOE_HALF, pair, singles)
        for r in range(MOE_NSUB - MOE_NSUB % 2, MOE_NSUB):
            one_block(r)

    @pl.when(jnp.logical_and(t == MOE_NF - 1, n > 0))
    def _():
        scatter_wait(1 - slot)
        acc_ref[slot] = acc_ref[slot] + bd_ref[...]

        @pl.when(is_last_live)
        def _():
            def issue(r, c):
                scatter_row(r, start, n, slot)
                return c
            lax.fori_loop(0, MOE_ROWS, issue, 0, unroll=8)
            scatter_wait(slot)
            gather_wait()


def _moe(item_e, item_start, item_n, tok_sorted, dest_sorted, hn2, w_gate_up, b_gate_up, w_down, b_down):
    tf = MOE_TF
    live_t = lambda w, t, n: jnp.where(n[w] > 0, t, MOE_NF - 1)
    grid_spec = pltpu.PrefetchScalarGridSpec(
        num_scalar_prefetch=5,
        grid=(MOE_ITEMS, MOE_NF),
        in_specs=[
            pl.BlockSpec(memory_space=pl.ANY),
            pl.BlockSpec((None, D_MODEL, tf), lambda w, t, e, s, n, tk, ds: (e[w], 0, live_t(w, t, n))),
            pl.BlockSpec((None, D_MODEL, tf), lambda w, t, e, s, n, tk, ds: (e[w], 0, MOE_NF + live_t(w, t, n))),
            pl.BlockSpec((None, 1, tf), lambda w, t, e, s, n, tk, ds: (e[w], 0, live_t(w, t, n))),
            pl.BlockSpec((None, 1, tf), lambda w, t, e, s, n, tk, ds: (e[w], 0, MOE_NF + live_t(w, t, n))),
            pl.BlockSpec((None, tf, D_MODEL), lambda w, t, e, s, n, tk, ds: (e[w], live_t(w, t, n), 0)),
            pl.BlockSpec((None, 1, D_MODEL), lambda w, t, e, s, n, tk, ds: (e[w], 0, 0)),
        ],
        out_specs=pl.BlockSpec(memory_space=pl.ANY),
        scratch_shapes=[
            pltpu.VMEM((2, MOE_ROWS, D_MODEL), f32),
            pltpu.VMEM((MOE_ROWS, D_MODEL), f32),
            pltpu.VMEM((MOE_ROWS, D_MODEL), bf16),
            pltpu.VMEM((D_MODEL, tf), bf16),
            pltpu.VMEM((D_MODEL, tf), bf16),
            pltpu.VMEM((tf, D_MODEL), bf16),
            pltpu.SemaphoreType.DMA,
            pltpu.SemaphoreType.DMA,
        ],
    )
    return pl.pallas_call(
        _moe_kernel,
        grid_spec=grid_spec,
        out_shape=jax.ShapeDtypeStruct((Y_ROWS, D_MODEL), f32),
        compiler_params=_params(("arbitrary", "arbitrary"), MOE_VMEM_LIMIT_BYTES),
        name="moe_experts",
    )(item_e, item_start, item_n, tok_sorted, dest_sorted,
      hn2, w_gate_up, w_gate_up, b_gate_up, b_gate_up, w_down, b_down)


def _combine_kernel(y0_ref, y1_ref, y2_ref, y3_ref, h_ref, gate_ref, g_ref, o_ref):
    h = h_ref[...]
    gates = gate_ref[...]
    for k, y_ref in enumerate((y0_ref, y1_ref, y2_ref, y3_ref)):
        h = h + y_ref[...] * gates[:, k:k + 1]
    o_ref[...] = h * lax.rsqrt(jnp.mean(h * h, axis=-1, keepdims=True) + RMS_EPS) * g_ref[...]


def _combine(y, h1, gates, gain, tm=256):
    slot_spec = lambda k: pl.BlockSpec((tm, D_MODEL), lambda i, k=k: (k * (N_TOK // tm) + i, 0))
    return pl.pallas_call(
        _combine_kernel,
        grid=(N_TOK // tm,),
        in_specs=[slot_spec(k) for k in range(TOP_K)] + [
            pl.BlockSpec((tm, D_MODEL), lambda i: (i, 0)),
            pl.BlockSpec((tm, TOP_K), lambda i: (i, 0)),
            pl.BlockSpec((1, D_MODEL), lambda i: (0, 0)),
        ],
        out_specs=pl.BlockSpec((tm, D_MODEL), lambda i: (i, 0)),
        out_shape=jax.ShapeDtypeStruct((N_TOK, D_MODEL), f32),
        compiler_params=_params(("parallel",)),
        name="combine_norm",
    )(y, y, y, y, h1, gates, gain)


def _route(logits):
    top_vals, top_idx = lax.top_k(logits, TOP_K)
    gates = jax.nn.softmax(top_vals, axis=-1)
    expert_flat = top_idx.reshape(-1).astype(jnp.int32)
    order = jnp.argsort(expert_flat, stable=True).astype(jnp.int32)
    counts = jnp.zeros((N_EXPERTS,), jnp.int32).at[expert_flat].add(1)
    starts = jnp.cumsum(counts) - counts
    tok_sorted = order // TOP_K
    dest_sorted = (order % TOP_K) * N_TOK + tok_sorted
    tok_sorted = jnp.pad(tok_sorted, (0, MOE_ROWS))
    dest_sorted = jnp.pad(dest_sorted, (0, MOE_ROWS))
    chunks = (counts + MOE_ROWS - 1) // MOE_ROWS
    chunk_end = jnp.cumsum(chunks)
    n_items = chunk_end[-1]
    w = jnp.arange(MOE_ITEMS, dtype=jnp.int32)
    w_eff = jnp.minimum(w, n_items - 1)
    e = jnp.sum(w_eff[:, None] >= chunk_end[None, :], axis=1).astype(jnp.int32)
    c = w_eff - (chunk_end - chunks)[e]
    item_start = starts[e] + c * MOE_ROWS
    item_n = jnp.where(w < n_items, jnp.clip(counts[e] - c * MOE_ROWS, 0, MOE_ROWS), 0)
    return gates, e, item_start.astype(jnp.int32), item_n.astype(jnp.int32), tok_sorted, dest_sorted


def kernel(x, meta_tokens, rel_bias_table, attn_norm, w_in, fox_forget_bias, lam_q1, lam_k1, lam_q2, lam_k2,
           diff_subln, w_branch_fox, w_branch_diff, w_out, ffn_norm, w_router, b_router, w_gate_up, b_gate_up,
           w_down, b_down, final_norm):
    x2d = x.reshape(N_TOK, D_MODEL)
    w_in0 = w_in[0]
    ff_lo = 3 * FOX_WIDTH
    w_main = jnp.concatenate([w_in0[:, :ff_lo], w_in0[:, ff_lo + FOX_HEADS:]], axis=1).astype(bf16)
    w_ff = jnp.pad(w_in0[:, ff_lo:ff_lo + FOX_HEADS], ((0, 0), (0, LANES - FOX_HEADS))).astype(bf16)
    fb_pad = jnp.pad(fox_forget_bias[0], (0, LANES - FOX_HEADS)).reshape(1, LANES)

    proj, ff_real = _inproj(x2d, attn_norm, w_main, w_ff, tm=1024, tn=512)
    meta_pad = jnp.pad(meta_tokens, ((0, N_META_PAD - N_META), (0, 0)))
    proj_meta, ff_meta = _inproj(meta_pad, attn_norm, w_main, w_ff, tm=N_META_PAD, tn=2048)

    ck, ckm = _forget_cumsum(ff_real, ff_meta, fb_pad)
    o_fox = _fox_attention(proj, proj_meta, ck, ckm)

    near, mbias = _bias_tiles(rel_bias_table)
    lam_vecs = jnp.concatenate([lam_q1, lam_k1, lam_q2, lam_k2], axis=0)
    o_diff = _diff_attention(proj, proj_meta, near, mbias, lam_vecs, diff_subln)

    merged = _gated_merge(o_fox, o_diff, w_branch_fox[0].astype(bf16), w_branch_diff[0].astype(bf16), proj)
    w_router_pad = jnp.pad(w_router[0], ((0, 0), (0, LANES - N_EXPERTS)))
    w_router_hi = w_router_pad.astype(bf16)
    w_router_lo = (w_router_pad - w_router_hi.astype(f32)).astype(bf16)
    w_router_split = jnp.concatenate([w_router_hi, w_router_lo], axis=1)
    b_router_pad = jnp.pad(b_router[0], (0, LANES - N_EXPERTS)).reshape(1, LANES)
    h1, hn2, logits = _outproj(merged, w_out[0].astype(bf16), x2d, ffn_norm, w_router_split, b_router_pad)

    gates, item_e, item_start, item_n, tok_sorted, dest_sorted = _route(logits[:, :N_EXPERTS])
    y = _moe(item_e, item_start, item_n, tok_sorted, dest_sorted, hn2,
             w_gate_up[0], b_gate_up[0].reshape(N_EXPERTS, 1, 2 * D_EXPERT),
             w_down[0], b_down[0].reshape(N_EXPERTS, 1, D_MODEL))
    out = _combine(y, h1, gates, final_norm.reshape(1, D_MODEL))
    return out.reshape(BATCH, SEQ, D_MODEL)
```

```python
import functools
import math

import jax
import jax.numpy as jnp
from jax import lax
from jax.experimental import pallas as pl
from jax.experimental.pallas import tpu as pltpu

D_MODEL = 2048
BATCH = 4
SEQ = 2048
N_TOK = BATCH * SEQ
N_META = 16
N_META_PAD = 128
HEAD_DIM = 128
FOX_HEADS = 8
DIFF_HEADS = 4
DIFF_V_DIM = 2 * HEAD_DIM
FOX_WIDTH = FOX_HEADS * HEAD_DIM
DIFF_QK_WIDTH = DIFF_HEADS * 2 * HEAD_DIM
DIFF_WIDTH = DIFF_HEADS * DIFF_V_DIM
N_BUCKETS = 32
MAX_DISTANCE = 128
N_EXPERTS = 32
TOP_K = 4
D_EXPERT = D_MODEL
SWIGLU_LIMIT = 7.0
SWIGLU_ALPHA = 1.702
RMS_EPS = 1e-5
NEG_INF = -1e30
LAMBDA_INIT = 0.8 - 0.6 * math.exp(-0.3 * 0)
ATTN_SCALE = HEAD_DIM ** -0.5
LOG2E = math.log2(math.e)
Q_PRESCALE = ATTN_SCALE * LOG2E

LANES = 128
SUBLANES = 8
VMEM_LIMIT_BYTES = 56 * 1024 * 1024
MOE_VMEM_LIMIT_BYTES = 60 * 1024 * 1024

COL_FQ = 0
COL_FK = COL_FQ + FOX_WIDTH
COL_FV = COL_FK + FOX_WIDTH
COL_DQ = COL_FV + FOX_WIDTH
COL_DK = COL_DQ + DIFF_QK_WIDTH
COL_DV = COL_DK + DIFF_QK_WIDTH
COL_GF = COL_DV + DIFF_WIDTH
COL_GD = COL_GF + D_MODEL
D_PROJ = COL_GD + D_MODEL

ATTN_TILE = 512
N_QBLK = SEQ // ATTN_TILE

MOE_ROWS = 1280
MOE_SUB = 256
MOE_HALF = MOE_SUB // 2
MOE_NSUB = MOE_ROWS // MOE_SUB
MOE_TF = 256
MOE_NF = D_EXPERT // MOE_TF
MOE_CHUNK = MOE_ROWS // (MOE_NF * MOE_NSUB)
N_ASSIGN = N_TOK * TOP_K
MOE_ITEMS = N_EXPERTS + N_ASSIGN // MOE_ROWS
Y_ROWS = N_ASSIGN + MOE_ROWS

f32 = jnp.float32
bf16 = jnp.bfloat16


def _params(sem, vmem=VMEM_LIMIT_BYTES):
    return pltpu.CompilerParams(dimension_semantics=sem, vmem_limit_bytes=vmem)


def _inproj_kernel(x_ref, g_ref, wa_ref, wb_ref, wff_ref, o_ref, ff_ref, xn_ref, *, tn):
    j = pl.program_id(1)

    @pl.when(j == 0)
    def _():
        x = x_ref[...]
        ms = jnp.mean(x * x, axis=-1, keepdims=True)
        xn_ref[...] = (x * lax.rsqrt(ms + RMS_EPS) * g_ref[...]).astype(bf16)
        ff_ref[...] = jnp.dot(xn_ref[...], wff_ref[...], preferred_element_type=f32)

    is_q = jnp.logical_or(j < COL_FK // tn, jnp.logical_and(j >= COL_DQ // tn, j < COL_DK // tn))
    qs = jnp.where(is_q, Q_PRESCALE, 1.0)

    @pl.when(j < COL_DQ // tn)
    def _():
        y = jnp.dot(xn_ref[...], wa_ref[...].astype(bf16), preferred_element_type=f32)
        o_ref[...] = (y * qs).astype(bf16)

    @pl.when(j >= COL_DQ // tn)
    def _():
        y = jnp.dot(xn_ref[...], wb_ref[...], preferred_element_type=f32)
        o_ref[...] = (y * qs).astype(bf16)


def _inproj(x2d, gain, w_in2d, w_rest, w_ff, tm, tn):
    n = x2d.shape[0]
    na = COL_DQ // tn
    return pl.pallas_call(
        functools.partial(_inproj_kernel, tn=tn),
        grid=(n // tm, D_PROJ // tn),
        in_specs=[
            pl.BlockSpec((tm, D_MODEL), lambda i, j: (i, 0)),
            pl.BlockSpec((1, D_MODEL), lambda i, j: (0, 0)),
            pl.BlockSpec((D_MODEL, tn), lambda i, j: (0, jnp.minimum(j, na - 1))),
            pl.BlockSpec((D_MODEL, tn), lambda i, j: (0, jnp.maximum(j - na, 0))),
            pl.BlockSpec((D_MODEL, LANES), lambda i, j: (0, 0)),
        ],
        out_specs=[
            pl.BlockSpec((tm, tn), lambda i, j: (i, j)),
            pl.BlockSpec((tm, LANES), lambda i, j: (i, 0)),
        ],
        out_shape=[
            jax.ShapeDtypeStruct((n, D_PROJ), bf16),
            jax.ShapeDtypeStruct((n, LANES), f32),
        ],
        scratch_shapes=[pltpu.VMEM((tm, D_MODEL), bf16)],
        compiler_params=_params(("parallel", "arbitrary")),
        name="inproj",
    )(x2d, gain, w_in2d, w_rest, w_ff)


def _log_sigmoid(x):
    return jnp.minimum(x, 0.0) - jnp.log(1.0 + jnp.exp(-jnp.abs(x)))


def _lane_cumsum(x):
    n = x.shape[-1]
    lane = lax.broadcasted_iota(jnp.int32, x.shape, x.ndim - 1)
    s = 1
    while s < n:
        x = x + jnp.where(lane >= s, pltpu.roll(x, s, x.ndim - 1), 0.0)
        s *= 2
    return x


def _cum_kernel(ff_ref, ffm_ref, fb_ref, ck_ref, ckm_ref):
    fb = fb_ref[...]
    row = lax.broadcasted_iota(jnp.int32, (LANES, LANES), 0)
    lfm = jnp.where(row < N_META, _log_sigmoid(ffm_ref[...] + fb), 0.0)
    cm = _lane_cumsum(lfm.T)
    m_total = cm[:, N_META - 1:N_META]
    lf = _log_sigmoid(ff_ref[...] + fb)
    cr = _lane_cumsum(lf.T) + m_total
    ck_ref[...] = cr[:SUBLANES] * LOG2E
    ckm_ref[...] = cm[:SUBLANES] * LOG2E


def _forget_cumsum(ff_real, ff_meta_pad, fb_pad):
    return pl.pallas_call(
        _cum_kernel,
        grid=(BATCH,),
        in_specs=[
            pl.BlockSpec((SEQ, LANES), lambda b: (b, 0)),
            pl.BlockSpec((LANES, LANES), lambda b: (0, 0)),
            pl.BlockSpec((1, LANES), lambda b: (0, 0)),
        ],
        out_specs=[
            pl.BlockSpec((None, SUBLANES, SEQ), lambda b: (b, 0, 0)),
            pl.BlockSpec((None, SUBLANES, LANES), lambda b: (b, 0, 0)),
        ],
        out_shape=[
            jax.ShapeDtypeStruct((BATCH, FOX_HEADS, SEQ), f32),
            jax.ShapeDtypeStruct((BATCH, FOX_HEADS, LANES), f32),
        ],
        compiler_params=_params(("parallel",)),
        name="forget_cumsum",
    )(ff_real, ff_meta_pad, fb_pad)


def _qk(q, k):
    return lax.dot_general(q, k, (((1,), (1,)), ((), ())), preferred_element_type=f32)


def _row_to_col(row):
    t = row.shape[-1]
    r = lax.broadcasted_iota(jnp.int32, (t, t), 0)
    c = lax.broadcasted_iota(jnp.int32, (t, t), 1)
    return jnp.sum(jnp.where(r == c, row, 0.0), axis=1, keepdims=True)


def _online_update(carry, s, v):
    m, l, acc = carry
    m_new = jnp.maximum(m, jnp.max(s, axis=-1, keepdims=True))
    alpha = jnp.exp2(m - m_new)
    p = jnp.exp2(s - m_new)
    l = alpha * l + jnp.sum(p, axis=-1, keepdims=True)
    acc = alpha * acc + jnp.dot(p.astype(bf16), v, preferred_element_type=f32)
    return m_new, l, acc


def _first_update(s, v):
    m = jnp.max(s, axis=-1, keepdims=True)
    p = jnp.exp2(s - m)
    l = jnp.sum(p, axis=-1, keepdims=True)
    acc = jnp.dot(p.astype(bf16), v, preferred_element_type=f32)
    return m, l, acc


FOX_GROUP = 2


def _fox_update(carry, u, cq, v):
    m_blk = jnp.max(u, axis=-1, keepdims=True) + cq
    if carry is None:
        m_new = m_blk
    else:
        m, l, acc = carry
        m_new = jnp.maximum(m, m_blk)
    p = jnp.exp2(u - (m_new - cq))
    ps = jnp.sum(p, axis=-1, keepdims=True)
    pv = jnp.dot(p.astype(bf16), v, preferred_element_type=f32)
    if carry is None:
        return m_new, ps, pv
    alpha = jnp.exp2(m - m_new)
    return m_new, alpha * l + ps, alpha * acc + pv


def _fox_kernel(q_ref, k_ref, v_ref, km_ref, vm_ref, ck_ref, ckm_ref, o_ref):
    hg = pl.program_id(1)
    i = pl.program_id(2)
    t = ATTN_TILE
    q0 = pl.multiple_of(i * t, t)
    heads = range(FOX_GROUP)
    col = lambda g: slice(g * HEAD_DIM, (g + 1) * HEAD_DIM)
    ck_row = lambda g, k0: ck_ref[pl.ds(hg * FOX_GROUP + g, 1), pl.ds(k0, t)]
    qs = [q_ref[:, col(g)] for g in heads]
    cqs = [_row_to_col(ck_row(g, q0)) for g in heads]

    meta_ok = lax.broadcasted_iota(jnp.int32, (t, N_META_PAD), 1) < N_META
    us = [_qk(qs[g], km_ref[:, col(g)]) - ckm_ref[pl.ds(hg * FOX_GROUP + g, 1), :] for g in heads]
    carry = tuple(_fox_update(None, jnp.where(meta_ok, us[g], NEG_INF), cqs[g], vm_ref[:, col(g)]) for g in heads)

    def body(j, carry):
        k0 = pl.multiple_of(j * t, t)
        us = [_qk(qs[g], k_ref[pl.ds(k0, t), col(g)]) - ck_row(g, k0) for g in heads]
        return tuple(_fox_update(carry[g], us[g], cqs[g], v_ref[pl.ds(k0, t), col(g)]) for g in heads)

    carry = lax.fori_loop(0, i, body, carry)

    r = lax.broadcasted_iota(jnp.int32, (t, t), 0)
    c = lax.broadcasted_iota(jnp.int32, (t, t), 1)
    us = [_qk(qs[g], k_ref[pl.ds(q0, t), col(g)]) - ck_row(g, q0) for g in heads]
    for g in heads:
        _, l, acc = _fox_update(carry[g], jnp.where(c <= r, us[g], NEG_INF), cqs[g], v_ref[pl.ds(q0, t), col(g)])
        o_ref[:, col(g)] = (acc / l).astype(bf16)


def _fox_attention(proj, proj_meta, ck, ckm):
    t = ATTN_TILE
    w = FOX_GROUP * HEAD_DIM
    cb = lambda col: col // w
    return pl.pallas_call(
        _fox_kernel,
        grid=(BATCH, FOX_HEADS // FOX_GROUP, N_QBLK),
        in_specs=[
            pl.BlockSpec((t, w), lambda b, h, i: (b * N_QBLK + i, cb(COL_FQ) + h)),
            pl.BlockSpec((SEQ, w), lambda b, h, i: (b, cb(COL_FK) + h)),
            pl.BlockSpec((SEQ, w), lambda b, h, i: (b, cb(COL_FV) + h)),
            pl.BlockSpec((N_META_PAD, w), lambda b, h, i: (0, cb(COL_FK) + h)),
            pl.BlockSpec((N_META_PAD, w), lambda b, h, i: (0, cb(COL_FV) + h)),
            pl.BlockSpec((None, FOX_HEADS, SEQ), lambda b, h, i: (b, 0, 0)),
            pl.BlockSpec((None, FOX_HEADS, LANES), lambda b, h, i: (b, 0, 0)),
        ],
        out_specs=pl.BlockSpec((t, w), lambda b, h, i: (b * N_QBLK + i, h)),
        out_shape=jax.ShapeDtypeStruct((N_TOK, FOX_WIDTH), bf16),
        compiler_params=_params(("parallel", "parallel", "arbitrary")),
        name="fox_attention",
    )(proj, proj, proj, proj_meta, proj_meta, ck, ckm)


def _t5_bias(dist, table_ref, h):
    n = jnp.maximum(dist, 0)
    max_exact = N_BUCKETS // 2
    log_part = jnp.log(jnp.maximum(n, 1).astype(f32) / max_exact) / math.log(MAX_DISTANCE / max_exact)
    large = max_exact + (log_part * (N_BUCKETS - max_exact)).astype(jnp.int32)
    large = jnp.minimum(large, N_BUCKETS - 1)
    bucket = jnp.where(n < max_exact, n, large)
    far = table_ref[N_BUCKETS - 1, h]
    out = jnp.zeros(dist.shape, f32)
    for b in range(N_BUCKETS - 1):
        out = jnp.where(bucket == b, (table_ref[b, h] - far) * LOG2E, out)
    return out


def _bias_kernel(table_ref, near_ref, meta_ref):
    h = pl.program_id(0)
    t = ATTN_TILE
    r = lax.broadcasted_iota(jnp.int32, (t, 2 * t), 0)
    c = lax.broadcasted_iota(jnp.int32, (t, 2 * t), 1)
    near_ref[...] = _t5_bias(r + t - c, table_ref, h)
    r = lax.broadcasted_iota(jnp.int32, (t, LANES), 0)
    c = lax.broadcasted_iota(jnp.int32, (t, LANES), 1)
    meta_ref[...] = _t5_bias(N_META + r - c, table_ref, h)


def _bias_tiles(table):
    t = ATTN_TILE
    return pl.pallas_call(
        _bias_kernel,
        grid=(DIFF_HEADS,),
        in_specs=[pl.BlockSpec(memory_space=pltpu.SMEM)],
        out_specs=[
            pl.BlockSpec((None, t, 2 * t), lambda h: (h, 0, 0)),
            pl.BlockSpec((None, t, LANES), lambda h: (h, 0, 0)),
        ],
        out_shape=[
            jax.ShapeDtypeStruct((DIFF_HEADS, t, 2 * t), f32),
            jax.ShapeDtypeStruct((DIFF_HEADS, t, LANES), f32),
        ],
        compiler_params=_params(("arbitrary",)),
        name="t5_bias_tiles",
    )(table)


def _diff_kernel(q1_ref, q2_ref, k1_ref, k2_ref, v_ref, k1m_ref, k2m_ref, vm_ref,
                 near_ref, mbias_ref, lam_ref, subln_ref, o_ref):
    i = pl.program_id(2)
    t = ATTN_TILE
    q1 = q1_ref[...]
    q2 = q2_ref[...]
    q0 = pl.multiple_of(i * t, t)
    lam = (jnp.exp(jnp.sum(lam_ref[0:1, :] * lam_ref[1:2, :], axis=-1, keepdims=True))
           - jnp.exp(jnp.sum(lam_ref[2:3, :] * lam_ref[3:4, :], axis=-1, keepdims=True))
           + LAMBDA_INIT)

    mb = jnp.where(i == 0, mbias_ref[...], 0.0)
    meta_ok = lax.broadcasted_iota(jnp.int32, (t, N_META_PAD), 1) < N_META
    vm = vm_ref[...]
    c1 = _first_update(jnp.where(meta_ok, _qk(q1, k1m_ref[...]) + mb, NEG_INF), vm)
    c2 = _first_update(jnp.where(meta_ok, _qk(q2, k2m_ref[...]) + mb, NEG_INF), vm)

    def far_body(j, carry):
        c1, c2 = carry
        k0 = pl.multiple_of(j * t, t)
        v = v_ref[pl.ds(k0, t), :]
        c1 = _online_update(c1, _qk(q1, k1_ref[pl.ds(k0, t), :]), v)
        c2 = _online_update(c2, _qk(q2, k2_ref[pl.ds(k0, t), :]), v)
        return c1, c2

    c1, c2 = lax.fori_loop(0, jnp.maximum(i - 1, 0), far_body, (c1, c2))

    def prev_block(carry):
        c1, c2 = carry
        k0 = pl.multiple_of((i - 1) * t, t)
        v = v_ref[pl.ds(k0, t), :]
        bias = near_ref[:, :t]
        c1 = _online_update(c1, _qk(q1, k1_ref[pl.ds(k0, t), :]) + bias, v)
        c2 = _online_update(c2, _qk(q2, k2_ref[pl.ds(k0, t), :]) + bias, v)
        return c1, c2

    c1, c2 = lax.cond(i > 0, prev_block, lambda carry: carry, (c1, c2))

    r = lax.broadcasted_iota(jnp.int32, (t, t), 0)
    c = lax.broadcasted_iota(jnp.int32, (t, t), 1)
    mask = c <= r
    bias = near_ref[:, t:]
    v = v_ref[pl.ds(q0, t), :]
    s1 = jnp.where(mask, _qk(q1, k1_ref[pl.ds(q0, t), :]) + bias, NEG_INF)
    s2 = jnp.where(mask, _qk(q2, k2_ref[pl.ds(q0, t), :]) + bias, NEG_INF)
    _, l1, a1 = _online_update(c1, s1, v)
    _, l2, a2 = _online_update(c2, s2, v)

    o = a1 / l1 - lam * (a2 / l2)
    y = o * lax.rsqrt(jnp.mean(o * o, axis=-1, keepdims=True) + RMS_EPS) * subln_ref[...]
    o_ref[...] = (y * (1.0 - LAMBDA_INIT)).astype(bf16)


def _diff_attention(proj, proj_meta, near, mbias, lam_vecs, subln):
    t = ATTN_TILE
    cb = lambda col: col // HEAD_DIM
    vb = lambda col: col // DIFF_V_DIM
    row = lambda b, h, i: b * N_QBLK + i
    return pl.pallas_call(
        _diff_kernel,
        grid=(BATCH, DIFF_HEADS, N_QBLK),
        in_specs=[
            pl.BlockSpec((t, HEAD_DIM), lambda b, h, i: (row(b, h, i), cb(COL_DQ) + 2 * h)),
            pl.BlockSpec((t, HEAD_DIM), lambda b, h, i: (row(b, h, i), cb(COL_DQ) + 2 * h + 1)),
            pl.BlockSpec((SEQ, HEAD_DIM), lambda b, h, i: (b, cb(COL_DK) + 2 * h)),
            pl.BlockSpec((SEQ, HEAD_DIM), lambda b, h, i: (b, cb(COL_DK) + 2 * h + 1)),
            pl.BlockSpec((SEQ, DIFF_V_DIM), lambda b, h, i: (b, vb(COL_DV) + h)),
            pl.BlockSpec((N_META_PAD, HEAD_DIM), lambda b, h, i: (0, cb(COL_DK) + 2 * h)),
            pl.BlockSpec((N_META_PAD, HEAD_DIM), lambda b, h, i: (0, cb(COL_DK) + 2 * h + 1)),
            pl.BlockSpec((N_META_PAD, DIFF_V_DIM), lambda b, h, i: (0, vb(COL_DV) + h)),
            pl.BlockSpec((None, t, 2 * t), lambda b, h, i: (h, 0, 0)),
            pl.BlockSpec((None, t, LANES), lambda b, h, i: (h, 0, 0)),
            pl.BlockSpec((4, HEAD_DIM), lambda b, h, i: (0, 0)),
            pl.BlockSpec((1, DIFF_V_DIM), lambda b, h, i: (0, 0)),
        ],
        out_specs=pl.BlockSpec((t, DIFF_V_DIM), lambda b, h, i: (row(b, h, i), h)),
        out_shape=jax.ShapeDtypeStruct((N_TOK, DIFF_WIDTH), bf16),
        compiler_params=_params(("parallel", "parallel", "arbitrary")),
        name="diff_attention",
    )(proj, proj, proj, proj, proj, proj_meta, proj_meta, proj_meta, near, mbias, lam_vecs, subln)


def _merge_kernel(of_ref, od_ref, wf_ref, wd_ref, gf_ref, gd_ref, o_ref):
    yf = jnp.dot(of_ref[...], wf_ref[...], preferred_element_type=f32)
    yd = jnp.dot(od_ref[...], wd_ref[...], preferred_element_type=f32)
    gf = jax.nn.sigmoid(gf_ref[...].astype(f32))
    gd = jax.nn.sigmoid(gd_ref[...].astype(f32))
    o_ref[...] = (gf * yf + gd * yd).astype(bf16)


def _gated_merge(o_fox, o_diff, w_bf, w_bd, proj, tm=1024, tn=512):
    return pl.pallas_call(
        _merge_kernel,
        grid=(D_MODEL // tn, N_TOK // tm),
        in_specs=[
            pl.BlockSpec((tm, FOX_WIDTH), lambda j, i: (i, 0)),
            pl.BlockSpec((tm, DIFF_WIDTH), lambda j, i: (i, 0)),
            pl.BlockSpec((FOX_WIDTH, tn), lambda j, i: (0, j)),
            pl.BlockSpec((DIFF_WIDTH, tn), lambda j, i: (0, j)),
            pl.BlockSpec((tm, tn), lambda j, i: (i, COL_GF // tn + j)),
            pl.BlockSpec((tm, tn), lambda j, i: (i, COL_GD // tn + j)),
        ],
        out_specs=pl.BlockSpec((tm, tn), lambda j, i: (i, j)),
        out_shape=jax.ShapeDtypeStruct((N_TOK, D_MODEL), bf16),
        compiler_params=_params(("parallel", "arbitrary")),
        name="gated_merge",
    )(o_fox, o_diff, w_bf, w_bd, proj, proj)


def _outproj_kernel(m_ref, w_ref, x_ref, g_ref, wr_ref, br_ref, h_ref, hn_ref, lg_ref):
    h1 = x_ref[...] + jnp.dot(m_ref[...], w_ref[...], preferred_element_type=f32)
    h_ref[...] = h1
    hn = h1 * lax.rsqrt(jnp.mean(h1 * h1, axis=-1, keepdims=True) + RMS_EPS) * g_ref[...]
    hn_ref[...] = hn
    hn_hi = hn.astype(bf16)
    hn_lo = (hn - hn_hi.astype(f32)).astype(bf16)
    a = jnp.dot(hn_hi, wr_ref[...], preferred_element_type=f32)
    b = jnp.dot(hn_lo, wr_ref[:, :LANES], preferred_element_type=f32)
    lg_ref[...] = a[:, :LANES] + a[:, LANES:] + b + br_ref[...]


def _outproj(merged, w_out, x2d, gain, w_router_split, b_router_pad, tm=512):
    return pl.pallas_call(
        _outproj_kernel,
        grid=(N_TOK // tm,),
        in_specs=[
            pl.BlockSpec((tm, D_MODEL), lambda i: (i, 0)),
            pl.BlockSpec((D_MODEL, D_MODEL), lambda i: (0, 0)),
            pl.BlockSpec((tm, D_MODEL), lambda i: (i, 0)),
            pl.BlockSpec((1, D_MODEL), lambda i: (0, 0)),
            pl.BlockSpec((D_MODEL, 2 * LANES), lambda i: (0, 0)),
            pl.BlockSpec((1, LANES), lambda i: (0, 0)),
        ],
        out_specs=[
            pl.BlockSpec((tm, D_MODEL), lambda i: (i, 0)),
            pl.BlockSpec((tm, D_MODEL), lambda i: (i, 0)),
            pl.BlockSpec((tm, LANES), lambda i: (i, 0)),
        ],
        out_shape=[
            jax.ShapeDtypeStruct((N_TOK, D_MODEL), f32),
            jax.ShapeDtypeStruct((N_TOK, D_MODEL), f32),
            jax.ShapeDtypeStruct((N_TOK, LANES), f32),
        ],
        compiler_params=_params(("parallel",)),
        name="outproj_router",
    )(merged, w_out, x2d, gain, w_router_split, b_router_pad)


def _moe_kernel(item_e_ref, item_start_ref, item_n_ref, tok_ref, dest_ref,
                hn_hbm, wg_ref, wl_ref, bg_ref, bl_ref, wd_ref, bd_ref, y_hbm,
                acc_ref, xg_ref, xb_ref, wgb_ref, wlb_ref, wdb_ref, gsem, ssem):
    w = pl.program_id(0)
    t = pl.program_id(1)
    last_w = MOE_ITEMS - 1
    n = item_n_ref[w]
    start = item_start_ref[w]
    slot = lax.rem(w, 2)
    w_next = jnp.minimum(w + 1, last_w)
    start_next = item_start_ref[w_next]
    w_prev = jnp.maximum(w - 1, 0)
    start_prev = item_start_ref[w_prev]
    n_prev = jnp.where(w > 0, item_n_ref[w_prev], 0)
    is_last_live = jnp.logical_or(w == last_w, item_n_ref[w_next] == 0)

    def gather_row(row, base):
        tk = tok_ref[base + row]
        pltpu.make_async_copy(hn_hbm.at[pl.ds(tk, 1), :], xg_ref.at[pl.ds(row, 1), :], gsem).start()

    def gather_wait():
        pltpu.make_async_copy(hn_hbm.at[pl.ds(0, MOE_ROWS), :], xg_ref, gsem).wait()

    def scatter_row(row, base, cnt, sl):
        d = jnp.where(row < cnt, dest_ref[base + row], N_ASSIGN + row)
        pltpu.make_async_copy(acc_ref.at[sl, pl.ds(row, 1), :], y_hbm.at[pl.ds(d, 1), :], ssem).start()

    def scatter_wait(sl):
        pltpu.make_async_copy(acc_ref.at[sl], y_hbm.at[pl.ds(0, MOE_ROWS), :], ssem).wait()

    @pl.when(jnp.logical_and(w == 0, t == 0))
    def _():
        acc_ref[1] = jnp.zeros((MOE_ROWS, D_MODEL), f32)

        def issue(r, c):
            gather_row(r, start)
            return c
        lax.fori_loop(0, MOE_ROWS, issue, 0, unroll=8)

    @pl.when(jnp.logical_and(t == 0, n > 0))
    def _():
        gather_wait()
        xb_ref[...] = xg_ref[...].astype(bf16)
        acc_ref[slot] = jnp.zeros((MOE_ROWS, D_MODEL), f32)

    def dma_chunk(r):
        row0 = t * (MOE_NSUB * MOE_CHUNK) + r * MOE_CHUNK
        for k in range(MOE_CHUNK):
            gather_row(row0 + k, start_next)
        for k in range(MOE_CHUNK):
            scatter_row(row0 + k, start_prev, n_prev, 1 - slot)

    def ffn_up(r0, rows):
        xs = xb_ref[r0:r0 + rows, :]
        hg = jnp.dot(xs, wgb_ref[...], preferred_element_type=f32) + bg_ref[...]
        hl = jnp.dot(xs, wlb_ref[...], preferred_element_type=f32) + bl_ref[...]
        return hg, hl

    def ffn_down(r0, rows, hg, hl):
        hg = jnp.minimum(hg, SWIGLU_LIMIT)
        hl = jnp.clip(hl, -SWIGLU_LIMIT, SWIGLU_LIMIT)
        act = hg * jax.nn.sigmoid(SWIGLU_ALPHA * hg) * (hl + 1.0)
        acc_ref[slot, r0:r0 + rows, :] += jnp.dot(act.astype(bf16), wdb_ref[...], preferred_element_type=f32)

    def ffn_rows(r0, rows):
        ffn_down(r0, rows, *ffn_up(r0, rows))

    @pl.when(n > 0)
    def _():
        wgb_ref[...] = wg_ref[...].astype(bf16)
        wlb_ref[...] = wl_ref[...].astype(bf16)
        wdb_ref[...] = wd_ref[...].astype(bf16)
        def one_block(r):
            r0 = r * MOE_SUB

            def full():
                dma_chunk(r)
                ffn_rows(r0, MOE_SUB)

            def half():
                dma_chunk(r)
                ffn_rows(r0, MOE_HALF)

            def idle():
                dma_chunk(r)

            lax.cond(n > r0 + MOE_HALF, full, lambda: lax.cond(n > r0, half, idle))

        for r in range(0, MOE_NSUB - 1, 2):
            def pair(r=r):
                dma_chunk(r)
                dma_chunk(r + 1)
                ra, rb = r * MOE_SUB, (r + 1) * MOE_SUB
                ha = ffn_up(ra, MOE_SUB)
                hb = ffn_up(rb, MOE_SUB)
                ffn_down(ra, MOE_SUB, *ha)
                ffn_down(rb, MOE_SUB, *hb)

            def singles(r=r):
                one_block(r)
                one_block(r + 1)

            lax.cond(n > (r + 1) * MOE_SUB + MOE_HALF, pair, singles)
        for r in range(MOE_NSUB - MOE_NSUB % 2, MOE_NSUB):
            one_block(r)

    @pl.when(jnp.logical_and(t == MOE_NF - 1, n > 0))
    def _():
        scatter_wait(1 - slot)
        acc_ref[slot] = acc_ref[slot] + bd_ref[...]

        @pl.when(is_last_live)
        def _():
            def issue(r, c):
                scatter_row(r, start, n, slot)
                return c
            lax.fori_loop(0, MOE_ROWS, issue, 0, unroll=8)
            scatter_wait(slot)
            gather_wait()


def _moe(item_e, item_start, item_n, tok_sorted, dest_sorted, hn2, w_gate_up, b_gate_up, w_down, b_down):
    tf = MOE_TF
    live_t = lambda w, t, n: jnp.where(n[w] > 0, t, MOE_NF - 1)
    grid_spec = pltpu.PrefetchScalarGridSpec(
        num_scalar_prefetch=5,
        grid=(MOE_ITEMS, MOE_NF),
        in_specs=[
            pl.BlockSpec(memory_space=pl.ANY),
            pl.BlockSpec((None, D_MODEL, tf), lambda w, t, e, s, n, tk, ds: (e[w], 0, live_t(w, t, n))),
            pl.BlockSpec((None, D_MODEL, tf), lambda w, t, e, s, n, tk, ds: (e[w], 0, MOE_NF + live_t(w, t, n))),
            pl.BlockSpec((None, 1, tf), lambda w, t, e, s, n, tk, ds: (e[w], 0, live_t(w, t, n))),
            pl.BlockSpec((None, 1, tf), lambda w, t, e, s, n, tk, ds: (e[w], 0, MOE_NF + live_t(w, t, n))),
            pl.BlockSpec((None, tf, D_MODEL), lambda w, t, e, s, n, tk, ds: (e[w], live_t(w, t, n), 0)),
            pl.BlockSpec((None, 1, D_MODEL), lambda w, t, e, s, n, tk, ds: (e[w], 0, 0)),
        ],
        out_specs=pl.BlockSpec(memory_space=pl.ANY),
        scratch_shapes=[
            pltpu.VMEM((2, MOE_ROWS, D_MODEL), f32),
            pltpu.VMEM((MOE_ROWS, D_MODEL), f32),
            pltpu.VMEM((MOE_ROWS, D_MODEL), bf16),
            pltpu.VMEM((D_MODEL, tf), bf16),
            pltpu.VMEM((D_MODEL, tf), bf16),
            pltpu.VMEM((tf, D_MODEL), bf16),
            pltpu.SemaphoreType.DMA,
            pltpu.SemaphoreType.DMA,
        ],
    )
    return pl.pallas_call(
        _moe_kernel,
        grid_spec=grid_spec,
        out_shape=jax.ShapeDtypeStruct((Y_ROWS, D_MODEL), f32),
        compiler_params=_params(("arbitrary", "arbitrary"), MOE_VMEM_LIMIT_BYTES),
        name="moe_experts",
    )(item_e, item_start, item_n, tok_sorted, dest_sorted,
      hn2, w_gate_up, w_gate_up, b_gate_up, b_gate_up, w_down, b_down)


def _combine_kernel(y0_ref, y1_ref, y2_ref, y3_ref, h_ref, gate_ref, g_ref, o_ref):
    h = h_ref[...]
    gates = gate_ref[...]
    for k, y_ref in enumerate((y0_ref, y1_ref, y2_ref, y3_ref)):
        h = h + y_ref[...] * gates[:, k:k + 1]
    o_ref[...] = h * lax.rsqrt(jnp.mean(h * h, axis=-1, keepdims=True) + RMS_EPS) * g_ref[...]


def _combine(y, h1, gates, gain, tm=256):
    slot_spec = lambda k: pl.BlockSpec((tm, D_MODEL), lambda i, k=k: (k * (N_TOK // tm) + i, 0))
    return pl.pallas_call(
        _combine_kernel,
        grid=(N_TOK // tm,),
        in_specs=[slot_spec(k) for k in range(TOP_K)] + [
            pl.BlockSpec((tm, D_MODEL), lambda i: (i, 0)),
            pl.BlockSpec((tm, TOP_K), lambda i: (i, 0)),
            pl.BlockSpec((1, D_MODEL), lambda i: (0, 0)),
        ],
        out_specs=pl.BlockSpec((tm, D_MODEL), lambda i: (i, 0)),
        out_shape=jax.ShapeDtypeStruct((N_TOK, D_MODEL), f32),
        compiler_params=_params(("parallel",)),
        name="combine_norm",
    )(y, y, y, y, h1, gates, gain)


def _route(logits):
    top_vals, top_idx = lax.top_k(logits, TOP_K)
    gates = jax.nn.softmax(top_vals, axis=-1)
    expert_flat = top_idx.reshape(-1).astype(jnp.int32)
    order = jnp.argsort(expert_flat, stable=True).astype(jnp.int32)
    counts = jnp.zeros((N_EXPERTS,), jnp.int32).at[expert_flat].add(1)
    starts = jnp.cumsum(counts) - counts
    tok_sorted = order // TOP_K
    dest_sorted = (order % TOP_K) * N_TOK + tok_sorted
    tok_sorted = jnp.pad(tok_sorted, (0, MOE_ROWS))
    dest_sorted = jnp.pad(dest_sorted, (0, MOE_ROWS))
    chunks = (counts + MOE_ROWS - 1) // MOE_ROWS
    chunk_end = jnp.cumsum(chunks)
    n_items = chunk_end[-1]
    w = jnp.arange(MOE_ITEMS, dtype=jnp.int32)
    w_eff = jnp.minimum(w, n_items - 1)
    e = jnp.sum(w_eff[:, None] >= chunk_end[None, :], axis=1).astype(jnp.int32)
    c = w_eff - (chunk_end - chunks)[e]
    item_start = starts[e] + c * MOE_ROWS
    item_n = jnp.where(w < n_items, jnp.clip(counts[e] - c * MOE_ROWS, 0, MOE_ROWS), 0)
    return gates, e, item_start.astype(jnp.int32), item_n.astype(jnp.int32), tok_sorted, dest_sorted


def kernel(x, meta_tokens, rel_bias_table, attn_norm, w_in, fox_forget_bias, lam_q1, lam_k1, lam_q2, lam_k2,
           diff_subln, w_branch_fox, w_branch_diff, w_out, ffn_norm, w_router, b_router, w_gate_up, b_gate_up,
           w_down, b_down, final_norm):
    x2d = x.reshape(N_TOK, D_MODEL)
    w_in2d = w_in[0]
    ff_lo = 3 * FOX_WIDTH
    w_rest = w_in2d[:, ff_lo + FOX_HEADS:].astype(bf16)
    w_ff = jnp.pad(w_in2d[:, ff_lo:ff_lo + FOX_HEADS], ((0, 0), (0, LANES - FOX_HEADS))).astype(bf16)
    fb_pad = jnp.pad(fox_forget_bias[0], (0, LANES - FOX_HEADS)).reshape(1, LANES)

    proj, ff_real = _inproj(x2d, attn_norm, w_in2d, w_rest, w_ff, tm=1024, tn=512)
    meta_pad = jnp.pad(meta_tokens, ((0, N_META_PAD - N_META), (0, 0)))
    proj_meta, ff_meta = _inproj(meta_pad, attn_norm, w_in2d, w_rest, w_ff, tm=N_META_PAD, tn=1024)

    ck, ckm = _forget_cumsum(ff_real, ff_meta, fb_pad)
    o_fox = _fox_attention(proj, proj_meta, ck, ckm)

    near, mbias = _bias_tiles(rel_bias_table)
    lam_vecs = jnp.concatenate([lam_q1, lam_k1, lam_q2, lam_k2], axis=0)
    o_diff = _diff_attention(proj, proj_meta, near, mbias, lam_vecs, diff_subln)

    merged = _gated_merge(o_fox, o_diff, w_branch_fox[0].astype(bf16), w_branch_diff[0].astype(bf16), proj)
    w_router_pad = jnp.pad(w_router[0], ((0, 0), (0, LANES - N_EXPERTS)))
    w_router_hi = w_router_pad.astype(bf16)
    w_router_lo = (w_router_pad - w_router_hi.astype(f32)).astype(bf16)
    w_router_split = jnp.concatenate([w_router_hi, w_router_lo], axis=1)
    b_router_pad = jnp.pad(b_router[0], (0, LANES - N_EXPERTS)).reshape(1, LANES)
    h1, hn2, logits = _outproj(merged, w_out[0].astype(bf16), x2d, ffn_norm, w_router_split, b_router_pad)

    gates, item_e, item_start, item_n, tok_sorted, dest_sorted = _route(logits[:, :N_EXPERTS])
    y = _moe(item_e, item_start, item_n, tok_sorted, dest_sorted, hn2,
             w_gate_up[0], b_gate_up[0].reshape(N_EXPERTS, 1, 2 * D_EXPERT),
             w_down[0], b_down[0].reshape(N_EXPERTS, 1, D_MODEL))
    out = _combine(y, h1, gates, final_norm.reshape(1, D_MODEL))
    return out.reshape(BATCH, SEQ, D_MODEL)
```

```python
import functools
import math

import jax
import jax.numpy as jnp
from jax import lax
from jax.experimental import pallas as pl
from jax.experimental.pallas import tpu as pltpu

D_MODEL = 2048
BATCH = 4
SEQ = 2048
N_TOK = BATCH * SEQ
N_META = 16
N_META_PAD = 128
HEAD_DIM = 128
FOX_HEADS = 8
DIFF_HEADS = 4
DIFF_V_DIM = 2 * HEAD_DIM
FOX_WIDTH = FOX_HEADS * HEAD_DIM
DIFF_QK_WIDTH = DIFF_HEADS * 2 * HEAD_DIM
DIFF_WIDTH = DIFF_HEADS * DIFF_V_DIM
N_BUCKETS = 32
MAX_DISTANCE = 128
N_EXPERTS = 32
TOP_K = 4
D_EXPERT = D_MODEL
SWIGLU_LIMIT = 7.0
SWIGLU_ALPHA = 1.702
RMS_EPS = 1e-5
NEG_INF = -1e30
LAMBDA_INIT = 0.8 - 0.6 * math.exp(-0.3 * 0)
ATTN_SCALE = HEAD_DIM ** -0.5
LOG2E = math.log2(math.e)
Q_PRESCALE = ATTN_SCALE * LOG2E

LANES = 128
SUBLANES = 8
VMEM_LIMIT_BYTES = 56 * 1024 * 1024
BIG_VMEM_LIMIT_BYTES = 60 * 1024 * 1024

COL_FQ = 0
COL_FK = COL_FQ + FOX_WIDTH
COL_FV = COL_FK + FOX_WIDTH
COL_DQ = COL_FV + FOX_WIDTH
COL_DK = COL_DQ + DIFF_QK_WIDTH
COL_DV = COL_DK + DIFF_QK_WIDTH
COL_GF = COL_DV + DIFF_WIDTH
COL_GD = COL_GF + D_MODEL
D_PROJ = COL_GD + D_MODEL

ATTN_TILE = 512
N_QBLK = SEQ // ATTN_TILE

MOE_ROWS = 1280
MOE_SUB = 256
MOE_HALF = MOE_SUB // 2
MOE_NSUB = MOE_ROWS // MOE_SUB
MOE_TF = 256
MOE_NF = D_EXPERT // MOE_TF
MOE_CHUNK = MOE_ROWS // (MOE_NF * MOE_NSUB)
N_ASSIGN = N_TOK * TOP_K
MOE_ITEMS = N_EXPERTS + N_ASSIGN // MOE_ROWS
Y_ROWS = N_ASSIGN + MOE_ROWS

f32 = jnp.float32
bf16 = jnp.bfloat16


def _params(sem, vmem=VMEM_LIMIT_BYTES):
    return pltpu.CompilerParams(dimension_semantics=sem, vmem_limit_bytes=vmem)


def _inproj_kernel(x_hbm, g_ref, wt_ref, wfft_ref, o_ref, ff_ref, xn_ref, xt_ref, wb_ref, wffb_ref, sem, *, tm, tn):
    j = pl.program_id(0)
    i = pl.program_id(1)
    rows = pl.ds(pl.multiple_of(i * tm, tm), tm)

    @pl.when(i == 0)
    def _():
        wb_ref[...] = wt_ref[...].T.astype(bf16)

    @pl.when(j == 0)
    def _():
        @pl.when(i == 0)
        def _():
            wffb_ref[...] = wfft_ref[...].T.astype(bf16)

        cp = pltpu.make_async_copy(x_hbm.at[rows, :], xt_ref, sem)
        cp.start()
        cp.wait()
        x = xt_ref[...]
        ms = jnp.mean(x * x, axis=-1, keepdims=True)
        xn = (x * lax.rsqrt(ms + RMS_EPS) * g_ref[...]).astype(bf16)
        xn_ref[rows, :] = xn
        ff_ref[...] = jnp.dot(xn, wffb_ref[...], preferred_element_type=f32)

    is_q = jnp.logical_or(j < COL_FK // tn, jnp.logical_and(j >= COL_DQ // tn, j < COL_DK // tn))
    qs = jnp.where(is_q, Q_PRESCALE, 1.0)
    y = jnp.dot(xn_ref[rows, :], wb_ref[...], preferred_element_type=f32)
    o_ref[...] = (y * qs).astype(bf16)


def _inproj(x2d, gain, w_t, tm, tn):
    n = x2d.shape[0]
    n_i = n // tm
    w_row = lambda j: SUBLANES * (j * (tn // SUBLANES) + jnp.where(j * tn >= COL_DQ, FOX_HEADS // SUBLANES, 0))
    first_pass_row = lambda j, i: jnp.where(j == 0, i, n_i - 1)
    return pl.pallas_call(
        functools.partial(_inproj_kernel, tm=tm, tn=tn),
        grid=(D_PROJ // tn, n_i),
        in_specs=[
            pl.BlockSpec(memory_space=pl.ANY),
            pl.BlockSpec((1, D_MODEL), lambda j, i: (0, 0)),
            pl.BlockSpec((pl.Element(tn), pl.Element(D_MODEL)), lambda j, i: (w_row(j), 0)),
            pl.BlockSpec((pl.Element(LANES), pl.Element(D_MODEL)), lambda j, i: (COL_DQ, 0)),
        ],
        out_specs=[
            pl.BlockSpec((tm, tn), lambda j, i: (i, j)),
            pl.BlockSpec((tm, LANES), lambda j, i: (first_pass_row(j, i), 0)),
        ],
        out_shape=[
            jax.ShapeDtypeStruct((n, D_PROJ), bf16),
            jax.ShapeDtypeStruct((n, LANES), f32),
        ],
        scratch_shapes=[
            pltpu.VMEM((n, D_MODEL), bf16),
            pltpu.VMEM((tm, D_MODEL), f32),
            pltpu.VMEM((D_MODEL, tn), bf16),
            pltpu.VMEM((D_MODEL, LANES), bf16),
            pltpu.SemaphoreType.DMA,
        ],
        compiler_params=_params(("arbitrary", "arbitrary"), BIG_VMEM_LIMIT_BYTES),
        name="inproj",
    )(x2d, gain, w_t, w_t)


def _log_sigmoid(x):
    return jnp.minimum(x, 0.0) - jnp.log(1.0 + jnp.exp(-jnp.abs(x)))


def _lane_cumsum(x):
    n = x.shape[-1]
    lane = lax.broadcasted_iota(jnp.int32, x.shape, x.ndim - 1)
    s = 1
    while s < n:
        x = x + jnp.where(lane >= s, pltpu.roll(x, s, x.ndim - 1), 0.0)
        s *= 2
    return x


def _cum_kernel(ff_ref, ffm_ref, fb_ref, ck_ref, ckm_ref):
    fb = fb_ref[...]
    row = lax.broadcasted_iota(jnp.int32, (LANES, LANES), 0)
    lfm = jnp.where(row < N_META, _log_sigmoid(ffm_ref[...] + fb), 0.0)
    cm = _lane_cumsum(lfm.T)
    m_total = cm[:, N_META - 1:N_META]
    lf = _log_sigmoid(ff_ref[...] + fb)
    cr = _lane_cumsum(lf.T) + m_total
    ck_ref[...] = cr[:SUBLANES] * LOG2E
    ckm_ref[...] = cm[:SUBLANES] * LOG2E


def _forget_cumsum(ff_real, ff_meta_pad, fb_pad):
    return pl.pallas_call(
        _cum_kernel,
        grid=(BATCH,),
        in_specs=[
            pl.BlockSpec((SEQ, LANES), lambda b: (b, 0)),
            pl.BlockSpec((LANES, LANES), lambda b: (0, 0)),
            pl.BlockSpec((1, LANES), lambda b: (0, 0)),
        ],
        out_specs=[
            pl.BlockSpec((None, SUBLANES, SEQ), lambda b: (b, 0, 0)),
            pl.BlockSpec((None, SUBLANES, LANES), lambda b: (b, 0, 0)),
        ],
        out_shape=[
            jax.ShapeDtypeStruct((BATCH, FOX_HEADS, SEQ), f32),
            jax.ShapeDtypeStruct((BATCH, FOX_HEADS, LANES), f32),
        ],
        compiler_params=_params(("parallel",)),
        name="forget_cumsum",
    )(ff_real, ff_meta_pad, fb_pad)


def _qk(q, k):
    return lax.dot_general(q, k, (((1,), (1,)), ((), ())), preferred_element_type=f32)


def _row_to_col(row):
    t = row.shape[-1]
    r = lax.broadcasted_iota(jnp.int32, (t, t), 0)
    c = lax.broadcasted_iota(jnp.int32, (t, t), 1)
    return jnp.sum(jnp.where(r == c, row, 0.0), axis=1, keepdims=True)


def _online_update(carry, s, v):
    m, l, acc = carry
    m_new = jnp.maximum(m, jnp.max(s, axis=-1, keepdims=True))
    alpha = jnp.exp2(m - m_new)
    p = jnp.exp2(s - m_new)
    l = alpha * l + jnp.sum(p, axis=-1, keepdims=True)
    acc = alpha * acc + jnp.dot(p.astype(bf16), v, preferred_element_type=f32)
    return m_new, l, acc


def _first_update(s, v):
    m = jnp.max(s, axis=-1, keepdims=True)
    p = jnp.exp2(s - m)
    l = jnp.sum(p, axis=-1, keepdims=True)
    acc = jnp.dot(p.astype(bf16), v, preferred_element_type=f32)
    return m, l, acc


FOX_GROUP = 2


def _fox_update(carry, u, cq, v):
    m_blk = jnp.max(u, axis=-1, keepdims=True) + cq
    if carry is None:
        m_new = m_blk
    else:
        m, l, acc = carry
        m_new = jnp.maximum(m, m_blk)
    p = jnp.exp2(u - (m_new - cq))
    ps = jnp.sum(p, axis=-1, keepdims=True)
    pv = jnp.dot(p.astype(bf16), v, preferred_element_type=f32)
    if carry is None:
        return m_new, ps, pv
    alpha = jnp.exp2(m - m_new)
    return m_new, alpha * l + ps, alpha * acc + pv


def _fox_kernel(q_ref, k_ref, v_ref, km_ref, vm_ref, ck_ref, ckm_ref, o_ref):
    hg = pl.program_id(1)
    i = pl.program_id(2)
    t = ATTN_TILE
    q0 = pl.multiple_of(i * t, t)
    heads = range(FOX_GROUP)
    col = lambda g: slice(g * HEAD_DIM, (g + 1) * HEAD_DIM)
    ck_row = lambda g, k0: ck_ref[pl.ds(hg * FOX_GROUP + g, 1), pl.ds(k0, t)]
    qs = [q_ref[:, col(g)] for g in heads]
    cqs = [_row_to_col(ck_row(g, q0)) for g in heads]

    meta_ok = lax.broadcasted_iota(jnp.int32, (t, N_META_PAD), 1) < N_META
    us = [_qk(qs[g], km_ref[:, col(g)]) - ckm_ref[pl.ds(hg * FOX_GROUP + g, 1), :] for g in heads]
    carry = tuple(_fox_update(None, jnp.where(meta_ok, us[g], NEG_INF), cqs[g], vm_ref[:, col(g)]) for g in heads)

    def body(j, carry):
        k0 = pl.multiple_of(j * t, t)
        us = [_qk(qs[g], k_ref[pl.ds(k0, t), col(g)]) - ck_row(g, k0) for g in heads]
        return tuple(_fox_update(carry[g], us[g], cqs[g], v_ref[pl.ds(k0, t), col(g)]) for g in heads)

    carry = lax.fori_loop(0, i, body, carry)

    r = lax.broadcasted_iota(jnp.int32, (t, t), 0)
    c = lax.broadcasted_iota(jnp.int32, (t, t), 1)
    us = [_qk(qs[g], k_ref[pl.ds(q0, t), col(g)]) - ck_row(g, q0) for g in heads]
    for g in heads:
        _, l, acc = _fox_update(carry[g], jnp.where(c <= r, us[g], NEG_INF), cqs[g], v_ref[pl.ds(q0, t), col(g)])
        o_ref[:, col(g)] = (acc / l).astype(bf16)


def _fox_attention(proj, proj_meta, ck, ckm):
    t = ATTN_TILE
    w = FOX_GROUP * HEAD_DIM
    cb = lambda col: col // w
    return pl.pallas_call(
        _fox_kernel,
        grid=(BATCH, FOX_HEADS // FOX_GROUP, N_QBLK),
        in_specs=[
            pl.BlockSpec((t, w), lambda b, h, i: (b * N_QBLK + i, cb(COL_FQ) + h)),
            pl.BlockSpec((SEQ, w), lambda b, h, i: (b, cb(COL_FK) + h)),
            pl.BlockSpec((SEQ, w), lambda b, h, i: (b, cb(COL_FV) + h)),
            pl.BlockSpec((N_META_PAD, w), lambda b, h, i: (0, cb(COL_FK) + h)),
            pl.BlockSpec((N_META_PAD, w), lambda b, h, i: (0, cb(COL_FV) + h)),
            pl.BlockSpec((None, FOX_HEADS, SEQ), lambda b, h, i: (b, 0, 0)),
            pl.BlockSpec((None, FOX_HEADS, LANES), lambda b, h, i: (b, 0, 0)),
        ],
        out_specs=pl.BlockSpec((t, w), lambda b, h, i: (b * N_QBLK + i, h)),
        out_shape=jax.ShapeDtypeStruct((N_TOK, FOX_WIDTH), bf16),
        compiler_params=_params(("parallel", "parallel", "arbitrary")),
        name="fox_attention",
    )(proj, proj, proj, proj_meta, proj_meta, ck, ckm)


def _t5_bias(dist, table_ref, h):
    n = jnp.maximum(dist, 0)
    max_exact = N_BUCKETS // 2
    log_part = jnp.log(jnp.maximum(n, 1).astype(f32) / max_exact) / math.log(MAX_DISTANCE / max_exact)
    v = log_part * (N_BUCKETS - max_exact)
    far = table_ref[N_BUCKETS - 1, h]
    val = lambda b: (table_ref[b, h] - far) * LOG2E
    large = jnp.zeros(dist.shape, f32)
    for b in range(N_BUCKETS - 2, max_exact - 1, -1):
        large = jnp.where(v < b + 1 - max_exact, val(b), large)
    out = large
    for b in range(max_exact):
        out = jnp.where(n == b, val(b), out)
    return out


def _bias_kernel(table_ref, near_ref, meta_ref):
    h = pl.program_id(0)
    t = ATTN_TILE
    r = lax.broadcasted_iota(jnp.int32, (t, 2 * t), 0)
    c = lax.broadcasted_iota(jnp.int32, (t, 2 * t), 1)
    near_ref[...] = _t5_bias(r + t - c, table_ref, h)
    r = lax.broadcasted_iota(jnp.int32, (t, LANES), 0)
    c = lax.broadcasted_iota(jnp.int32, (t, LANES), 1)
    meta_ref[...] = _t5_bias(N_META + r - c, table_ref, h)


def _bias_tiles(table):
    t = ATTN_TILE
    return pl.pallas_call(
        _bias_kernel,
        grid=(DIFF_HEADS,),
        in_specs=[pl.BlockSpec(memory_space=pltpu.SMEM)],
        out_specs=[
            pl.BlockSpec((None, t, 2 * t), lambda h: (h, 0, 0)),
            pl.BlockSpec((None, t, LANES), lambda h: (h, 0, 0)),
        ],
        out_shape=[
            jax.ShapeDtypeStruct((DIFF_HEADS, t, 2 * t), f32),
            jax.ShapeDtypeStruct((DIFF_HEADS, t, LANES), f32),
        ],
        compiler_params=_params(("arbitrary",)),
        name="t5_bias_tiles",
    )(table)


def _diff_kernel(q1_ref, q2_ref, k1_ref, k2_ref, v_ref, k1m_ref, k2m_ref, vm_ref,
                 near_ref, mbias_ref, lam_ref, subln_ref, o_ref):
    i = pl.program_id(2)
    t = ATTN_TILE
    q1 = q1_ref[...]
    q2 = q2_ref[...]
    q0 = pl.multiple_of(i * t, t)
    lam = (jnp.exp(jnp.sum(lam_ref[0:1, :] * lam_ref[1:2, :], axis=-1, keepdims=True))
           - jnp.exp(jnp.sum(lam_ref[2:3, :] * lam_ref[3:4, :], axis=-1, keepdims=True))
           + LAMBDA_INIT)

    mb = jnp.where(i == 0, mbias_ref[...], 0.0)
    meta_ok = lax.broadcasted_iota(jnp.int32, (t, N_META_PAD), 1) < N_META
    vm = vm_ref[...]
    c1 = _first_update(jnp.where(meta_ok, _qk(q1, k1m_ref[...]) + mb, NEG_INF), vm)
    c2 = _first_update(jnp.where(meta_ok, _qk(q2, k2m_ref[...]) + mb, NEG_INF), vm)

    def far_body(j, carry):
        c1, c2 = carry
        k0 = pl.multiple_of(j * t, t)
        v = v_ref[pl.ds(k0, t), :]
        c1 = _online_update(c1, _qk(q1, k1_ref[pl.ds(k0, t), :]), v)
        c2 = _online_update(c2, _qk(q2, k2_ref[pl.ds(k0, t), :]), v)
        return c1, c2

    c1, c2 = lax.fori_loop(0, jnp.maximum(i - 1, 0), far_body, (c1, c2))

    def prev_block(carry):
        c1, c2 = carry
        k0 = pl.multiple_of((i - 1) * t, t)
        v = v_ref[pl.ds(k0, t), :]
        bias = near_ref[:, :t]
        c1 = _online_update(c1, _qk(q1, k1_ref[pl.ds(k0, t), :]) + bias, v)
        c2 = _online_update(c2, _qk(q2, k2_ref[pl.ds(k0, t), :]) + bias, v)
        return c1, c2

    c1, c2 = lax.cond(i > 0, prev_block, lambda carry: carry, (c1, c2))

    r = lax.broadcasted_iota(jnp.int32, (t, t), 0)
    c = lax.broadcasted_iota(jnp.int32, (t, t), 1)
    mask = c <= r
    bias = near_ref[:, t:]
    v = v_ref[pl.ds(q0, t), :]
    s1 = jnp.where(mask, _qk(q1, k1_ref[pl.ds(q0, t), :]) + bias, NEG_INF)
    s2 = jnp.where(mask, _qk(q2, k2_ref[pl.ds(q0, t), :]) + bias, NEG_INF)
    _, l1, a1 = _online_update(c1, s1, v)
    _, l2, a2 = _online_update(c2, s2, v)

    o = a1 / l1 - lam * (a2 / l2)
    y = o * lax.rsqrt(jnp.mean(o * o, axis=-1, keepdims=True) + RMS_EPS) * subln_ref[...]
    o_ref[...] = (y * (1.0 - LAMBDA_INIT)).astype(bf16)


def _diff_attention(proj, proj_meta, near, mbias, lam_vecs, subln):
    t = ATTN_TILE
    cb = lambda col: col // HEAD_DIM
    vb = lambda col: col // DIFF_V_DIM
    row = lambda b, h, i: b * N_QBLK + i
    return pl.pallas_call(
        _diff_kernel,
        grid=(BATCH, DIFF_HEADS, N_QBLK),
        in_specs=[
            pl.BlockSpec((t, HEAD_DIM), lambda b, h, i: (row(b, h, i), cb(COL_DQ) + 2 * h)),
            pl.BlockSpec((t, HEAD_DIM), lambda b, h, i: (row(b, h, i), cb(COL_DQ) + 2 * h + 1)),
            pl.BlockSpec((SEQ, HEAD_DIM), lambda b, h, i: (b, cb(COL_DK) + 2 * h)),
            pl.BlockSpec((SEQ, HEAD_DIM), lambda b, h, i: (b, cb(COL_DK) + 2 * h + 1)),
            pl.BlockSpec((SEQ, DIFF_V_DIM), lambda b, h, i: (b, vb(COL_DV) + h)),
            pl.BlockSpec((N_META_PAD, HEAD_DIM), lambda b, h, i: (0, cb(COL_DK) + 2 * h)),
            pl.BlockSpec((N_META_PAD, HEAD_DIM), lambda b, h, i: (0, cb(COL_DK) + 2 * h + 1)),
            pl.BlockSpec((N_META_PAD, DIFF_V_DIM), lambda b, h, i: (0, vb(COL_DV) + h)),
            pl.BlockSpec((None, t, 2 * t), lambda b, h, i: (h, 0, 0)),
            pl.BlockSpec((None, t, LANES), lambda b, h, i: (h, 0, 0)),
            pl.BlockSpec((4, HEAD_DIM), lambda b, h, i: (0, 0)),
            pl.BlockSpec((1, DIFF_V_DIM), lambda b, h, i: (0, 0)),
        ],
        out_specs=pl.BlockSpec((t, DIFF_V_DIM), lambda b, h, i: (row(b, h, i), h)),
        out_shape=jax.ShapeDtypeStruct((N_TOK, DIFF_WIDTH), bf16),
        compiler_params=_params(("parallel", "parallel", "arbitrary")),
        name="diff_attention",
    )(proj, proj, proj, proj, proj, proj_meta, proj_meta, proj_meta, near, mbias, lam_vecs, subln)


def _merge_kernel(of_ref, od_ref, wf_ref, wd_ref, gf_ref, gd_ref, o_ref):
    yf = jnp.dot(of_ref[...], wf_ref[...], preferred_element_type=f32)
    yd = jnp.dot(od_ref[...], wd_ref[...], preferred_element_type=f32)
    gf = jax.nn.sigmoid(gf_ref[...].astype(f32))
    gd = jax.nn.sigmoid(gd_ref[...].astype(f32))
    o_ref[...] = (gf * yf + gd * yd).astype(bf16)


def _gated_merge(o_fox, o_diff, w_bf, w_bd, proj, tm=1024, tn=512):
    return pl.pallas_call(
        _merge_kernel,
        grid=(D_MODEL // tn, N_TOK // tm),
        in_specs=[
            pl.BlockSpec((tm, FOX_WIDTH), lambda j, i: (i, 0)),
            pl.BlockSpec((tm, DIFF_WIDTH), lambda j, i: (i, 0)),
            pl.BlockSpec((FOX_WIDTH, tn), lambda j, i: (0, j)),
            pl.BlockSpec((DIFF_WIDTH, tn), lambda j, i: (0, j)),
            pl.BlockSpec((tm, tn), lambda j, i: (i, COL_GF // tn + j)),
            pl.BlockSpec((tm, tn), lambda j, i: (i, COL_GD // tn + j)),
        ],
        out_specs=pl.BlockSpec((tm, tn), lambda j, i: (i, j)),
        out_shape=jax.ShapeDtypeStruct((N_TOK, D_MODEL), bf16),
        compiler_params=_params(("parallel", "arbitrary")),
        name="gated_merge",
    )(o_fox, o_diff, w_bf, w_bd, proj, proj)


def _outproj_kernel(m_ref, w_ref, x_ref, g_ref, wr_ref, br_ref, h_ref, hn_ref, lg_ref):
    h1 = x_ref[...] + jnp.dot(m_ref[...], w_ref[...], preferred_element_type=f32)
    h_ref[...] = h1
    hn = h1 * lax.rsqrt(jnp.mean(h1 * h1, axis=-1, keepdims=True) + RMS_EPS) * g_ref[...]
    hn_ref[...] = hn
    hn_hi = hn.astype(bf16)
    hn_lo = (hn - hn_hi.astype(f32)).astype(bf16)
    a = jnp.dot(hn_hi, wr_ref[...], preferred_element_type=f32)
    b = jnp.dot(hn_lo, wr_ref[:, :LANES], preferred_element_type=f32)
    lg_ref[...] = a[:, :LANES] + a[:, LANES:] + b + br_ref[...]


def _outproj(merged, w_out, x2d, gain, w_router_split, b_router_pad, tm=512):
    return pl.pallas_call(
        _outproj_kernel,
        grid=(N_TOK // tm,),
        in_specs=[
            pl.BlockSpec((tm, D_MODEL), lambda i: (i, 0)),
            pl.BlockSpec((D_MODEL, D_MODEL), lambda i: (0, 0)),
            pl.BlockSpec((tm, D_MODEL), lambda i: (i, 0)),
            pl.BlockSpec((1, D_MODEL), lambda i: (0, 0)),
            pl.BlockSpec((D_MODEL, 2 * LANES), lambda i: (0, 0)),
            pl.BlockSpec((1, LANES), lambda i: (0, 0)),
        ],
        out_specs=[
            pl.BlockSpec((tm, D_MODEL), lambda i: (i, 0)),
            pl.BlockSpec((tm, D_MODEL), lambda i: (i, 0)),
            pl.BlockSpec((tm, LANES), lambda i: (i, 0)),
        ],
        out_shape=[
            jax.ShapeDtypeStruct((N_TOK, D_MODEL), f32),
            jax.ShapeDtypeStruct((N_TOK, D_MODEL), f32),
            jax.ShapeDtypeStruct((N_TOK, LANES), f32),
        ],
        compiler_params=_params(("parallel",)),
        name="outproj_router",
    )(merged, w_out, x2d, gain, w_router_split, b_router_pad)


def _moe_kernel(item_e_ref, item_start_ref, item_n_ref, tok_ref, dest_ref,
                hn_hbm, wg_ref, wl_ref, bg_ref, bl_ref, wd_ref, bd_ref, y_hbm,
                acc_ref, xg_ref, xb_ref, wgb_ref, wlb_ref, wdb_ref, gsem, ssem):
    w = pl.program_id(0)
    t = pl.program_id(1)
    last_w = MOE_ITEMS - 1
    n = item_n_ref[w]
    start = item_start_ref[w]
    slot = lax.rem(w, 2)
    w_next = jnp.minimum(w + 1, last_w)
    start_next = item_start_ref[w_next]
    w_prev = jnp.maximum(w - 1, 0)
    start_prev = item_start_ref[w_prev]
    n_prev = jnp.where(w > 0, item_n_ref[w_prev], 0)
    is_last_live = jnp.logical_or(w == last_w, item_n_ref[w_next] == 0)

    def gather_row(row, base):
        tk = tok_ref[base + row]
        pltpu.make_async_copy(hn_hbm.at[pl.ds(tk, 1), :], xg_ref.at[pl.ds(row, 1), :], gsem).start()

    def gather_wait():
        pltpu.make_async_copy(hn_hbm.at[pl.ds(0, MOE_ROWS), :], xg_ref, gsem).wait()

    def scatter_row(row, base, cnt, sl):
        d = jnp.where(row < cnt, dest_ref[base + row], N_ASSIGN + row)
        pltpu.make_async_copy(acc_ref.at[sl, pl.ds(row, 1), :], y_hbm.at[pl.ds(d, 1), :], ssem).start()

    def scatter_wait(sl):
        pltpu.make_async_copy(acc_ref.at[sl], y_hbm.at[pl.ds(0, MOE_ROWS), :], ssem).wait()

    @pl.when(jnp.logical_and(w == 0, t == 0))
    def _():
        acc_ref[1] = jnp.zeros((MOE_ROWS, D_MODEL), f32)

        def issue(r, c):
            gather_row(r, start)
            return c
        lax.fori_loop(0, MOE_ROWS, issue, 0, unroll=8)

    @pl.when(jnp.logical_and(t == 0, n > 0))
    def _():
        gather_wait()
        xb_ref[...] = xg_ref[...].astype(bf16)
        acc_ref[slot] = jnp.zeros((MOE_ROWS, D_MODEL), f32)

    def dma_chunk(r):
        row0 = t * (MOE_NSUB * MOE_CHUNK) + r * MOE_CHUNK
        for k in range(MOE_CHUNK):
            gather_row(row0 + k, start_next)
        for k in range(MOE_CHUNK):
            scatter_row(row0 + k, start_prev, n_prev, 1 - slot)

    def ffn_up(r0, rows):
        xs = xb_ref[r0:r0 + rows, :]
        hg = jnp.dot(xs, wgb_ref[...], preferred_element_type=f32) + bg_ref[...]
        hl = jnp.dot(xs, wlb_ref[...], preferred_element_type=f32) + bl_ref[...]
        return hg, hl

    def ffn_down(r0, rows, hg, hl):
        hg = jnp.minimum(hg, SWIGLU_LIMIT)
        hl = jnp.clip(hl, -SWIGLU_LIMIT, SWIGLU_LIMIT)
        act = hg * jax.nn.sigmoid(SWIGLU_ALPHA * hg) * (hl + 1.0)
        acc_ref[slot, r0:r0 + rows, :] += jnp.dot(act.astype(bf16), wdb_ref[...], preferred_element_type=f32)

    def ffn_rows(r0, rows):
        ffn_down(r0, rows, *ffn_up(r0, rows))

    @pl.when(n > 0)
    def _():
        wgb_ref[...] = wg_ref[...].astype(bf16)
        wlb_ref[...] = wl_ref[...].astype(bf16)
        wdb_ref[...] = wd_ref[...].astype(bf16)
        def one_block(r):
            r0 = r * MOE_SUB

            def full():
                dma_chunk(r)
                ffn_rows(r0, MOE_SUB)

            def half():
                dma_chunk(r)
                ffn_rows(r0, MOE_HALF)

            def idle():
                dma_chunk(r)

            lax.cond(n > r0 + MOE_HALF, full, lambda: lax.cond(n > r0, half, idle))

        for r in range(0, MOE_NSUB - 1, 2):
            def pair(r=r):
                dma_chunk(r)
                dma_chunk(r + 1)
                ra, rb = r * MOE_SUB, (r + 1) * MOE_SUB
                ha = ffn_up(ra, MOE_SUB)
                hb = ffn_up(rb, MOE_SUB)
                ffn_down(ra, MOE_SUB, *ha)
                ffn_down(rb, MOE_SUB, *hb)

            def singles(r=r):
                one_block(r)
                one_block(r + 1)

            lax.cond(n > (r + 1) * MOE_SUB + MOE_HALF, pair, singles)
        for r in range(MOE_NSUB - MOE_NSUB % 2, MOE_NSUB):
            one_block(r)

    @pl.when(jnp.logical_and(t == MOE_NF - 1, n > 0))
    def _():
        scatter_wait(1 - slot)
        acc_ref[slot] = acc_ref[slot] + bd_ref[...]

        @pl.when(is_last_live)
        def _():
            def issue(r, c):
                scatter_row(r, start, n, slot)
                return c
            lax.fori_loop(0, MOE_ROWS, issue, 0, unroll=8)
            scatter_wait(slot)
            gather_wait()


def _moe(item_e, item_start, item_n, tok_sorted, dest_sorted, hn2, w_gate_up, b_gate_up, w_down, b_down):
    tf = MOE_TF
    live_t = lambda w, t, n: jnp.where(n[w] > 0, t, MOE_NF - 1)
    grid_spec = pltpu.PrefetchScalarGridSpec(
        num_scalar_prefetch=5,
        grid=(MOE_ITEMS, MOE_NF),
        in_specs=[
            pl.BlockSpec(memory_space=pl.ANY),
            pl.BlockSpec((None, D_MODEL, tf), lambda w, t, e, s, n, tk, ds: (e[w], 0, live_t(w, t, n))),
            pl.BlockSpec((None, D_MODEL, tf), lambda w, t, e, s, n, tk, ds: (e[w], 0, MOE_NF + live_t(w, t, n))),
            pl.BlockSpec((None, 1, tf), lambda w, t, e, s, n, tk, ds: (e[w], 0, live_t(w, t, n))),
            pl.BlockSpec((None, 1, tf), lambda w, t, e, s, n, tk, ds: (e[w], 0, MOE_NF + live_t(w, t, n))),
            pl.BlockSpec((None, tf, D_MODEL), lambda w, t, e, s, n, tk, ds: (e[w], live_t(w, t, n), 0)),
            pl.BlockSpec((None, 1, D_MODEL), lambda w, t, e, s, n, tk, ds: (e[w], 0, 0)),
        ],
        out_specs=pl.BlockSpec(memory_space=pl.ANY),
        scratch_shapes=[
            pltpu.VMEM((2, MOE_ROWS, D_MODEL), f32),
            pltpu.VMEM((MOE_ROWS, D_MODEL), f32),
            pltpu.VMEM((MOE_ROWS, D_MODEL), bf16),
            pltpu.VMEM((D_MODEL, tf), bf16),
            pltpu.VMEM((D_MODEL, tf), bf16),
            pltpu.VMEM((tf, D_MODEL), bf16),
            pltpu.SemaphoreType.DMA,
            pltpu.SemaphoreType.DMA,
        ],
    )
    return pl.pallas_call(
        _moe_kernel,
        grid_spec=grid_spec,
        out_shape=jax.ShapeDtypeStruct((Y_ROWS, D_MODEL), f32),
        compiler_params=_params(("arbitrary", "arbitrary"), BIG_VMEM_LIMIT_BYTES),
        name="moe_experts",
    )(item_e, item_start, item_n, tok_sorted, dest_sorted,
      hn2, w_gate_up, w_gate_up, b_gate_up, b_gate_up, w_down, b_down)


def _combine_kernel(y0_ref, y1_ref, y2_ref, y3_ref, h_ref, gate_ref, g_ref, o_ref):
    h = h_ref[...]
    gates = gate_ref[...]
    for k, y_ref in enumerate((y0_ref, y1_ref, y2_ref, y3_ref)):
        h = h + y_ref[...] * gates[:, k:k + 1]
    o_ref[...] = h * lax.rsqrt(jnp.mean(h * h, axis=-1, keepdims=True) + RMS_EPS) * g_ref[...]


def _combine(y, h1, gates, gain, tm=256):
    slot_spec = lambda k: pl.BlockSpec((tm, D_MODEL), lambda i, k=k: (k * (N_TOK // tm) + i, 0))
    return pl.pallas_call(
        _combine_kernel,
        grid=(N_TOK // tm,),
        in_specs=[slot_spec(k) for k in range(TOP_K)] + [
            pl.BlockSpec((tm, D_MODEL), lambda i: (i, 0)),
            pl.BlockSpec((tm, TOP_K), lambda i: (i, 0)),
            pl.BlockSpec((1, D_MODEL), lambda i: (0, 0)),
        ],
        out_specs=pl.BlockSpec((tm, D_MODEL), lambda i: (i, 0)),
        out_shape=jax.ShapeDtypeStruct((N_TOK, D_MODEL), f32),
        compiler_params=_params(("parallel",)),
        name="combine_norm",
    )(y, y, y, y, h1, gates, gain)


def _route(logits):
    top_vals, top_idx = lax.top_k(logits, TOP_K)
    gates = jax.nn.softmax(top_vals, axis=-1)
    expert_flat = top_idx.reshape(-1).astype(jnp.int32)
    order = jnp.argsort(expert_flat, stable=True).astype(jnp.int32)
    counts = jnp.zeros((N_EXPERTS,), jnp.int32).at[expert_flat].add(1)
    starts = jnp.cumsum(counts) - counts
    tok_sorted = order // TOP_K
    dest_sorted = (order % TOP_K) * N_TOK + tok_sorted
    tok_sorted = jnp.pad(tok_sorted, (0, MOE_ROWS))
    dest_sorted = jnp.pad(dest_sorted, (0, MOE_ROWS))
    chunks = (counts + MOE_ROWS - 1) // MOE_ROWS
    chunk_end = jnp.cumsum(chunks)
    n_items = chunk_end[-1]
    w = jnp.arange(MOE_ITEMS, dtype=jnp.int32)
    w_eff = jnp.minimum(w, n_items - 1)
    e = jnp.sum(w_eff[:, None] >= chunk_end[None, :], axis=1).astype(jnp.int32)
    c = w_eff - (chunk_end - chunks)[e]
    item_start = starts[e] + c * MOE_ROWS
    item_n = jnp.where(w < n_items, jnp.clip(counts[e] - c * MOE_ROWS, 0, MOE_ROWS), 0)
    return gates, e, item_start.astype(jnp.int32), item_n.astype(jnp.int32), tok_sorted, dest_sorted


def kernel(x, meta_tokens, rel_bias_table, attn_norm, w_in, fox_forget_bias, lam_q1, lam_k1, lam_q2, lam_k2,
           diff_subln, w_branch_fox, w_branch_diff, w_out, ffn_norm, w_router, b_router, w_gate_up, b_gate_up,
           w_down, b_down, final_norm):
    x2d = x.reshape(N_TOK, D_MODEL)
    w_t = jnp.swapaxes(w_in[0], 0, 1)
    fb_pad = jnp.pad(fox_forget_bias[0], (0, LANES - FOX_HEADS)).reshape(1, LANES)

    proj, ff_real = _inproj(x2d, attn_norm, w_t, tm=1024, tn=512)
    meta_pad = jnp.pad(meta_tokens, ((0, N_META_PAD - N_META), (0, 0)))
    proj_meta, ff_meta = _inproj(meta_pad, attn_norm, w_t, tm=N_META_PAD, tn=512)

    ck, ckm = _forget_cumsum(ff_real, ff_meta, fb_pad)
    o_fox = _fox_attention(proj, proj_meta, ck, ckm)

    near, mbias = _bias_tiles(rel_bias_table)
    lam_vecs = jnp.concatenate([lam_q1, lam_k1, lam_q2, lam_k2], axis=0)
    o_diff = _diff_attention(proj, proj_meta, near, mbias, lam_vecs, diff_subln)

    merged = _gated_merge(o_fox, o_diff, w_branch_fox[0].astype(bf16), w_branch_diff[0].astype(bf16), proj)
    w_router_pad = jnp.pad(w_router[0], ((0, 0), (0, LANES - N_EXPERTS)))
    w_router_hi = w_router_pad.astype(bf16)
    w_router_lo = (w_router_pad - w_router_hi.astype(f32)).astype(bf16)
    w_router_split = jnp.concatenate([w_router_hi, w_router_lo], axis=1)
    b_router_pad = jnp.pad(b_router[0], (0, LANES - N_EXPERTS)).reshape(1, LANES)
    h1, hn2, logits = _outproj(merged, w_out[0].astype(bf16), x2d, ffn_norm, w_router_split, b_router_pad)

    gates, item_e, item_start, item_n, tok_sorted, dest_sorted = _route(logits[:, :N_EXPERTS])
    y = _moe(item_e, item_start, item_n, tok_sorted, dest_sorted, hn2,
             w_gate_up[0], b_gate_up[0].reshape(N_EXPERTS, 1, 2 * D_EXPERT),
             w_down[0], b_down[0].reshape(N_EXPERTS, 1, D_MODEL))
    out = _combine(y, h1, gates, final_norm.reshape(1, D_MODEL))
    return out.reshape(BATCH, SEQ, D_MODEL)
```

```python
import functools
import math

import jax
import jax.numpy as jnp
from jax import lax
from jax.experimental import pallas as pl
from jax.experimental.pallas import tpu as pltpu

D_MODEL = 2048
BATCH = 4
SEQ = 2048
N_TOK = BATCH * SEQ
N_META = 16
N_META_PAD = 128
HEAD_DIM = 128
FOX_HEADS = 8
DIFF_HEADS = 4
DIFF_V_DIM = 2 * HEAD_DIM
FOX_WIDTH = FOX_HEADS * HEAD_DIM
DIFF_QK_WIDTH = DIFF_HEADS * 2 * HEAD_DIM
DIFF_WIDTH = DIFF_HEADS * DIFF_V_DIM
N_BUCKETS = 32
MAX_DISTANCE = 128
N_EXPERTS = 32
TOP_K = 4
D_EXPERT = D_MODEL
SWIGLU_LIMIT = 7.0
SWIGLU_ALPHA = 1.702
RMS_EPS = 1e-5
NEG_INF = -1e30
LAMBDA_INIT = 0.8 - 0.6 * math.exp(-0.3 * 0)
ATTN_SCALE = HEAD_DIM ** -0.5
LOG2E = math.log2(math.e)
Q_PRESCALE = ATTN_SCALE * LOG2E

LANES = 128
SUBLANES = 8
VMEM_LIMIT_BYTES = 56 * 1024 * 1024
BIG_VMEM_LIMIT_BYTES = 60 * 1024 * 1024

COL_FQ = 0
COL_FK = COL_FQ + FOX_WIDTH
COL_FV = COL_FK + FOX_WIDTH
COL_DQ = COL_FV + FOX_WIDTH
COL_DK = COL_DQ + DIFF_QK_WIDTH
COL_DV = COL_DK + DIFF_QK_WIDTH
COL_GF = COL_DV + DIFF_WIDTH
COL_GD = COL_GF + D_MODEL
D_PROJ = COL_GD + D_MODEL

ATTN_TILE = 512
N_QBLK = SEQ // ATTN_TILE

MOE_ROWS = 1280
MOE_SUB = 256
MOE_HALF = MOE_SUB // 2
MOE_NSUB = MOE_ROWS // MOE_SUB
MOE_TF = 256
MOE_NF = D_EXPERT // MOE_TF
MOE_CHUNK = MOE_ROWS // (MOE_NF * MOE_NSUB)
N_ASSIGN = N_TOK * TOP_K
MOE_ITEMS = N_EXPERTS + N_ASSIGN // MOE_ROWS
Y_ROWS = N_ASSIGN + MOE_ROWS

f32 = jnp.float32
bf16 = jnp.bfloat16


def _params(sem, vmem=VMEM_LIMIT_BYTES):
    return pltpu.CompilerParams(dimension_semantics=sem, vmem_limit_bytes=vmem)


def _inproj_kernel(x_hbm, g_ref, wt_ref, wfft_ref, o_ref, ff_ref, xn_ref, xt_ref, wb_ref, wffb_ref, sem, *, tm, tn):
    j = pl.program_id(0)
    i = pl.program_id(1)
    rows = pl.ds(pl.multiple_of(i * tm, tm), tm)

    @pl.when(i == 0)
    def _():
        wb_ref[...] = wt_ref[...].T.astype(bf16)

    @pl.when(j == 0)
    def _():
        @pl.when(i == 0)
        def _():
            wffb_ref[...] = wfft_ref[...].T.astype(bf16)

        cp = pltpu.make_async_copy(x_hbm.at[rows, :], xt_ref, sem)
        cp.start()
        cp.wait()
        x = xt_ref[...]
        ms = jnp.mean(x * x, axis=-1, keepdims=True)
        xn = (x * lax.rsqrt(ms + RMS_EPS) * g_ref[...]).astype(bf16)
        xn_ref[rows, :] = xn
        ff_ref[...] = jnp.dot(xn, wffb_ref[...], preferred_element_type=f32)

    is_q = jnp.logical_or(j < COL_FK // tn, jnp.logical_and(j >= COL_DQ // tn, j < COL_DK // tn))
    qs = jnp.where(is_q, Q_PRESCALE, 1.0)
    y = jnp.dot(xn_ref[rows, :], wb_ref[...], preferred_element_type=f32)
    o_ref[...] = (y * qs).astype(bf16)


def _inproj(x2d, gain, w_t, tm, tn):
    n = x2d.shape[0]
    n_i = n // tm
    w_row = lambda j: SUBLANES * (j * (tn // SUBLANES) + jnp.where(j * tn >= COL_DQ, FOX_HEADS // SUBLANES, 0))
    first_pass_row = lambda j, i: jnp.where(j == 0, i, n_i - 1)
    return pl.pallas_call(
        functools.partial(_inproj_kernel, tm=tm, tn=tn),
        grid=(D_PROJ // tn, n_i),
        in_specs=[
            pl.BlockSpec(memory_space=pl.ANY),
            pl.BlockSpec((1, D_MODEL), lambda j, i: (0, 0)),
            pl.BlockSpec((pl.Element(tn), pl.Element(D_MODEL)), lambda j, i: (w_row(j), 0)),
            pl.BlockSpec((pl.Element(LANES), pl.Element(D_MODEL)), lambda j, i: (COL_DQ, 0)),
        ],
        out_specs=[
            pl.BlockSpec((tm, tn), lambda j, i: (i, j)),
            pl.BlockSpec((tm, LANES), lambda j, i: (first_pass_row(j, i), 0)),
        ],
        out_shape=[
            jax.ShapeDtypeStruct((n, D_PROJ), bf16),
            jax.ShapeDtypeStruct((n, LANES), f32),
        ],
        scratch_shapes=[
            pltpu.VMEM((n, D_MODEL), bf16),
            pltpu.VMEM((tm, D_MODEL), f32),
            pltpu.VMEM((D_MODEL, tn), bf16),
            pltpu.VMEM((D_MODEL, LANES), bf16),
            pltpu.SemaphoreType.DMA,
        ],
        compiler_params=_params(("arbitrary", "arbitrary"), BIG_VMEM_LIMIT_BYTES),
        name="inproj",
    )(x2d, gain, w_t, w_t)


def _log_sigmoid(x):
    return jnp.minimum(x, 0.0) - jnp.log(1.0 + jnp.exp(-jnp.abs(x)))


def _lane_cumsum(x):
    n = x.shape[-1]
    lane = lax.broadcasted_iota(jnp.int32, x.shape, x.ndim - 1)
    s = 1
    while s < n:
        x = x + jnp.where(lane >= s, pltpu.roll(x, s, x.ndim - 1), 0.0)
        s *= 2
    return x


def _cum_kernel(ff_ref, ffm_ref, fb_ref, ck_ref, ckm_ref):
    fb = fb_ref[...]
    row = lax.broadcasted_iota(jnp.int32, (LANES, LANES), 0)
    lfm = jnp.where(row < N_META, _log_sigmoid(ffm_ref[...] + fb), 0.0)
    cm = _lane_cumsum(lfm.T)
    m_total = cm[:, N_META - 1:N_META]
    lf = _log_sigmoid(ff_ref[...] + fb)
    cr = _lane_cumsum(lf.T) + m_total
    ck_ref[...] = cr[:SUBLANES] * LOG2E
    ckm_ref[...] = cm[:SUBLANES] * LOG2E


def _forget_cumsum(ff_real, ff_meta_pad, fb_pad):
    return pl.pallas_call(
        _cum_kernel,
        grid=(BATCH,),
        in_specs=[
            pl.BlockSpec((SEQ, LANES), lambda b: (b, 0)),
            pl.BlockSpec((LANES, LANES), lambda b: (0, 0)),
            pl.BlockSpec((1, LANES), lambda b: (0, 0)),
        ],
        out_specs=[
            pl.BlockSpec((None, SUBLANES, SEQ), lambda b: (b, 0, 0)),
            pl.BlockSpec((None, SUBLANES, LANES), lambda b: (b, 0, 0)),
        ],
        out_shape=[
            jax.ShapeDtypeStruct((BATCH, FOX_HEADS, SEQ), f32),
            jax.ShapeDtypeStruct((BATCH, FOX_HEADS, LANES), f32),
        ],
        compiler_params=_params(("parallel",)),
        name="forget_cumsum",
    )(ff_real, ff_meta_pad, fb_pad)


def _qk(q, k):
    return lax.dot_general(q, k, (((1,), (1,)), ((), ())), preferred_element_type=f32)


def _row_to_col(row):
    t = row.shape[-1]
    r = lax.broadcasted_iota(jnp.int32, (t, t), 0)
    c = lax.broadcasted_iota(jnp.int32, (t, t), 1)
    return jnp.sum(jnp.where(r == c, row, 0.0), axis=1, keepdims=True)


def _online_update(carry, s, v):
    m, l, acc = carry
    m_new = jnp.maximum(m, jnp.max(s, axis=-1, keepdims=True))
    alpha = jnp.exp2(m - m_new)
    p = jnp.exp2(s - m_new)
    l = alpha * l + jnp.sum(p, axis=-1, keepdims=True)
    acc = alpha * acc + jnp.dot(p.astype(bf16), v, preferred_element_type=f32)
    return m_new, l, acc


def _first_update(s, v):
    m = jnp.max(s, axis=-1, keepdims=True)
    p = jnp.exp2(s - m)
    l = jnp.sum(p, axis=-1, keepdims=True)
    acc = jnp.dot(p.astype(bf16), v, preferred_element_type=f32)
    return m, l, acc


FOX_GROUP = 2


def _fox_update(carry, u, cq, v):
    m_blk = jnp.max(u, axis=-1, keepdims=True) + cq
    if carry is None:
        m_new = m_blk
    else:
        m, l, acc = carry
        m_new = jnp.maximum(m, m_blk)
    p = jnp.exp2(u - (m_new - cq))
    ps = jnp.sum(p, axis=-1, keepdims=True)
    pv = jnp.dot(p.astype(bf16), v, preferred_element_type=f32)
    if carry is None:
        return m_new, ps, pv
    alpha = jnp.exp2(m - m_new)
    return m_new, alpha * l + ps, alpha * acc + pv


def _fox_kernel(q_ref, k_ref, v_ref, km_ref, vm_ref, ck_ref, ckm_ref, o_ref):
    hg = pl.program_id(1)
    i = pl.program_id(2)
    t = ATTN_TILE
    q0 = pl.multiple_of(i * t, t)
    heads = range(FOX_GROUP)
    col = lambda g: slice(g * HEAD_DIM, (g + 1) * HEAD_DIM)
    ck_row = lambda g, k0: ck_ref[pl.ds(hg * FOX_GROUP + g, 1), pl.ds(k0, t)]
    qs = [q_ref[:, col(g)] for g in heads]
    cqs = [_row_to_col(ck_row(g, q0)) for g in heads]

    meta_ok = lax.broadcasted_iota(jnp.int32, (t, N_META_PAD), 1) < N_META
    us = [_qk(qs[g], km_ref[:, col(g)]) - ckm_ref[pl.ds(hg * FOX_GROUP + g, 1), :] for g in heads]
    carry = tuple(_fox_update(None, jnp.where(meta_ok, us[g], NEG_INF), cqs[g], vm_ref[:, col(g)]) for g in heads)

    def body(j, carry):
        k0 = pl.multiple_of(j * t, t)
        us = [_qk(qs[g], k_ref[pl.ds(k0, t), col(g)]) - ck_row(g, k0) for g in heads]
        return tuple(_fox_update(carry[g], us[g], cqs[g], v_ref[pl.ds(k0, t), col(g)]) for g in heads)

    carry = lax.fori_loop(0, i, body, carry)

    r = lax.broadcasted_iota(jnp.int32, (t, t), 0)
    c = lax.broadcasted_iota(jnp.int32, (t, t), 1)
    us = [_qk(qs[g], k_ref[pl.ds(q0, t), col(g)]) - ck_row(g, q0) for g in heads]
    for g in heads:
        _, l, acc = _fox_update(carry[g], jnp.where(c <= r, us[g], NEG_INF), cqs[g], v_ref[pl.ds(q0, t), col(g)])
        o_ref[:, col(g)] = (acc / l).astype(bf16)


def _fox_attention(proj, proj_meta, ck, ckm):
    t = ATTN_TILE
    w = FOX_GROUP * HEAD_DIM
    cb = lambda col: col // w
    return pl.pallas_call(
        _fox_kernel,
        grid=(BATCH, FOX_HEADS // FOX_GROUP, N_QBLK),
        in_specs=[
            pl.BlockSpec((t, w), lambda b, h, i: (b * N_QBLK + i, cb(COL_FQ) + h)),
            pl.BlockSpec((SEQ, w), lambda b, h, i: (b, cb(COL_FK) + h)),
            pl.BlockSpec((SEQ, w), lambda b, h, i: (b, cb(COL_FV) + h)),
            pl.BlockSpec((N_META_PAD, w), lambda b, h, i: (0, cb(COL_FK) + h)),
            pl.BlockSpec((N_META_PAD, w), lambda b, h, i: (0, cb(COL_FV) + h)),
            pl.BlockSpec((None, FOX_HEADS, SEQ), lambda b, h, i: (b, 0, 0)),
            pl.BlockSpec((None, FOX_HEADS, LANES), lambda b, h, i: (b, 0, 0)),
        ],
        out_specs=pl.BlockSpec((t, w), lambda b, h, i: (b * N_QBLK + i, h)),
        out_shape=jax.ShapeDtypeStruct((N_TOK, FOX_WIDTH), bf16),
        compiler_params=_params(("parallel", "parallel", "arbitrary")),
        name="fox_attention",
    )(proj, proj, proj, proj_meta, proj_meta, ck, ckm)


def _t5_bias(dist, table_ref, h):
    n = jnp.maximum(dist, 0)
    max_exact = N_BUCKETS // 2
    log_part = jnp.log(jnp.maximum(n, 1).astype(f32) / max_exact) / math.log(MAX_DISTANCE / max_exact)
    v = log_part * (N_BUCKETS - max_exact)
    far = table_ref[N_BUCKETS - 1, h]
    val = lambda b: (table_ref[b, h] - far) * LOG2E
    large = jnp.zeros(dist.shape, f32)
    for b in range(N_BUCKETS - 2, max_exact - 1, -1):
        large = jnp.where(v < b + 1 - max_exact, val(b), large)
    out = large
    for b in range(max_exact):
        out = jnp.where(n == b, val(b), out)
    return out


def _bias_kernel(table_ref, near_ref, meta_ref):
    h = pl.program_id(0)
    t = ATTN_TILE
    r = lax.broadcasted_iota(jnp.int32, (t, 2 * t), 0)
    c = lax.broadcasted_iota(jnp.int32, (t, 2 * t), 1)
    near_ref[...] = _t5_bias(r + t - c, table_ref, h)
    r = lax.broadcasted_iota(jnp.int32, (t, LANES), 0)
    c = lax.broadcasted_iota(jnp.int32, (t, LANES), 1)
    meta_ref[...] = _t5_bias(N_META + r - c, table_ref, h)


def _bias_tiles(table):
    t = ATTN_TILE
    return pl.pallas_call(
        _bias_kernel,
        grid=(DIFF_HEADS,),
        in_specs=[pl.BlockSpec(memory_space=pltpu.SMEM)],
        out_specs=[
            pl.BlockSpec((None, t, 2 * t), lambda h: (h, 0, 0)),
            pl.BlockSpec((None, t, LANES), lambda h: (h, 0, 0)),
        ],
        out_shape=[
            jax.ShapeDtypeStruct((DIFF_HEADS, t, 2 * t), f32),
            jax.ShapeDtypeStruct((DIFF_HEADS, t, LANES), f32),
        ],
        compiler_params=_params(("arbitrary",)),
        name="t5_bias_tiles",
    )(table)


def _diff_kernel(q1_ref, q2_ref, k1_ref, k2_ref, v_ref, k1m_ref, k2m_ref, vm_ref,
                 near_ref, mbias_ref, lam_ref, subln_ref, o_ref):
    i = pl.program_id(2)
    t = ATTN_TILE
    q1 = q1_ref[...]
    q2 = q2_ref[...]
    q0 = pl.multiple_of(i * t, t)
    lam = (jnp.exp(jnp.sum(lam_ref[0:1, :] * lam_ref[1:2, :], axis=-1, keepdims=True))
           - jnp.exp(jnp.sum(lam_ref[2:3, :] * lam_ref[3:4, :], axis=-1, keepdims=True))
           + LAMBDA_INIT)

    mb = jnp.where(i == 0, mbias_ref[...], 0.0)
    meta_ok = lax.broadcasted_iota(jnp.int32, (t, N_META_PAD), 1) < N_META
    vm = vm_ref[...]
    c1 = _first_update(jnp.where(meta_ok, _qk(q1, k1m_ref[...]) + mb, NEG_INF), vm)
    c2 = _first_update(jnp.where(meta_ok, _qk(q2, k2m_ref[...]) + mb, NEG_INF), vm)

    def far_body(j, carry):
        c1, c2 = carry
        k0 = pl.multiple_of(j * t, t)
        v = v_ref[pl.ds(k0, t), :]
        c1 = _online_update(c1, _qk(q1, k1_ref[pl.ds(k0, t), :]), v)
        c2 = _online_update(c2, _qk(q2, k2_ref[pl.ds(k0, t), :]), v)
        return c1, c2

    c1, c2 = lax.fori_loop(0, jnp.maximum(i - 1, 0), far_body, (c1, c2))

    def prev_block(carry):
        c1, c2 = carry
        k0 = pl.multiple_of((i - 1) * t, t)
        v = v_ref[pl.ds(k0, t), :]
        bias = near_ref[:, :t]
        c1 = _online_update(c1, _qk(q1, k1_ref[pl.ds(k0, t), :]) + bias, v)
        c2 = _online_update(c2, _qk(q2, k2_ref[pl.ds(k0, t), :]) + bias, v)
        return c1, c2

    c1, c2 = lax.cond(i > 0, prev_block, lambda carry: carry, (c1, c2))

    r = lax.broadcasted_iota(jnp.int32, (t, t), 0)
    c = lax.broadcasted_iota(jnp.int32, (t, t), 1)
    mask = c <= r
    bias = near_ref[:, t:]
    v = v_ref[pl.ds(q0, t), :]
    s1 = jnp.where(mask, _qk(q1, k1_ref[pl.ds(q0, t), :]) + bias, NEG_INF)
    s2 = jnp.where(mask, _qk(q2, k2_ref[pl.ds(q0, t), :]) + bias, NEG_INF)
    _, l1, a1 = _online_update(c1, s1, v)
    _, l2, a2 = _online_update(c2, s2, v)

    o = a1 / l1 - lam * (a2 / l2)
    y = o * lax.rsqrt(jnp.mean(o * o, axis=-1, keepdims=True) + RMS_EPS) * subln_ref[...]
    o_ref[...] = (y * (1.0 - LAMBDA_INIT)).astype(bf16)


def _diff_attention(proj, proj_meta, near, mbias, lam_vecs, subln):
    t = ATTN_TILE
    cb = lambda col: col // HEAD_DIM
    vb = lambda col: col // DIFF_V_DIM
    row = lambda b, h, i: b * N_QBLK + i
    return pl.pallas_call(
        _diff_kernel,
        grid=(BATCH, DIFF_HEADS, N_QBLK),
        in_specs=[
            pl.BlockSpec((t, HEAD_DIM), lambda b, h, i: (row(b, h, i), cb(COL_DQ) + 2 * h)),
            pl.BlockSpec((t, HEAD_DIM), lambda b, h, i: (row(b, h, i), cb(COL_DQ) + 2 * h + 1)),
            pl.BlockSpec((SEQ, HEAD_DIM), lambda b, h, i: (b, cb(COL_DK) + 2 * h)),
            pl.BlockSpec((SEQ, HEAD_DIM), lambda b, h, i: (b, cb(COL_DK) + 2 * h + 1)),
            pl.BlockSpec((SEQ, DIFF_V_DIM), lambda b, h, i: (b, vb(COL_DV) + h)),
            pl.BlockSpec((N_META_PAD, HEAD_DIM), lambda b, h, i: (0, cb(COL_DK) + 2 * h)),
            pl.BlockSpec((N_META_PAD, HEAD_DIM), lambda b, h, i: (0, cb(COL_DK) + 2 * h + 1)),
            pl.BlockSpec((N_META_PAD, DIFF_V_DIM), lambda b, h, i: (0, vb(COL_DV) + h)),
            pl.BlockSpec((None, t, 2 * t), lambda b, h, i: (h, 0, 0)),
            pl.BlockSpec((None, t, LANES), lambda b, h, i: (h, 0, 0)),
            pl.BlockSpec((4, HEAD_DIM), lambda b, h, i: (0, 0)),
            pl.BlockSpec((1, DIFF_V_DIM), lambda b, h, i: (0, 0)),
        ],
        out_specs=pl.BlockSpec((t, DIFF_V_DIM), lambda b, h, i: (row(b, h, i), h)),
        out_shape=jax.ShapeDtypeStruct((N_TOK, DIFF_WIDTH), bf16),
        compiler_params=_params(("parallel", "parallel", "arbitrary")),
        name="diff_attention",
    )(proj, proj, proj, proj, proj, proj_meta, proj_meta, proj_meta, near, mbias, lam_vecs, subln)


def _merge_kernel(of_ref, od_ref, wf_ref, wd_ref, gf_ref, gd_ref, o_ref):
    yf = jnp.dot(of_ref[...], wf_ref[...], preferred_element_type=f32)
    yd = jnp.dot(od_ref[...], wd_ref[...], preferred_element_type=f32)
    gf = jax.nn.sigmoid(gf_ref[...].astype(f32))
    gd = jax.nn.sigmoid(gd_ref[...].astype(f32))
    o_ref[...] = (gf * yf + gd * yd).astype(bf16)


def _gated_merge(o_fox, o_diff, w_bf, w_bd, proj, tm=1024, tn=512):
    return pl.pallas_call(
        _merge_kernel,
        grid=(D_MODEL // tn, N_TOK // tm),
        in_specs=[
            pl.BlockSpec((tm, FOX_WIDTH), lambda j, i: (i, 0)),
            pl.BlockSpec((tm, DIFF_WIDTH), lambda j, i: (i, 0)),
            pl.BlockSpec((FOX_WIDTH, tn), lambda j, i: (0, j)),
            pl.BlockSpec((DIFF_WIDTH, tn), lambda j, i: (0, j)),
            pl.BlockSpec((tm, tn), lambda j, i: (i, COL_GF // tn + j)),
            pl.BlockSpec((tm, tn), lambda j, i: (i, COL_GD // tn + j)),
        ],
        out_specs=pl.BlockSpec((tm, tn), lambda j, i: (i, j)),
        out_shape=jax.ShapeDtypeStruct((N_TOK, D_MODEL), bf16),
        compiler_params=_params(("parallel", "arbitrary")),
        name="gated_merge",
    )(o_fox, o_diff, w_bf, w_bd, proj, proj)


def _outproj_kernel(m_ref, w_ref, x_ref, g_ref, wr_ref, br_ref, h_ref, hn_ref, lg_ref):
    h1 = x_ref[...] + jnp.dot(m_ref[...], w_ref[...], preferred_element_type=f32)
    h_ref[...] = h1
    hn = h1 * lax.rsqrt(jnp.mean(h1 * h1, axis=-1, keepdims=True) + RMS_EPS) * g_ref[...]
    hn_ref[...] = hn
    hn_hi = hn.astype(bf16)
    hn_lo = (hn - hn_hi.astype(f32)).astype(bf16)
    a = jnp.dot(hn_hi, wr_ref[...], preferred_element_type=f32)
    b = jnp.dot(hn_lo, wr_ref[:, :LANES], preferred_element_type=f32)
    lg_ref[...] = a[:, :LANES] + a[:, LANES:] + b + br_ref[...]


def _outproj(merged, w_out, x2d, gain, w_router_split, b_router_pad, tm=512):
    return pl.pallas_call(
        _outproj_kernel,
        grid=(N_TOK // tm,),
        in_specs=[
            pl.BlockSpec((tm, D_MODEL), lambda i: (i, 0)),
            pl.BlockSpec((D_MODEL, D_MODEL), lambda i: (0, 0)),
            pl.BlockSpec((tm, D_MODEL), lambda i: (i, 0)),
            pl.BlockSpec((1, D_MODEL), lambda i: (0, 0)),
            pl.BlockSpec((D_MODEL, 2 * LANES), lambda i: (0, 0)),
            pl.BlockSpec((1, LANES), lambda i: (0, 0)),
        ],
        out_specs=[
            pl.BlockSpec((tm, D_MODEL), lambda i: (i, 0)),
            pl.BlockSpec((tm, D_MODEL), lambda i: (i, 0)),
            pl.BlockSpec((tm, LANES), lambda i: (i, 0)),
        ],
        out_shape=[
            jax.ShapeDtypeStruct((N_TOK, D_MODEL), f32),
            jax.ShapeDtypeStruct((N_TOK, D_MODEL), f32),
            jax.ShapeDtypeStruct((N_TOK, LANES), f32),
        ],
        compiler_params=_params(("parallel",)),
        name="outproj_router",
    )(merged, w_out, x2d, gain, w_router_split, b_router_pad)


def _moe_kernel(item_e_ref, item_start_ref, item_n_ref, tok_ref, dest_ref,
                hn_hbm, wg_ref, wl_ref, bg_ref, bl_ref, wd_ref, bd_ref, y_hbm,
                acc_ref, xg_ref, xb_ref, wgb_ref, wlb_ref, wdb_ref, gsem, ssem):
    w = pl.program_id(0)
    t = pl.program_id(1)
    last_w = MOE_ITEMS - 1
    n = item_n_ref[w]
    start = item_start_ref[w]
    slot = lax.rem(w, 2)
    w_next = jnp.minimum(w + 1, last_w)
    start_next = item_start_ref[w_next]
    w_prev = jnp.maximum(w - 1, 0)
    start_prev = item_start_ref[w_prev]
    n_prev = jnp.where(w > 0, item_n_ref[w_prev], 0)
    is_last_live = jnp.logical_or(w == last_w, item_n_ref[w_next] == 0)

    def gather_row(row, base):
        tk = tok_ref[base + row]
        pltpu.make_async_copy(hn_hbm.at[pl.ds(tk, 1), :], xg_ref.at[pl.ds(row, 1), :], gsem).start()

    def gather_wait():
        pltpu.make_async_copy(hn_hbm.at[pl.ds(0, MOE_ROWS), :], xg_ref, gsem).wait()

    def scatter_row(row, base, cnt, sl):
        d = jnp.where(row < cnt, dest_ref[base + row], N_ASSIGN + row)
        pltpu.make_async_copy(acc_ref.at[sl, pl.ds(row, 1), :], y_hbm.at[pl.ds(d, 1), :], ssem).start()

    def scatter_wait(sl):
        pltpu.make_async_copy(acc_ref.at[sl], y_hbm.at[pl.ds(0, MOE_ROWS), :], ssem).wait()

    @pl.when(jnp.logical_and(w == 0, t == 0))
    def _():
        acc_ref[1] = jnp.zeros((MOE_ROWS, D_MODEL), f32)

        def issue(r, c):
            gather_row(r, start)
            return c
        lax.fori_loop(0, MOE_ROWS, issue, 0, unroll=8)

    @pl.when(jnp.logical_and(t == 0, n > 0))
    def _():
        gather_wait()
        xb_ref[...] = xg_ref[...].astype(bf16)
        acc_ref[slot] = jnp.zeros((MOE_ROWS, D_MODEL), f32)

    def dma_chunk(r):
        row0 = t * (MOE_NSUB * MOE_CHUNK) + r * MOE_CHUNK
        for k in range(MOE_CHUNK):
            gather_row(row0 + k, start_next)
        for k in range(MOE_CHUNK):
            scatter_row(row0 + k, start_prev, n_prev, 1 - slot)

    def ffn_up(r0, rows):
        xs = xb_ref[r0:r0 + rows, :]
        hg = jnp.dot(xs, wgb_ref[...], preferred_element_type=f32) + bg_ref[...]
        hl = jnp.dot(xs, wlb_ref[...], preferred_element_type=f32) + bl_ref[...]
        return hg, hl

    def ffn_down(r0, rows, hg, hl):
        hg = jnp.minimum(hg, SWIGLU_LIMIT)
        hl = jnp.clip(hl, -SWIGLU_LIMIT, SWIGLU_LIMIT)
        act = hg * jax.nn.sigmoid(SWIGLU_ALPHA * hg) * (hl + 1.0)
        acc_ref[slot, r0:r0 + rows, :] += jnp.dot(act.astype(bf16), wdb_ref[...], preferred_element_type=f32)

    def ffn_rows(r0, rows):
        ffn_down(r0, rows, *ffn_up(r0, rows))

    @pl.when(n > 0)
    def _():
        def step(nb, last_rows):
            def body():
                for r in range(MOE_NSUB):
                    dma_chunk(r)
                wgb_ref[...] = wg_ref[...].astype(bf16)
                wlb_ref[...] = wl_ref[...].astype(bf16)
                wdb_ref[...] = wd_ref[...].astype(bf16)
                rows = [MOE_SUB] * (nb - 1) + [last_rows]
                hs = [ffn_up(r * MOE_SUB, rows[r]) for r in range(nb)]
                for r in range(nb):
                    ffn_down(r * MOE_SUB, rows[r], *hs[r])
            return body

        def dispatch(nb):
            def pick(lo, hi):
                if lo == hi:
                    last_live = n - (lo - 1) * MOE_SUB
                    return lambda: lax.cond(last_live > MOE_HALF, step(lo, MOE_SUB), step(lo, MOE_HALF))
                mid = (lo + hi) // 2
                return lambda: lax.cond(nb <= mid, pick(lo, mid), pick(mid + 1, hi))
            pick(1, MOE_NSUB)()

        dispatch((n + MOE_SUB - 1) // MOE_SUB)

    @pl.when(jnp.logical_and(t == MOE_NF - 1, n > 0))
    def _():
        scatter_wait(1 - slot)
        acc_ref[slot] = acc_ref[slot] + bd_ref[...]

        @pl.when(is_last_live)
        def _():
            def issue(r, c):
                scatter_row(r, start, n, slot)
                return c
            lax.fori_loop(0, MOE_ROWS, issue, 0, unroll=8)
            scatter_wait(slot)
            gather_wait()


def _moe(item_e, item_start, item_n, tok_sorted, dest_sorted, hn2, w_gate_up, b_gate_up, w_down, b_down):
    tf = MOE_TF
    live_t = lambda w, t, n: jnp.where(n[w] > 0, t, MOE_NF - 1)
    grid_spec = pltpu.PrefetchScalarGridSpec(
        num_scalar_prefetch=5,
        grid=(MOE_ITEMS, MOE_NF),
        in_specs=[
            pl.BlockSpec(memory_space=pl.ANY),
            pl.BlockSpec((None, D_MODEL, tf), lambda w, t, e, s, n, tk, ds: (e[w], 0, live_t(w, t, n))),
            pl.BlockSpec((None, D_MODEL, tf), lambda w, t, e, s, n, tk, ds: (e[w], 0, MOE_NF + live_t(w, t, n))),
            pl.BlockSpec((None, 1, tf), lambda w, t, e, s, n, tk, ds: (e[w], 0, live_t(w, t, n))),
            pl.BlockSpec((None, 1, tf), lambda w, t, e, s, n, tk, ds: (e[w], 0, MOE_NF + live_t(w, t, n))),
            pl.BlockSpec((None, tf, D_MODEL), lambda w, t, e, s, n, tk, ds: (e[w], live_t(w, t, n), 0)),
            pl.BlockSpec((None, 1, D_MODEL), lambda w, t, e, s, n, tk, ds: (e[w], 0, 0)),
        ],
        out_specs=pl.BlockSpec(memory_space=pl.ANY),
        scratch_shapes=[
            pltpu.VMEM((2, MOE_ROWS, D_MODEL), f32),
            pltpu.VMEM((MOE_ROWS, D_MODEL), f32),
            pltpu.VMEM((MOE_ROWS, D_MODEL), bf16),
            pltpu.VMEM((D_MODEL, tf), bf16),
            pltpu.VMEM((D_MODEL, tf), bf16),
            pltpu.VMEM((tf, D_MODEL), bf16),
            pltpu.SemaphoreType.DMA,
            pltpu.SemaphoreType.DMA,
        ],
    )
    return pl.pallas_call(
        _moe_kernel,
        grid_spec=grid_spec,
        out_shape=jax.ShapeDtypeStruct((Y_ROWS, D_MODEL), f32),
        compiler_params=_params(("arbitrary", "arbitrary"), BIG_VMEM_LIMIT_BYTES),
        name="moe_experts",
    )(item_e, item_start, item_n, tok_sorted, dest_sorted,
      hn2, w_gate_up, w_gate_up, b_gate_up, b_gate_up, w_down, b_down)


def _combine_kernel(y0_ref, y1_ref, y2_ref, y3_ref, h_ref, gate_ref, g_ref, o_ref):
    h = h_ref[...]
    gates = gate_ref[...]
    for k, y_ref in enumerate((y0_ref, y1_ref, y2_ref, y3_ref)):
        h = h + y_ref[...] * gates[:, k:k + 1]
    o_ref[...] = h * lax.rsqrt(jnp.mean(h * h, axis=-1, keepdims=True) + RMS_EPS) * g_ref[...]


def _combine(y, h1, gates, gain, tm=256):
    slot_spec = lambda k: pl.BlockSpec((tm, D_MODEL), lambda i, k=k: (k * (N_TOK // tm) + i, 0))
    return pl.pallas_call(
        _combine_kernel,
        grid=(N_TOK // tm,),
        in_specs=[slot_spec(k) for k in range(TOP_K)] + [
            pl.BlockSpec((tm, D_MODEL), lambda i: (i, 0)),
            pl.BlockSpec((tm, TOP_K), lambda i: (i, 0)),
            pl.BlockSpec((1, D_MODEL), lambda i: (0, 0)),
        ],
        out_specs=pl.BlockSpec((tm, D_MODEL), lambda i: (i, 0)),
        out_shape=jax.ShapeDtypeStruct((N_TOK, D_MODEL), f32),
        compiler_params=_params(("parallel",)),
        name="combine_norm",
    )(y, y, y, y, h1, gates, gain)


def _route(logits):
    top_vals, top_idx = lax.top_k(logits, TOP_K)
    gates = jax.nn.softmax(top_vals, axis=-1)
    expert_flat = top_idx.reshape(-1).astype(jnp.int32)
    order = jnp.argsort(expert_flat, stable=True).astype(jnp.int32)
    counts = jnp.zeros((N_EXPERTS,), jnp.int32).at[expert_flat].add(1)
    starts = jnp.cumsum(counts) - counts
    tok_sorted = order // TOP_K
    dest_sorted = (order % TOP_K) * N_TOK + tok_sorted
    tok_sorted = jnp.pad(tok_sorted, (0, MOE_ROWS))
    dest_sorted = jnp.pad(dest_sorted, (0, MOE_ROWS))
    chunks = (counts + MOE_ROWS - 1) // MOE_ROWS
    chunk_end = jnp.cumsum(chunks)
    n_items = chunk_end[-1]
    w = jnp.arange(MOE_ITEMS, dtype=jnp.int32)
    w_eff = jnp.minimum(w, n_items - 1)
    e = jnp.sum(w_eff[:, None] >= chunk_end[None, :], axis=1).astype(jnp.int32)
    c = w_eff - (chunk_end - chunks)[e]
    item_start = starts[e] + c * MOE_ROWS
    item_n = jnp.where(w < n_items, jnp.clip(counts[e] - c * MOE_ROWS, 0, MOE_ROWS), 0)
    return gates, e, item_start.astype(jnp.int32), item_n.astype(jnp.int32), tok_sorted, dest_sorted


def kernel(x, meta_tokens, rel_bias_table, attn_norm, w_in, fox_forget_bias, lam_q1, lam_k1, lam_q2, lam_k2,
           diff_subln, w_branch_fox, w_branch_diff, w_out, ffn_norm, w_router, b_router, w_gate_up, b_gate_up,
           w_down, b_down, final_norm):
    x2d = x.reshape(N_TOK, D_MODEL)
    w_t = jnp.swapaxes(w_in[0], 0, 1)
    fb_pad = jnp.pad(fox_forget_bias[0], (0, LANES - FOX_HEADS)).reshape(1, LANES)

    proj, ff_real = _inproj(x2d, attn_norm, w_t, tm=1024, tn=512)
    meta_pad = jnp.pad(meta_tokens, ((0, N_META_PAD - N_META), (0, 0)))
    proj_meta, ff_meta = _inproj(meta_pad, attn_norm, w_t, tm=N_META_PAD, tn=512)

    ck, ckm = _forget_cumsum(ff_real, ff_meta, fb_pad)
    o_fox = _fox_attention(proj, proj_meta, ck, ckm)

    near, mbias = _bias_tiles(rel_bias_table)
    lam_vecs = jnp.concatenate([lam_q1, lam_k1, lam_q2, lam_k2], axis=0)
    o_diff = _diff_attention(proj, proj_meta, near, mbias, lam_vecs, diff_subln)

    merged = _gated_merge(o_fox, o_diff, w_branch_fox[0].astype(bf16), w_branch_diff[0].astype(bf16), proj)
    w_router_pad = jnp.pad(w_router[0], ((0, 0), (0, LANES - N_EXPERTS)))
    w_router_hi = w_router_pad.astype(bf16)
    w_router_lo = (w_router_pad - w_router_hi.astype(f32)).astype(bf16)
    w_router_split = jnp.concatenate([w_router_hi, w_router_lo], axis=1)
    b_router_pad = jnp.pad(b_router[0], (0, LANES - N_EXPERTS)).reshape(1, LANES)
    h1, hn2, logits = _outproj(merged, w_out[0].astype(bf16), x2d, ffn_norm, w_router_split, b_router_pad)

    gates, item_e, item_start, item_n, tok_sorted, dest_sorted = _route(logits[:, :N_EXPERTS])
    y = _moe(item_e, item_start, item_n, tok_sorted, dest_sorted, hn2,
             w_gate_up[0], b_gate_up[0].reshape(N_EXPERTS, 1, 2 * D_EXPERT),
             w_down[0], b_down[0].reshape(N_EXPERTS, 1, D_MODEL))
    out = _combine(y, h1, gates, final_norm.reshape(1, D_MODEL))
    return out.reshape(BATCH, SEQ, D_MODEL)
```

```python
import functools
import math

import jax
import jax.numpy as jnp
from jax import lax
from jax.experimental import pallas as pl
from jax.experimental.pallas import tpu as pltpu

D_MODEL = 2048
BATCH = 4
SEQ = 2048
N_TOK = BATCH * SEQ
N_META = 16
N_META_PAD = 128
HEAD_DIM = 128
FOX_HEADS = 8
DIFF_HEADS = 4
DIFF_V_DIM = 2 * HEAD_DIM
FOX_WIDTH = FOX_HEADS * HEAD_DIM
DIFF_QK_WIDTH = DIFF_HEADS * 2 * HEAD_DIM
DIFF_WIDTH = DIFF_HEADS * DIFF_V_DIM
N_BUCKETS = 32
MAX_DISTANCE = 128
N_EXPERTS = 32
TOP_K = 4
D_EXPERT = D_MODEL
SWIGLU_LIMIT = 7.0
SWIGLU_ALPHA = 1.702
RMS_EPS = 1e-5
NEG_INF = -1e30
LAMBDA_INIT = 0.8 - 0.6 * math.exp(-0.3 * 0)
ATTN_SCALE = HEAD_DIM ** -0.5
LOG2E = math.log2(math.e)
Q_PRESCALE = ATTN_SCALE * LOG2E

LANES = 128
SUBLANES = 8
VMEM_LIMIT_BYTES = 56 * 1024 * 1024
BIG_VMEM_LIMIT_BYTES = 60 * 1024 * 1024

COL_FQ = 0
COL_FK = COL_FQ + FOX_WIDTH
COL_FV = COL_FK + FOX_WIDTH
COL_DQ = COL_FV + FOX_WIDTH
COL_DK = COL_DQ + DIFF_QK_WIDTH
COL_DV = COL_DK + DIFF_QK_WIDTH
COL_GF = COL_DV + DIFF_WIDTH
COL_GD = COL_GF + D_MODEL
D_PROJ = COL_GD + D_MODEL

ATTN_TILE = 512
N_QBLK = SEQ // ATTN_TILE

MOE_ROWS = 1280
MOE_SUB = 256
MOE_HALF = MOE_SUB // 2
MOE_NSUB = MOE_ROWS // MOE_SUB
MOE_TF = 256
MOE_NF = D_EXPERT // MOE_TF
MOE_CHUNK = MOE_ROWS // (MOE_NF * MOE_NSUB)
N_ASSIGN = N_TOK * TOP_K
MOE_ITEMS = N_EXPERTS + N_ASSIGN // MOE_ROWS
Y_ROWS = N_ASSIGN + MOE_ROWS

f32 = jnp.float32
bf16 = jnp.bfloat16


def _params(sem, vmem=VMEM_LIMIT_BYTES):
    return pltpu.CompilerParams(dimension_semantics=sem, vmem_limit_bytes=vmem)


def _rms_bf16(x, g):
    return (x * lax.rsqrt(jnp.mean(x * x, axis=-1, keepdims=True) + RMS_EPS) * g).astype(bf16)


def _inproj_kernel(x_hbm, xm_ref, g_ref, wt_ref, wfft_ref, o_ref, om_ref, ff_ref, ffm_ref,
                   xn_ref, xnm_ref, xt_ref, wb_ref, wffb_ref, sem, *, tm, tn):
    j = pl.program_id(0)
    i = pl.program_id(1)
    rows = pl.ds(pl.multiple_of(i * tm, tm), tm)
    is_q = jnp.logical_or(j < COL_FK // tn, jnp.logical_and(j >= COL_DQ // tn, j < COL_DK // tn))
    qs = jnp.where(is_q, Q_PRESCALE, 1.0)

    @pl.when(jnp.logical_and(j == 0, i == 0))
    def _():
        wffb_ref[...] = wfft_ref[...].T.astype(bf16)
        xnm_ref[...] = _rms_bf16(xm_ref[...], g_ref[...])
        ffm_ref[...] = jnp.dot(xnm_ref[...], wffb_ref[...], preferred_element_type=f32)

    @pl.when(i == 0)
    def _():
        wb_ref[...] = wt_ref[...].T.astype(bf16)
        ym = jnp.dot(xnm_ref[...], wb_ref[...], preferred_element_type=f32)
        om_ref[...] = (ym * qs).astype(bf16)

    @pl.when(j == 0)
    def _():
        cp = pltpu.make_async_copy(x_hbm.at[rows, :], xt_ref, sem)
        cp.start()
        cp.wait()
        xn = _rms_bf16(xt_ref[...], g_ref[...])
        xn_ref[rows, :] = xn
        ff_ref[...] = jnp.dot(xn, wffb_ref[...], preferred_element_type=f32)

    y = jnp.dot(xn_ref[rows, :], wb_ref[...], preferred_element_type=f32)
    o_ref[...] = (y * qs).astype(bf16)


def _inproj(x2d, meta_pad, gain, w_t, tm, tn):
    n = x2d.shape[0]
    n_i = n // tm
    w_row = lambda j: SUBLANES * (j * (tn // SUBLANES) + jnp.where(j * tn >= COL_DQ, FOX_HEADS // SUBLANES, 0))
    first_pass_row = lambda j, i: jnp.where(j == 0, i, n_i - 1)
    return pl.pallas_call(
        functools.partial(_inproj_kernel, tm=tm, tn=tn),
        grid=(D_PROJ // tn, n_i),
        in_specs=[
            pl.BlockSpec(memory_space=pl.ANY),
            pl.BlockSpec((N_META_PAD, D_MODEL), lambda j, i: (0, 0)),
            pl.BlockSpec((1, D_MODEL), lambda j, i: (0, 0)),
            pl.BlockSpec((pl.Element(tn), pl.Element(D_MODEL)), lambda j, i: (w_row(j), 0)),
            pl.BlockSpec((pl.Element(LANES), pl.Element(D_MODEL)), lambda j, i: (COL_DQ, 0)),
        ],
        out_specs=[
            pl.BlockSpec((tm, tn), lambda j, i: (i, j)),
            pl.BlockSpec((N_META_PAD, tn), lambda j, i: (0, j)),
            pl.BlockSpec((tm, LANES), lambda j, i: (first_pass_row(j, i), 0)),
            pl.BlockSpec((N_META_PAD, LANES), lambda j, i: (0, 0)),
        ],
        out_shape=[
            jax.ShapeDtypeStruct((n, D_PROJ), bf16),
            jax.ShapeDtypeStruct((N_META_PAD, D_PROJ), bf16),
            jax.ShapeDtypeStruct((n, LANES), f32),
            jax.ShapeDtypeStruct((N_META_PAD, LANES), f32),
        ],
        scratch_shapes=[
            pltpu.VMEM((n, D_MODEL), bf16),
            pltpu.VMEM((N_META_PAD, D_MODEL), bf16),
            pltpu.VMEM((tm, D_MODEL), f32),
            pltpu.VMEM((D_MODEL, tn), bf16),
            pltpu.VMEM((D_MODEL, LANES), bf16),
            pltpu.SemaphoreType.DMA,
        ],
        compiler_params=_params(("arbitrary", "arbitrary"), BIG_VMEM_LIMIT_BYTES),
        name="inproj",
    )(x2d, meta_pad, gain, w_t, w_t)


def _log_sigmoid(x):
    return jnp.minimum(x, 0.0) - jnp.log(1.0 + jnp.exp(-jnp.abs(x)))


def _lane_cumsum(x):
    n = x.shape[-1]
    lane = lax.broadcasted_iota(jnp.int32, x.shape, x.ndim - 1)
    s = 1
    while s < n:
        x = x + jnp.where(lane >= s, pltpu.roll(x, s, x.ndim - 1), 0.0)
        s *= 2
    return x


def _cum_kernel(ff_ref, ffm_ref, fb_ref, ck_ref, ckm_ref):
    fb = fb_ref[...]
    row = lax.broadcasted_iota(jnp.int32, (LANES, LANES), 0)
    lfm = jnp.where(row < N_META, _log_sigmoid(ffm_ref[...] + fb), 0.0)
    cm = _lane_cumsum(lfm.T)
    m_total = cm[:, N_META - 1:N_META]
    lf = _log_sigmoid(ff_ref[...] + fb)
    cr = _lane_cumsum(lf.T) + m_total
    ck_ref[...] = cr[:SUBLANES] * LOG2E
    ckm_ref[...] = cm[:SUBLANES] * LOG2E


def _forget_cumsum(ff_real, ff_meta_pad, fb_pad):
    return pl.pallas_call(
        _cum_kernel,
        grid=(BATCH,),
        in_specs=[
            pl.BlockSpec((SEQ, LANES), lambda b: (b, 0)),
            pl.BlockSpec((LANES, LANES), lambda b: (0, 0)),
            pl.BlockSpec((1, LANES), lambda b: (0, 0)),
        ],
        out_specs=[
            pl.BlockSpec((None, SUBLANES, SEQ), lambda b: (b, 0, 0)),
            pl.BlockSpec((None, SUBLANES, LANES), lambda b: (b, 0, 0)),
        ],
        out_shape=[
            jax.ShapeDtypeStruct((BATCH, FOX_HEADS, SEQ), f32),
            jax.ShapeDtypeStruct((BATCH, FOX_HEADS, LANES), f32),
        ],
        compiler_params=_params(("parallel",)),
        name="forget_cumsum",
    )(ff_real, ff_meta_pad, fb_pad)


def _qk(q, k):
    return lax.dot_general(q, k, (((1,), (1,)), ((), ())), preferred_element_type=f32)


def _row_to_col(row):
    t = row.shape[-1]
    r = lax.broadcasted_iota(jnp.int32, (t, t), 0)
    c = lax.broadcasted_iota(jnp.int32, (t, t), 1)
    return jnp.sum(jnp.where(r == c, row, 0.0), axis=1, keepdims=True)


def _online_update(carry, s, v):
    m, l, acc = carry
    m_new = jnp.maximum(m, jnp.max(s, axis=-1, keepdims=True))
    alpha = jnp.exp2(m - m_new)
    p = jnp.exp2(s - m_new)
    l = alpha * l + jnp.sum(p, axis=-1, keepdims=True)
    acc = alpha * acc + jnp.dot(p.astype(bf16), v, preferred_element_type=f32)
    return m_new, l, acc


def _first_update(s, v):
    m = jnp.max(s, axis=-1, keepdims=True)
    p = jnp.exp2(s - m)
    l = jnp.sum(p, axis=-1, keepdims=True)
    acc = jnp.dot(p.astype(bf16), v, preferred_element_type=f32)
    return m, l, acc


FOX_GROUP = 2


def _fox_update(carry, u, cq, v):
    m_blk = jnp.max(u, axis=-1, keepdims=True) + cq
    if carry is None:
        m_new = m_blk
    else:
        m, l, acc = carry
        m_new = jnp.maximum(m, m_blk)
    p = jnp.exp2(u - (m_new - cq))
    ps = jnp.sum(p, axis=-1, keepdims=True)
    pv = jnp.dot(p.astype(bf16), v, preferred_element_type=f32)
    if carry is None:
        return m_new, ps, pv
    alpha = jnp.exp2(m - m_new)
    return m_new, alpha * l + ps, alpha * acc + pv


def _fox_kernel(q_ref, k_ref, v_ref, km_ref, vm_ref, ck_ref, ckm_ref, o_ref):
    hg = pl.program_id(1)
    i = pl.program_id(2)
    t = ATTN_TILE
    q0 = pl.multiple_of(i * t, t)
    heads = range(FOX_GROUP)
    col = lambda g: slice(g * HEAD_DIM, (g + 1) * HEAD_DIM)
    ck_row = lambda g, k0: ck_ref[pl.ds(hg * FOX_GROUP + g, 1), pl.ds(k0, t)]
    qs = [q_ref[:, col(g)] for g in heads]
    cqs = [_row_to_col(ck_row(g, q0)) for g in heads]

    meta_ok = lax.broadcasted_iota(jnp.int32, (t, N_META_PAD), 1) < N_META
    us = [_qk(qs[g], km_ref[:, col(g)]) - ckm_ref[pl.ds(hg * FOX_GROUP + g, 1), :] for g in heads]
    carry = tuple(_fox_update(None, jnp.where(meta_ok, us[g], NEG_INF), cqs[g], vm_ref[:, col(g)]) for g in heads)

    def body(j, carry):
        k0 = pl.multiple_of(j * t, t)
        us = [_qk(qs[g], k_ref[pl.ds(k0, t), col(g)]) - ck_row(g, k0) for g in heads]
        return tuple(_fox_update(carry[g], us[g], cqs[g], v_ref[pl.ds(k0, t), col(g)]) for g in heads)

    carry = lax.fori_loop(0, i, body, carry)

    r = lax.broadcasted_iota(jnp.int32, (t, t), 0)
    c = lax.broadcasted_iota(jnp.int32, (t, t), 1)
    us = [_qk(qs[g], k_ref[pl.ds(q0, t), col(g)]) - ck_row(g, q0) for g in heads]
    for g in heads:
        _, l, acc = _fox_update(carry[g], jnp.where(c <= r, us[g], NEG_INF), cqs[g], v_ref[pl.ds(q0, t), col(g)])
        o_ref[:, col(g)] = (acc / l).astype(bf16)


def _fox_attention(proj, proj_meta, ck, ckm):
    t = ATTN_TILE
    w = FOX_GROUP * HEAD_DIM
    cb = lambda col: col // w
    return pl.pallas_call(
        _fox_kernel,
        grid=(BATCH, FOX_HEADS // FOX_GROUP, N_QBLK),
        in_specs=[
            pl.BlockSpec((t, w), lambda b, h, i: (b * N_QBLK + i, cb(COL_FQ) + h)),
            pl.BlockSpec((SEQ, w), lambda b, h, i: (b, cb(COL_FK) + h)),
            pl.BlockSpec((SEQ, w), lambda b, h, i: (b, cb(COL_FV) + h)),
            pl.BlockSpec((N_META_PAD, w), lambda b, h, i: (0, cb(COL_FK) + h)),
            pl.BlockSpec((N_META_PAD, w), lambda b, h, i: (0, cb(COL_FV) + h)),
            pl.BlockSpec((None, FOX_HEADS, SEQ), lambda b, h, i: (b, 0, 0)),
            pl.BlockSpec((None, FOX_HEADS, LANES), lambda b, h, i: (b, 0, 0)),
        ],
        out_specs=pl.BlockSpec((t, w), lambda b, h, i: (b * N_QBLK + i, h)),
        out_shape=jax.ShapeDtypeStruct((N_TOK, FOX_WIDTH), bf16),
        compiler_params=_params(("parallel", "parallel", "arbitrary")),
        name="fox_attention",
    )(proj, proj, proj, proj_meta, proj_meta, ck, ckm)


def _t5_bias(dist, table_ref, h):
    n = jnp.maximum(dist, 0)
    max_exact = N_BUCKETS // 2
    log_part = jnp.log(jnp.maximum(n, 1).astype(f32) / max_exact) / math.log(MAX_DISTANCE / max_exact)
    v = log_part * (N_BUCKETS - max_exact)
    far = table_ref[N_BUCKETS - 1, h]
    val = lambda b: (table_ref[b, h] - far) * LOG2E
    large = jnp.zeros(dist.shape, f32)
    for b in range(N_BUCKETS - 2, max_exact - 1, -1):
        large = jnp.where(v < b + 1 - max_exact, val(b), large)
    out = large
    for b in range(max_exact):
        out = jnp.where(n == b, val(b), out)
    return out


def _bias_kernel(table_ref, near_ref, meta_ref):
    h = pl.program_id(0)
    t = ATTN_TILE
    r = lax.broadcasted_iota(jnp.int32, (t, 2 * t), 0)
    c = lax.broadcasted_iota(jnp.int32, (t, 2 * t), 1)
    near_ref[...] = _t5_bias(r + t - c, table_ref, h)
    r = lax.broadcasted_iota(jnp.int32, (t, LANES), 0)
    c = lax.broadcasted_iota(jnp.int32, (t, LANES), 1)
    meta_ref[...] = _t5_bias(N_META + r - c, table_ref, h)


def _bias_tiles(table):
    t = ATTN_TILE
    return pl.pallas_call(
        _bias_kernel,
        grid=(DIFF_HEADS,),
        in_specs=[pl.BlockSpec(memory_space=pltpu.SMEM)],
        out_specs=[
            pl.BlockSpec((None, t, 2 * t), lambda h: (h, 0, 0)),
            pl.BlockSpec((None, t, LANES), lambda h: (h, 0, 0)),
        ],
        out_shape=[
            jax.ShapeDtypeStruct((DIFF_HEADS, t, 2 * t), f32),
            jax.ShapeDtypeStruct((DIFF_HEADS, t, LANES), f32),
        ],
        compiler_params=_params(("arbitrary",)),
        name="t5_bias_tiles",
    )(table)


def _diff_kernel(q1_ref, q2_ref, k1_ref, k2_ref, v_ref, k1m_ref, k2m_ref, vm_ref,
                 near_ref, mbias_ref, lam_ref, subln_ref, o_ref):
    i = pl.program_id(2)
    t = ATTN_TILE
    q1 = q1_ref[...]
    q2 = q2_ref[...]
    q0 = pl.multiple_of(i * t, t)
    lam = (jnp.exp(jnp.sum(lam_ref[0:1, :] * lam_ref[1:2, :], axis=-1, keepdims=True))
           - jnp.exp(jnp.sum(lam_ref[2:3, :] * lam_ref[3:4, :], axis=-1, keepdims=True))
           + LAMBDA_INIT)

    mb = jnp.where(i == 0, mbias_ref[...], 0.0)
    meta_ok = lax.broadcasted_iota(jnp.int32, (t, N_META_PAD), 1) < N_META
    vm = vm_ref[...]
    c1 = _first_update(jnp.where(meta_ok, _qk(q1, k1m_ref[...]) + mb, NEG_INF), vm)
    c2 = _first_update(jnp.where(meta_ok, _qk(q2, k2m_ref[...]) + mb, NEG_INF), vm)

    def far_body(j, carry):
        c1, c2 = carry
        k0 = pl.multiple_of(j * t, t)
        v = v_ref[pl.ds(k0, t), :]
        c1 = _online_update(c1, _qk(q1, k1_ref[pl.ds(k0, t), :]), v)
        c2 = _online_update(c2, _qk(q2, k2_ref[pl.ds(k0, t), :]), v)
        return c1, c2

    c1, c2 = lax.fori_loop(0, jnp.maximum(i - 1, 0), far_body, (c1, c2))

    def prev_block(carry):
        c1, c2 = carry
        k0 = pl.multiple_of((i - 1) * t, t)
        v = v_ref[pl.ds(k0, t), :]
        bias = near_ref[:, :t]
        c1 = _online_update(c1, _qk(q1, k1_ref[pl.ds(k0, t), :]) + bias, v)
        c2 = _online_update(c2, _qk(q2, k2_ref[pl.ds(k0, t), :]) + bias, v)
        return c1, c2

    c1, c2 = lax.cond(i > 0, prev_block, lambda carry: carry, (c1, c2))

    r = lax.broadcasted_iota(jnp.int32, (t, t), 0)
    c = lax.broadcasted_iota(jnp.int32, (t, t), 1)
    mask = c <= r
    bias = near_ref[:, t:]
    v = v_ref[pl.ds(q0, t), :]
    s1 = jnp.where(mask, _qk(q1, k1_ref[pl.ds(q0, t), :]) + bias, NEG_INF)
    s2 = jnp.where(mask, _qk(q2, k2_ref[pl.ds(q0, t), :]) + bias, NEG_INF)
    _, l1, a1 = _online_update(c1, s1, v)
    _, l2, a2 = _online_update(c2, s2, v)

    o = a1 / l1 - lam * (a2 / l2)
    y = o * lax.rsqrt(jnp.mean(o * o, axis=-1, keepdims=True) + RMS_EPS) * subln_ref[...]
    o_ref[...] = (y * (1.0 - LAMBDA_INIT)).astype(bf16)


def _diff_attention(proj, proj_meta, near, mbias, lam_vecs, subln):
    t = ATTN_TILE
    cb = lambda col: col // HEAD_DIM
    vb = lambda col: col // DIFF_V_DIM
    row = lambda b, h, i: b * N_QBLK + i
    return pl.pallas_call(
        _diff_kernel,
        grid=(BATCH, DIFF_HEADS, N_QBLK),
        in_specs=[
            pl.BlockSpec((t, HEAD_DIM), lambda b, h, i: (row(b, h, i), cb(COL_DQ) + 2 * h)),
            pl.BlockSpec((t, HEAD_DIM), lambda b, h, i: (row(b, h, i), cb(COL_DQ) + 2 * h + 1)),
            pl.BlockSpec((SEQ, HEAD_DIM), lambda b, h, i: (b, cb(COL_DK) + 2 * h)),
            pl.BlockSpec((SEQ, HEAD_DIM), lambda b, h, i: (b, cb(COL_DK) + 2 * h + 1)),
            pl.BlockSpec((SEQ, DIFF_V_DIM), lambda b, h, i: (b, vb(COL_DV) + h)),
            pl.BlockSpec((N_META_PAD, HEAD_DIM), lambda b, h, i: (0, cb(COL_DK) + 2 * h)),
            pl.BlockSpec((N_META_PAD, HEAD_DIM), lambda b, h, i: (0, cb(COL_DK) + 2 * h + 1)),
            pl.BlockSpec((N_META_PAD, DIFF_V_DIM), lambda b, h, i: (0, vb(COL_DV) + h)),
            pl.BlockSpec((None, t, 2 * t), lambda b, h, i: (h, 0, 0)),
            pl.BlockSpec((None, t, LANES), lambda b, h, i: (h, 0, 0)),
            pl.BlockSpec((4, HEAD_DIM), lambda b, h, i: (0, 0)),
            pl.BlockSpec((1, DIFF_V_DIM), lambda b, h, i: (0, 0)),
        ],
        out_specs=pl.BlockSpec((t, DIFF_V_DIM), lambda b, h, i: (row(b, h, i), h)),
        out_shape=jax.ShapeDtypeStruct((N_TOK, DIFF_WIDTH), bf16),
        compiler_params=_params(("parallel", "parallel", "arbitrary")),
        name="diff_attention",
    )(proj, proj, proj, proj, proj, proj_meta, proj_meta, proj_meta, near, mbias, lam_vecs, subln)


def _merge_kernel(of_ref, od_ref, wf_ref, wd_ref, gf_ref, gd_ref, o_ref):
    yf = jnp.dot(of_ref[...], wf_ref[...], preferred_element_type=f32)
    yd = jnp.dot(od_ref[...], wd_ref[...], preferred_element_type=f32)
    gf = jax.nn.sigmoid(gf_ref[...].astype(f32))
    gd = jax.nn.sigmoid(gd_ref[...].astype(f32))
    o_ref[...] = (gf * yf + gd * yd).astype(bf16)


def _gated_merge(o_fox, o_diff, w_bf, w_bd, proj, tm=1024, tn=512):
    return pl.pallas_call(
        _merge_kernel,
        grid=(D_MODEL // tn, N_TOK // tm),
        in_specs=[
            pl.BlockSpec((tm, FOX_WIDTH), lambda j, i: (i, 0)),
            pl.BlockSpec((tm, DIFF_WIDTH), lambda j, i: (i, 0)),
            pl.BlockSpec((FOX_WIDTH, tn), lambda j, i: (0, j)),
            pl.BlockSpec((DIFF_WIDTH, tn), lambda j, i: (0, j)),
            pl.BlockSpec((tm, tn), lambda j, i: (i, COL_GF // tn + j)),
            pl.BlockSpec((tm, tn), lambda j, i: (i, COL_GD // tn + j)),
        ],
        out_specs=pl.BlockSpec((tm, tn), lambda j, i: (i, j)),
        out_shape=jax.ShapeDtypeStruct((N_TOK, D_MODEL), bf16),
        compiler_params=_params(("parallel", "arbitrary")),
        name="gated_merge",
    )(o_fox, o_diff, w_bf, w_bd, proj, proj)


def _outproj_kernel(m_ref, w_ref, x_ref, g_ref, wr_ref, br_ref, h_ref, hn_ref, lg_ref):
    h1 = x_ref[...] + jnp.dot(m_ref[...], w_ref[...], preferred_element_type=f32)
    h_ref[...] = h1
    hn = h1 * lax.rsqrt(jnp.mean(h1 * h1, axis=-1, keepdims=True) + RMS_EPS) * g_ref[...]
    hn_ref[...] = hn
    hn_hi = hn.astype(bf16)
    hn_lo = (hn - hn_hi.astype(f32)).astype(bf16)
    a = jnp.dot(hn_hi, wr_ref[...], preferred_element_type=f32)
    b = jnp.dot(hn_lo, wr_ref[:, :LANES], preferred_element_type=f32)
    lg_ref[...] = a[:, :LANES] + a[:, LANES:] + b + br_ref[...]


def _outproj(merged, w_out, x2d, gain, w_router_split, b_router_pad, tm=512):
    return pl.pallas_call(
        _outproj_kernel,
        grid=(N_TOK // tm,),
        in_specs=[
            pl.BlockSpec((tm, D_MODEL), lambda i: (i, 0)),
            pl.BlockSpec((D_MODEL, D_MODEL), lambda i: (0, 0)),
            pl.BlockSpec((tm, D_MODEL), lambda i: (i, 0)),
            pl.BlockSpec((1, D_MODEL), lambda i: (0, 0)),
            pl.BlockSpec((D_MODEL, 2 * LANES), lambda i: (0, 0)),
            pl.BlockSpec((1, LANES), lambda i: (0, 0)),
        ],
        out_specs=[
            pl.BlockSpec((tm, D_MODEL), lambda i: (i, 0)),
            pl.BlockSpec((tm, D_MODEL), lambda i: (i, 0)),
            pl.BlockSpec((tm, LANES), lambda i: (i, 0)),
        ],
        out_shape=[
            jax.ShapeDtypeStruct((N_TOK, D_MODEL), f32),
            jax.ShapeDtypeStruct((N_TOK, D_MODEL), f32),
            jax.ShapeDtypeStruct((N_TOK, LANES), f32),
        ],
        compiler_params=_params(("parallel",)),
        name="outproj_router",
    )(merged, w_out, x2d, gain, w_router_split, b_router_pad)


def _moe_kernel(item_e_ref, item_start_ref, item_n_ref, tok_ref, dest_ref,
                hn_hbm, wg_ref, wl_ref, bg_ref, bl_ref, wd_ref, bd_ref, y_hbm,
                acc_ref, xg_ref, xb_ref, wgb_ref, wlb_ref, wdb_ref, gsem, ssem):
    w = pl.program_id(0)
    t = pl.program_id(1)
    last_w = MOE_ITEMS - 1
    n = item_n_ref[w]
    start = item_start_ref[w]
    slot = lax.rem(w, 2)
    w_next = jnp.minimum(w + 1, last_w)
    start_next = item_start_ref[w_next]
    w_prev = jnp.maximum(w - 1, 0)
    start_prev = item_start_ref[w_prev]
    n_prev = jnp.where(w > 0, item_n_ref[w_prev], 0)
    is_last_live = jnp.logical_or(w == last_w, item_n_ref[w_next] == 0)

    def gather_row(row, base):
        tk = tok_ref[base + row]
        pltpu.make_async_copy(hn_hbm.at[pl.ds(tk, 1), :], xg_ref.at[pl.ds(row, 1), :], gsem).start()

    def gather_wait():
        pltpu.make_async_copy(hn_hbm.at[pl.ds(0, MOE_ROWS), :], xg_ref, gsem).wait()

    def scatter_row(row, base, cnt, sl, priority=0):
        d = jnp.where(row < cnt, dest_ref[base + row], N_ASSIGN + row)
        pltpu.make_async_copy(acc_ref.at[sl, pl.ds(row, 1), :], y_hbm.at[pl.ds(d, 1), :],
                              ssem).start(priority=priority)

    def scatter_wait(sl):
        pltpu.make_async_copy(acc_ref.at[sl], y_hbm.at[pl.ds(0, MOE_ROWS), :], ssem).wait()

    @pl.when(jnp.logical_and(w == 0, t == 0))
    def _():
        acc_ref[1] = jnp.zeros((MOE_ROWS, D_MODEL), f32)

        def issue(r, c):
            gather_row(r, start)
            return c
        lax.fori_loop(0, MOE_ROWS, issue, 0, unroll=8)

    @pl.when(jnp.logical_and(t == 0, n > 0))
    def _():
        gather_wait()
        xb_ref[...] = xg_ref[...].astype(bf16)
        acc_ref[slot] = jnp.zeros((MOE_ROWS, D_MODEL), f32)

    def dma_chunk(r):
        row0 = t * (MOE_NSUB * MOE_CHUNK) + r * MOE_CHUNK
        for k in range(MOE_CHUNK):
            gather_row(row0 + k, start_next)
        for k in range(MOE_CHUNK):
            scatter_row(row0 + k, start_prev, n_prev, 1 - slot, priority=k % 2)

    def ffn_up(r0, rows):
        xs = xb_ref[r0:r0 + rows, :]
        hg = jnp.dot(xs, wgb_ref[...], preferred_element_type=f32) + bg_ref[...]
        hl = jnp.dot(xs, wlb_ref[...], preferred_element_type=f32) + bl_ref[...]
        return hg, hl

    def ffn_down(r0, rows, hg, hl):
        hg = jnp.minimum(hg, SWIGLU_LIMIT)
        hl = jnp.clip(hl, -SWIGLU_LIMIT, SWIGLU_LIMIT)
        act = hg * jax.nn.sigmoid(SWIGLU_ALPHA * hg) * (hl + 1.0)
        acc_ref[slot, r0:r0 + rows, :] += jnp.dot(act.astype(bf16), wdb_ref[...], preferred_element_type=f32)

    def ffn_rows(r0, rows):
        ffn_down(r0, rows, *ffn_up(r0, rows))

    @pl.when(n > 0)
    def _():
        wgb_ref[...] = wg_ref[...].astype(bf16)
        wlb_ref[...] = wl_ref[...].astype(bf16)
        wdb_ref[...] = wd_ref[...].astype(bf16)
        nb_full = n // MOE_SUB
        rem = n - nb_full * MOE_SUB
        ext = jnp.logical_and(jnp.logical_and(rem > 0, rem <= MOE_HALF), nb_full >= 1)
        own_block = jnp.logical_or(rem > MOE_HALF, jnp.logical_and(rem > 0, nb_full == 0))
        nb = nb_full + jnp.where(own_block, 1, 0)
        tiny = jnp.logical_and(nb_full == 0, rem <= MOE_HALF)

        def single(r, rows, chunks=1):
            def body():
                for c in range(chunks):
                    dma_chunk(r + c)
                ffn_rows(r * MOE_SUB, rows)
            return body

        def pair(r, rows_b):
            def body():
                dma_chunk(r)
                dma_chunk(r + 1)
                ra, rb = r * MOE_SUB, (r + 1) * MOE_SUB
                ha = ffn_up(ra, MOE_SUB)
                hb = ffn_up(rb, rows_b)
                ffn_down(ra, MOE_SUB, *ha)
                ffn_down(rb, rows_b, *hb)
            return body

        def idle(*rs):
            def body():
                for r in rs:
                    dma_chunk(r)
            return body

        def last_single(r):
            return lambda: lax.cond(
                ext, single(r, MOE_SUB + MOE_HALF, 2),
                lambda: lax.cond(tiny, single(r, MOE_HALF, 2), single(r, MOE_SUB, 2)))

        for r in range(0, MOE_NSUB - 1, 2):
            is_last_pair = jnp.logical_and(ext, nb == r + 2)
            lax.cond(
                nb >= r + 2,
                lambda r=r, is_last_pair=is_last_pair: lax.cond(
                    is_last_pair, pair(r, MOE_SUB + MOE_HALF), pair(r, MOE_SUB)),
                lambda r=r: lax.cond(nb == r + 1, last_single(r), idle(r, r + 1)))
        for r in range(MOE_NSUB - MOE_NSUB % 2, MOE_NSUB):
            lax.cond(nb == r + 1, single(r, MOE_SUB), idle(r))

    @pl.when(jnp.logical_and(t == MOE_NF - 1, n > 0))
    def _():
        scatter_wait(1 - slot)
        acc_ref[slot] = acc_ref[slot] + bd_ref[...]

        @pl.when(is_last_live)
        def _():
            def issue(r, c):
                scatter_row(r, start, n, slot)
                return c
            lax.fori_loop(0, MOE_ROWS, issue, 0, unroll=8)
            scatter_wait(slot)
            gather_wait()


def _moe(item_e, item_start, item_n, tok_sorted, dest_sorted, hn2, w_gate_up, b_gate_up, w_down, b_down):
    tf = MOE_TF
    live_t = lambda w, t, n: jnp.where(n[w] > 0, t, MOE_NF - 1)
    grid_spec = pltpu.PrefetchScalarGridSpec(
        num_scalar_prefetch=5,
        grid=(MOE_ITEMS, MOE_NF),
        in_specs=[
            pl.BlockSpec(memory_space=pl.ANY),
            pl.BlockSpec((None, D_MODEL, tf), lambda w, t, e, s, n, tk, ds: (e[w], 0, live_t(w, t, n))),
            pl.BlockSpec((None, D_MODEL, tf), lambda w, t, e, s, n, tk, ds: (e[w], 0, MOE_NF + live_t(w, t, n))),
            pl.BlockSpec((None, 1, tf), lambda w, t, e, s, n, tk, ds: (e[w], 0, live_t(w, t, n))),
            pl.BlockSpec((None, 1, tf), lambda w, t, e, s, n, tk, ds: (e[w], 0, MOE_NF + live_t(w, t, n))),
            pl.BlockSpec((None, tf, D_MODEL), lambda w, t, e, s, n, tk, ds: (e[w], live_t(w, t, n), 0)),
            pl.BlockSpec((None, 1, D_MODEL), lambda w, t, e, s, n, tk, ds: (e[w], 0, 0)),
        ],
        out_specs=pl.BlockSpec(memory_space=pl.ANY),
        scratch_shapes=[
            pltpu.VMEM((2, MOE_ROWS, D_MODEL), f32),
            pltpu.VMEM((MOE_ROWS, D_MODEL), f32),
            pltpu.VMEM((MOE_ROWS, D_MODEL), bf16),
            pltpu.VMEM((D_MODEL, tf), bf16),
            pltpu.VMEM((D_MODEL, tf), bf16),
            pltpu.VMEM((tf, D_MODEL), bf16),
            pltpu.SemaphoreType.DMA,
            pltpu.SemaphoreType.DMA,
        ],
    )
    return pl.pallas_call(
        _moe_kernel,
        grid_spec=grid_spec,
        out_shape=jax.ShapeDtypeStruct((Y_ROWS, D_MODEL), f32),
        compiler_params=_params(("arbitrary", "arbitrary"), BIG_VMEM_LIMIT_BYTES),
        name="moe_experts",
    )(item_e, item_start, item_n, tok_sorted, dest_sorted,
      hn2, w_gate_up, w_gate_up, b_gate_up, b_gate_up, w_down, b_down)


def _combine_kernel(y0_ref, y1_ref, y2_ref, y3_ref, h_ref, gate_ref, g_ref, o_ref):
    h = h_ref[...]
    gates = gate_ref[...]
    for k, y_ref in enumerate((y0_ref, y1_ref, y2_ref, y3_ref)):
        h = h + y_ref[...] * gates[:, k:k + 1]
    o_ref[...] = h * lax.rsqrt(jnp.mean(h * h, axis=-1, keepdims=True) + RMS_EPS) * g_ref[...]


def _combine(y, h1, gates, gain, tm=256):
    slot_spec = lambda k: pl.BlockSpec((tm, D_MODEL), lambda i, k=k: (k * (N_TOK // tm) + i, 0))
    return pl.pallas_call(
        _combine_kernel,
        grid=(N_TOK // tm,),
        in_specs=[slot_spec(k) for k in range(TOP_K)] + [
            pl.BlockSpec((tm, D_MODEL), lambda i: (i, 0)),
            pl.BlockSpec((tm, TOP_K), lambda i: (i, 0)),
            pl.BlockSpec((1, D_MODEL), lambda i: (0, 0)),
        ],
        out_specs=pl.BlockSpec((tm, D_MODEL), lambda i: (i, 0)),
        out_shape=jax.ShapeDtypeStruct((N_TOK, D_MODEL), f32),
        compiler_params=_params(("parallel",)),
        name="combine_norm",
    )(y, y, y, y, h1, gates, gain)


def _route(logits):
    top_vals, top_idx = lax.top_k(logits, TOP_K)
    gates = jax.nn.softmax(top_vals, axis=-1)
    expert_flat = top_idx.reshape(-1).astype(jnp.int32)
    order = jnp.argsort(expert_flat, stable=True).astype(jnp.int32)
    counts = jnp.zeros((N_EXPERTS,), jnp.int32).at[expert_flat].add(1)
    starts = jnp.cumsum(counts) - counts
    tok_sorted = order // TOP_K
    dest_sorted = (order % TOP_K) * N_TOK + tok_sorted
    tok_sorted = jnp.pad(tok_sorted, (0, MOE_ROWS))
    dest_sorted = jnp.pad(dest_sorted, (0, MOE_ROWS))
    chunks = (counts + MOE_ROWS - 1) // MOE_ROWS
    chunk_end = jnp.cumsum(chunks)
    n_items = chunk_end[-1]
    w = jnp.arange(MOE_ITEMS, dtype=jnp.int32)
    w_eff = jnp.minimum(w, n_items - 1)
    e = jnp.sum(w_eff[:, None] >= chunk_end[None, :], axis=1).astype(jnp.int32)
    c = w_eff - (chunk_end - chunks)[e]
    item_start = starts[e] + c * MOE_ROWS
    item_n = jnp.where(w < n_items, jnp.clip(counts[e] - c * MOE_ROWS, 0, MOE_ROWS), 0)
    return gates, e, item_start.astype(jnp.int32), item_n.astype(jnp.int32), tok_sorted, dest_sorted


def kernel(x, meta_tokens, rel_bias_table, attn_norm, w_in, fox_forget_bias, lam_q1, lam_k1, lam_q2, lam_k2,
           diff_subln, w_branch_fox, w_branch_diff, w_out, ffn_norm, w_router, b_router, w_gate_up, b_gate_up,
           w_down, b_down, final_norm):
    x2d = x.reshape(N_TOK, D_MODEL)
    w_t = jnp.swapaxes(w_in[0], 0, 1)
    fb_pad = jnp.pad(fox_forget_bias[0], (0, LANES - FOX_HEADS)).reshape(1, LANES)

    meta_pad = jnp.pad(meta_tokens, ((0, N_META_PAD - N_META), (0, 0)))
    proj, proj_meta, ff_real, ff_meta = _inproj(x2d, meta_pad, attn_norm, w_t, tm=1024, tn=512)

    ck, ckm = _forget_cumsum(ff_real, ff_meta, fb_pad)
    o_fox = _fox_attention(proj, proj_meta, ck, ckm)

    near, mbias = _bias_tiles(rel_bias_table)
    lam_vecs = jnp.concatenate([lam_q1, lam_k1, lam_q2, lam_k2], axis=0)
    o_diff = _diff_attention(proj, proj_meta, near, mbias, lam_vecs, diff_subln)

    merged = _gated_merge(o_fox, o_diff, w_branch_fox[0].astype(bf16), w_branch_diff[0].astype(bf16), proj)
    w_router_pad = jnp.pad(w_router[0], ((0, 0), (0, LANES - N_EXPERTS)))
    w_router_hi = w_router_pad.astype(bf16)
    w_router_lo = (w_router_pad - w_router_hi.astype(f32)).astype(bf16)
    w_router_split = jnp.concatenate([w_router_hi, w_router_lo], axis=1)
    b_router_pad = jnp.pad(b_router[0], (0, LANES - N_EXPERTS)).reshape(1, LANES)
    h1, hn2, logits = _outproj(merged, w_out[0].astype(bf16), x2d, ffn_norm, w_router_split, b_router_pad)

    gates, item_e, item_start, item_n, tok_sorted, dest_sorted = _route(logits[:, :N_EXPERTS])
    y = _moe(item_e, item_start, item_n, tok_sorted, dest_sorted, hn2,
             w_gate_up[0], b_gate_up[0].reshape(N_EXPERTS, 1, 2 * D_EXPERT),
             w_down[0], b_down[0].reshape(N_EXPERTS, 1, D_MODEL))
    out = _combine(y, h1, gates, final_norm.reshape(1, D_MODEL))
    return out.reshape(BATCH, SEQ, D_MODEL)
```

```python
import functools
import math

import jax
import jax.numpy as jnp
from jax import lax
from jax.experimental import pallas as pl
from jax.experimental.pallas import tpu as pltpu

D_MODEL = 2048
BATCH = 4
SEQ = 2048
N_TOK = BATCH * SEQ
N_META = 16
N_META_PAD = 128
HEAD_DIM = 128
FOX_HEADS = 8
DIFF_HEADS = 4
DIFF_V_DIM = 2 * HEAD_DIM
FOX_WIDTH = FOX_HEADS * HEAD_DIM
DIFF_QK_WIDTH = DIFF_HEADS * 2 * HEAD_DIM
DIFF_WIDTH = DIFF_HEADS * DIFF_V_DIM
N_BUCKETS = 32
MAX_DISTANCE = 128
N_EXPERTS = 32
TOP_K = 4
D_EXPERT = D_MODEL
SWIGLU_LIMIT = 7.0
SWIGLU_ALPHA = 1.702
RMS_EPS = 1e-5
NEG_INF = -1e30
LAMBDA_INIT = 0.8 - 0.6 * math.exp(-0.3 * 0)
ATTN_SCALE = HEAD_DIM ** -0.5
LOG2E = math.log2(math.e)
Q_PRESCALE = ATTN_SCALE * LOG2E

LANES = 128
SUBLANES = 8
VMEM_LIMIT_BYTES = 56 * 1024 * 1024
BIG_VMEM_LIMIT_BYTES = 60 * 1024 * 1024

COL_FQ = 0
COL_FK = COL_FQ + FOX_WIDTH
COL_FV = COL_FK + FOX_WIDTH
COL_DQ = COL_FV + FOX_WIDTH
COL_DK = COL_DQ + DIFF_QK_WIDTH
COL_DV = COL_DK + DIFF_QK_WIDTH
COL_GF = COL_DV + DIFF_WIDTH
COL_GD = COL_GF + D_MODEL
D_PROJ = COL_GD + D_MODEL

INPROJ_XT = 256
ATTN_TILE = 512
N_QBLK = SEQ // ATTN_TILE

MOE_ROWS = 1280
MOE_SUB = 256
MOE_HALF = MOE_SUB // 2
MOE_NSUB = MOE_ROWS // MOE_SUB
MOE_TF = 256
MOE_NF = D_EXPERT // MOE_TF
MOE_CHUNK = MOE_ROWS // (MOE_NF * MOE_NSUB)
N_ASSIGN = N_TOK * TOP_K
MOE_ITEMS = N_EXPERTS + N_ASSIGN // MOE_ROWS
Y_ROWS = N_ASSIGN + MOE_ROWS

f32 = jnp.float32
bf16 = jnp.bfloat16


def _params(sem, vmem=VMEM_LIMIT_BYTES):
    return pltpu.CompilerParams(dimension_semantics=sem, vmem_limit_bytes=vmem)


def _rms_bf16(x, g):
    return (x * lax.rsqrt(jnp.mean(x * x, axis=-1, keepdims=True) + RMS_EPS) * g).astype(bf16)


def _inproj_kernel(x_hbm, xm_ref, g_ref, wt_ref, wfft_ref, o_ref, om_ref, ff_ref, ffm_ref,
                   xn_ref, xnm_ref, xt_ref, wb_ref, wffb_ref, sem, *, tm, tn):
    j = pl.program_id(0)
    i = pl.program_id(1)
    rows = pl.ds(pl.multiple_of(i * tm, tm), tm)
    is_q = jnp.logical_or(j < COL_FK // tn, jnp.logical_and(j >= COL_DQ // tn, j < COL_DK // tn))
    qs = jnp.where(is_q, Q_PRESCALE, 1.0)

    @pl.when(jnp.logical_and(j == 0, i == 0))
    def _():
        wffb_ref[...] = wfft_ref[...].T.astype(bf16)
        xnm_ref[...] = _rms_bf16(xm_ref[...], g_ref[...])
        ffm_ref[...] = jnp.dot(xnm_ref[...], wffb_ref[...], preferred_element_type=f32)

    @pl.when(i == 0)
    def _():
        wb_ref[...] = wt_ref[...].T.astype(bf16)
        ym = jnp.dot(xnm_ref[...], wb_ref[...], preferred_element_type=f32)
        om_ref[...] = (ym * qs).astype(bf16)

    @pl.when(j == 0)
    def _():
        def piece(s):
            r0 = pl.multiple_of(i * tm + s * INPROJ_XT, INPROJ_XT)
            return pltpu.make_async_copy(x_hbm.at[pl.ds(r0, INPROJ_XT), :], xt_ref.at[s % 2], sem.at[s % 2])

        n_pieces = tm // INPROJ_XT
        piece(0).start()
        for s in range(n_pieces):
            if s + 1 < n_pieces:
                piece(s + 1).start()
            piece(s).wait()
            xn = _rms_bf16(xt_ref[s % 2], g_ref[...])
            xn_ref[pl.ds(pl.multiple_of(i * tm + s * INPROJ_XT, INPROJ_XT), INPROJ_XT), :] = xn
            ff_ref[s * INPROJ_XT:(s + 1) * INPROJ_XT, :] = jnp.dot(xn, wffb_ref[...], preferred_element_type=f32)

    y = jnp.dot(xn_ref[rows, :], wb_ref[...], preferred_element_type=f32)
    o_ref[...] = (y * qs).astype(bf16)


def _inproj(x2d, meta_pad, gain, w_t, tm, tn):
    n = x2d.shape[0]
    n_i = n // tm
    w_row = lambda j: SUBLANES * (j * (tn // SUBLANES) + jnp.where(j * tn >= COL_DQ, FOX_HEADS // SUBLANES, 0))
    first_pass_row = lambda j, i: jnp.where(j == 0, i, n_i - 1)
    return pl.pallas_call(
        functools.partial(_inproj_kernel, tm=tm, tn=tn),
        grid=(D_PROJ // tn, n_i),
        in_specs=[
            pl.BlockSpec(memory_space=pl.ANY),
            pl.BlockSpec((N_META_PAD, D_MODEL), lambda j, i: (0, 0)),
            pl.BlockSpec((1, D_MODEL), lambda j, i: (0, 0)),
            pl.BlockSpec((pl.Element(tn), pl.Element(D_MODEL)), lambda j, i: (w_row(j), 0)),
            pl.BlockSpec((pl.Element(LANES), pl.Element(D_MODEL)), lambda j, i: (COL_DQ, 0)),
        ],
        out_specs=[
            pl.BlockSpec((tm, tn), lambda j, i: (i, j)),
            pl.BlockSpec((N_META_PAD, tn), lambda j, i: (0, j)),
            pl.BlockSpec((tm, LANES), lambda j, i: (first_pass_row(j, i), 0)),
            pl.BlockSpec((N_META_PAD, LANES), lambda j, i: (0, 0)),
        ],
        out_shape=[
            jax.ShapeDtypeStruct((n, D_PROJ), bf16),
            jax.ShapeDtypeStruct((N_META_PAD, D_PROJ), bf16),
            jax.ShapeDtypeStruct((n, LANES), f32),
            jax.ShapeDtypeStruct((N_META_PAD, LANES), f32),
        ],
        scratch_shapes=[
            pltpu.VMEM((n, D_MODEL), bf16),
            pltpu.VMEM((N_META_PAD, D_MODEL), bf16),
            pltpu.VMEM((2, INPROJ_XT, D_MODEL), f32),
            pltpu.VMEM((D_MODEL, tn), bf16),
            pltpu.VMEM((D_MODEL, LANES), bf16),
            pltpu.SemaphoreType.DMA((2,)),
        ],
        compiler_params=_params(("arbitrary", "arbitrary"), BIG_VMEM_LIMIT_BYTES),
        name="inproj",
    )(x2d, meta_pad, gain, w_t, w_t)


def _log_sigmoid(x):
    return jnp.minimum(x, 0.0) - jnp.log(1.0 + jnp.exp(-jnp.abs(x)))


def _lane_cumsum(x):
    n = x.shape[-1]
    lane = lax.broadcasted_iota(jnp.int32, x.shape, x.ndim - 1)
    s = 1
    while s < n:
        x = x + jnp.where(lane >= s, pltpu.roll(x, s, x.ndim - 1), 0.0)
        s *= 2
    return x


def _cum_kernel(ff_ref, ffm_ref, fb_ref, ck_ref, ckm_ref):
    fb = fb_ref[...]
    row = lax.broadcasted_iota(jnp.int32, (LANES, LANES), 0)
    lfm = jnp.where(row < N_META, _log_sigmoid(ffm_ref[...] + fb), 0.0)
    cm = _lane_cumsum(lfm.T)
    m_total = cm[:, N_META - 1:N_META]
    lf = _log_sigmoid(ff_ref[...] + fb)
    cr = _lane_cumsum(lf.T) + m_total
    ck_ref[...] = cr[:SUBLANES] * LOG2E
    ckm_ref[...] = cm[:SUBLANES] * LOG2E


def _forget_cumsum(ff_real, ff_meta_pad, fb_pad):
    return pl.pallas_call(
        _cum_kernel,
        grid=(BATCH,),
        in_specs=[
            pl.BlockSpec((SEQ, LANES), lambda b: (b, 0)),
            pl.BlockSpec((LANES, LANES), lambda b: (0, 0)),
            pl.BlockSpec((1, LANES), lambda b: (0, 0)),
        ],
        out_specs=[
            pl.BlockSpec((None, SUBLANES, SEQ), lambda b: (b, 0, 0)),
            pl.BlockSpec((None, SUBLANES, LANES), lambda b: (b, 0, 0)),
        ],
        out_shape=[
            jax.ShapeDtypeStruct((BATCH, FOX_HEADS, SEQ), f32),
            jax.ShapeDtypeStruct((BATCH, FOX_HEADS, LANES), f32),
        ],
        compiler_params=_params(("parallel",)),
        name="forget_cumsum",
    )(ff_real, ff_meta_pad, fb_pad)


def _qk(q, k):
    return lax.dot_general(q, k, (((1,), (1,)), ((), ())), preferred_element_type=f32)


def _row_to_col(row):
    t = row.shape[-1]
    r = lax.broadcasted_iota(jnp.int32, (t, t), 0)
    c = lax.broadcasted_iota(jnp.int32, (t, t), 1)
    return jnp.sum(jnp.where(r == c, row, 0.0), axis=1, keepdims=True)


def _online_update(carry, s, v):
    m, l, acc = carry
    m_new = jnp.maximum(m, jnp.max(s, axis=-1, keepdims=True))
    alpha = jnp.exp2(m - m_new)
    p = jnp.exp2(s - m_new)
    l = alpha * l + jnp.sum(p, axis=-1, keepdims=True)
    acc = alpha * acc + jnp.dot(p.astype(bf16), v, preferred_element_type=f32)
    return m_new, l, acc


def _first_update(s, v):
    m = jnp.max(s, axis=-1, keepdims=True)
    p = jnp.exp2(s - m)
    l = jnp.sum(p, axis=-1, keepdims=True)
    acc = jnp.dot(p.astype(bf16), v, preferred_element_type=f32)
    return m, l, acc


FOX_GROUP = 2


def _fox_update(carry, u, cq, v):
    m_blk = jnp.max(u, axis=-1, keepdims=True) + cq
    if carry is None:
        m_new = m_blk
    else:
        m, l, acc = carry
        m_new = jnp.maximum(m, m_blk)
    p = jnp.exp2(u - (m_new - cq))
    ps = jnp.sum(p, axis=-1, keepdims=True)
    pv = jnp.dot(p.astype(bf16), v, preferred_element_type=f32)
    if carry is None:
        return m_new, ps, pv
    alpha = jnp.exp2(m - m_new)
    return m_new, alpha * l + ps, alpha * acc + pv


def _fox_kernel(q_ref, k_ref, v_ref, km_ref, vm_ref, ck_ref, ckm_ref, o_ref):
    hg = pl.program_id(1)
    i = pl.program_id(2)
    t = ATTN_TILE
    q0 = pl.multiple_of(i * t, t)
    heads = range(FOX_GROUP)
    col = lambda g: slice(g * HEAD_DIM, (g + 1) * HEAD_DIM)
    ck_row = lambda g, k0: ck_ref[pl.ds(hg * FOX_GROUP + g, 1), pl.ds(k0, t)]
    qs = [q_ref[:, col(g)] for g in heads]
    cqs = [_row_to_col(ck_row(g, q0)) for g in heads]

    meta_ok = lax.broadcasted_iota(jnp.int32, (t, N_META_PAD), 1) < N_META
    us = [_qk(qs[g], km_ref[:, col(g)]) - ckm_ref[pl.ds(hg * FOX_GROUP + g, 1), :] for g in heads]
    carry = tuple(_fox_update(None, jnp.where(meta_ok, us[g], NEG_INF), cqs[g], vm_ref[:, col(g)]) for g in heads)

    def body(j, carry):
        k0 = pl.multiple_of(j * t, t)
        us = [_qk(qs[g], k_ref[pl.ds(k0, t), col(g)]) - ck_row(g, k0) for g in heads]
        return tuple(_fox_update(carry[g], us[g], cqs[g], v_ref[pl.ds(k0, t), col(g)]) for g in heads)

    carry = lax.fori_loop(0, i, body, carry)

    r = lax.broadcasted_iota(jnp.int32, (t, t), 0)
    c = lax.broadcasted_iota(jnp.int32, (t, t), 1)
    us = [_qk(qs[g], k_ref[pl.ds(q0, t), col(g)]) - ck_row(g, q0) for g in heads]
    for g in heads:
        _, l, acc = _fox_update(carry[g], jnp.where(c <= r, us[g], NEG_INF), cqs[g], v_ref[pl.ds(q0, t), col(g)])
        o_ref[:, col(g)] = (acc / l).astype(bf16)


def _fox_attention(proj, proj_meta, ck, ckm):
    t = ATTN_TILE
    w = FOX_GROUP * HEAD_DIM
    cb = lambda col: col // w
    return pl.pallas_call(
        _fox_kernel,
        grid=(BATCH, FOX_HEADS // FOX_GROUP, N_QBLK),
        in_specs=[
            pl.BlockSpec((t, w), lambda b, h, i: (b * N_QBLK + i, cb(COL_FQ) + h)),
            pl.BlockSpec((SEQ, w), lambda b, h, i: (b, cb(COL_FK) + h)),
            pl.BlockSpec((SEQ, w), lambda b, h, i: (b, cb(COL_FV) + h)),
            pl.BlockSpec((N_META_PAD, w), lambda b, h, i: (0, cb(COL_FK) + h)),
            pl.BlockSpec((N_META_PAD, w), lambda b, h, i: (0, cb(COL_FV) + h)),
            pl.BlockSpec((None, FOX_HEADS, SEQ), lambda b, h, i: (b, 0, 0)),
            pl.BlockSpec((None, FOX_HEADS, LANES), lambda b, h, i: (b, 0, 0)),
        ],
        out_specs=pl.BlockSpec((t, w), lambda b, h, i: (b * N_QBLK + i, h)),
        out_shape=jax.ShapeDtypeStruct((N_TOK, FOX_WIDTH), bf16),
        compiler_params=_params(("parallel", "parallel", "arbitrary")),
        name="fox_attention",
    )(proj, proj, proj, proj_meta, proj_meta, ck, ckm)


def _t5_bias(dist, table_ref, h):
    n = jnp.maximum(dist, 0)
    max_exact = N_BUCKETS // 2
    log_part = jnp.log(jnp.maximum(n, 1).astype(f32) / max_exact) / math.log(MAX_DISTANCE / max_exact)
    v = log_part * (N_BUCKETS - max_exact)
    far = table_ref[N_BUCKETS - 1, h]
    val = lambda b: (table_ref[b, h] - far) * LOG2E
    large = jnp.zeros(dist.shape, f32)
    for b in range(N_BUCKETS - 2, max_exact - 1, -1):
        large = jnp.where(v < b + 1 - max_exact, val(b), large)
    out = large
    for b in range(max_exact):
        out = jnp.where(n == b, val(b), out)
    return out


def _bias_kernel(table_ref, near_ref, meta_ref):
    h = pl.program_id(0)
    t = ATTN_TILE
    r = lax.broadcasted_iota(jnp.int32, (t, 2 * t), 0)
    c = lax.broadcasted_iota(jnp.int32, (t, 2 * t), 1)
    near_ref[...] = _t5_bias(r + t - c, table_ref, h)
    r = lax.broadcasted_iota(jnp.int32, (t, LANES), 0)
    c = lax.broadcasted_iota(jnp.int32, (t, LANES), 1)
    meta_ref[...] = _t5_bias(N_META + r - c, table_ref, h)


def _bias_tiles(table):
    t = ATTN_TILE
    return pl.pallas_call(
        _bias_kernel,
        grid=(DIFF_HEADS,),
        in_specs=[pl.BlockSpec(memory_space=pltpu.SMEM)],
        out_specs=[
            pl.BlockSpec((None, t, 2 * t), lambda h: (h, 0, 0)),
            pl.BlockSpec((None, t, LANES), lambda h: (h, 0, 0)),
        ],
        out_shape=[
            jax.ShapeDtypeStruct((DIFF_HEADS, t, 2 * t), f32),
            jax.ShapeDtypeStruct((DIFF_HEADS, t, LANES), f32),
        ],
        compiler_params=_params(("arbitrary",)),
        name="t5_bias_tiles",
    )(table)


def _diff_kernel(q1_ref, q2_ref, k1_ref, k2_ref, v_ref, k1m_ref, k2m_ref, vm_ref,
                 near_ref, mbias_ref, lam_ref, subln_ref, o_ref):
    i = pl.program_id(2)
    t = ATTN_TILE
    q1 = q1_ref[...]
    q2 = q2_ref[...]
    q0 = pl.multiple_of(i * t, t)
    lam = (jnp.exp(jnp.sum(lam_ref[0:1, :] * lam_ref[1:2, :], axis=-1, keepdims=True))
           - jnp.exp(jnp.sum(lam_ref[2:3, :] * lam_ref[3:4, :], axis=-1, keepdims=True))
           + LAMBDA_INIT)

    mb = jnp.where(i == 0, mbias_ref[...], 0.0)
    meta_ok = lax.broadcasted_iota(jnp.int32, (t, N_META_PAD), 1) < N_META
    vm = vm_ref[...]
    c1 = _first_update(jnp.where(meta_ok, _qk(q1, k1m_ref[...]) + mb, NEG_INF), vm)
    c2 = _first_update(jnp.where(meta_ok, _qk(q2, k2m_ref[...]) + mb, NEG_INF), vm)

    def far_body(j, carry):
        c1, c2 = carry
        k0 = pl.multiple_of(j * t, t)
        v = v_ref[pl.ds(k0, t), :]
        c1 = _online_update(c1, _qk(q1, k1_ref[pl.ds(k0, t), :]), v)
        c2 = _online_update(c2, _qk(q2, k2_ref[pl.ds(k0, t), :]), v)
        return c1, c2

    c1, c2 = lax.fori_loop(0, jnp.maximum(i - 1, 0), far_body, (c1, c2))

    def prev_block(carry):
        c1, c2 = carry
        k0 = pl.multiple_of((i - 1) * t, t)
        v = v_ref[pl.ds(k0, t), :]
        bias = near_ref[:, :t]
        c1 = _online_update(c1, _qk(q1, k1_ref[pl.ds(k0, t), :]) + bias, v)
        c2 = _online_update(c2, _qk(q2, k2_ref[pl.ds(k0, t), :]) + bias, v)
        return c1, c2

    c1, c2 = lax.cond(i > 0, prev_block, lambda carry: carry, (c1, c2))

    r = lax.broadcasted_iota(jnp.int32, (t, t), 0)
    c = lax.broadcasted_iota(jnp.int32, (t, t), 1)
    mask = c <= r
    bias = near_ref[:, t:]
    v = v_ref[pl.ds(q0, t), :]
    s1 = jnp.where(mask, _qk(q1, k1_ref[pl.ds(q0, t), :]) + bias, NEG_INF)
    s2 = jnp.where(mask, _qk(q2, k2_ref[pl.ds(q0, t), :]) + bias, NEG_INF)
    _, l1, a1 = _online_update(c1, s1, v)
    _, l2, a2 = _online_update(c2, s2, v)

    o = a1 / l1 - lam * (a2 / l2)
    y = o * lax.rsqrt(jnp.mean(o * o, axis=-1, keepdims=True) + RMS_EPS) * subln_ref[...]
    o_ref[...] = (y * (1.0 - LAMBDA_INIT)).astype(bf16)


def _diff_attention(proj, proj_meta, near, mbias, lam_vecs, subln):
    t = ATTN_TILE
    cb = lambda col: col // HEAD_DIM
    vb = lambda col: col // DIFF_V_DIM
    row = lambda b, h, i: b * N_QBLK + i
    return pl.pallas_call(
        _diff_kernel,
        grid=(BATCH, DIFF_HEADS, N_QBLK),
        in_specs=[
            pl.BlockSpec((t, HEAD_DIM), lambda b, h, i: (row(b, h, i), cb(COL_DQ) + 2 * h)),
            pl.BlockSpec((t, HEAD_DIM), lambda b, h, i: (row(b, h, i), cb(COL_DQ) + 2 * h + 1)),
            pl.BlockSpec((SEQ, HEAD_DIM), lambda b, h, i: (b, cb(COL_DK) + 2 * h)),
            pl.BlockSpec((SEQ, HEAD_DIM), lambda b, h, i: (b, cb(COL_DK) + 2 * h + 1)),
            pl.BlockSpec((SEQ, DIFF_V_DIM), lambda b, h, i: (b, vb(COL_DV) + h)),
            pl.BlockSpec((N_META_PAD, HEAD_DIM), lambda b, h, i: (0, cb(COL_DK) + 2 * h)),
            pl.BlockSpec((N_META_PAD, HEAD_DIM), lambda b, h, i: (0, cb(COL_DK) + 2 * h + 1)),
            pl.BlockSpec((N_META_PAD, DIFF_V_DIM), lambda b, h, i: (0, vb(COL_DV) + h)),
            pl.BlockSpec((None, t, 2 * t), lambda b, h, i: (h, 0, 0)),
            pl.BlockSpec((None, t, LANES), lambda b, h, i: (h, 0, 0)),
            pl.BlockSpec((4, HEAD_DIM), lambda b, h, i: (0, 0)),
            pl.BlockSpec((1, DIFF_V_DIM), lambda b, h, i: (0, 0)),
        ],
        out_specs=pl.BlockSpec((t, DIFF_V_DIM), lambda b, h, i: (row(b, h, i), h)),
        out_shape=jax.ShapeDtypeStruct((N_TOK, DIFF_WIDTH), bf16),
        compiler_params=_params(("parallel", "parallel", "arbitrary")),
        name="diff_attention",
    )(proj, proj, proj, proj, proj, proj_meta, proj_meta, proj_meta, near, mbias, lam_vecs, subln)


def _merge_kernel(of_ref, od_ref, wf_ref, wd_ref, gf_ref, gd_ref, o_ref):
    yf = jnp.dot(of_ref[...], wf_ref[...], preferred_element_type=f32)
    yd = jnp.dot(od_ref[...], wd_ref[...], preferred_element_type=f32)
    gf = jax.nn.sigmoid(gf_ref[...].astype(f32))
    gd = jax.nn.sigmoid(gd_ref[...].astype(f32))
    o_ref[...] = (gf * yf + gd * yd).astype(bf16)


def _gated_merge(o_fox, o_diff, w_bf, w_bd, proj, tm=1024, tn=512):
    return pl.pallas_call(
        _merge_kernel,
        grid=(D_MODEL // tn, N_TOK // tm),
        in_specs=[
            pl.BlockSpec((tm, FOX_WIDTH), lambda j, i: (i, 0)),
            pl.BlockSpec((tm, DIFF_WIDTH), lambda j, i: (i, 0)),
            pl.BlockSpec((FOX_WIDTH, tn), lambda j, i: (0, j)),
            pl.BlockSpec((DIFF_WIDTH, tn), lambda j, i: (0, j)),
            pl.BlockSpec((tm, tn), lambda j, i: (i, COL_GF // tn + j)),
            pl.BlockSpec((tm, tn), lambda j, i: (i, COL_GD // tn + j)),
        ],
        out_specs=pl.BlockSpec((tm, tn), lambda j, i: (i, j)),
        out_shape=jax.ShapeDtypeStruct((N_TOK, D_MODEL), bf16),
        compiler_params=_params(("parallel", "arbitrary")),
        name="gated_merge",
    )(o_fox, o_diff, w_bf, w_bd, proj, proj)


def _outproj_kernel(m_ref, w_ref, x_ref, g_ref, wr_ref, br_ref, h_ref, hn_ref, lg_ref):
    h1 = x_ref[...] + jnp.dot(m_ref[...], w_ref[...], preferred_element_type=f32)
    h_ref[...] = h1
    hn = h1 * lax.rsqrt(jnp.mean(h1 * h1, axis=-1, keepdims=True) + RMS_EPS) * g_ref[...]
    hn_ref[...] = hn
    hn_hi = hn.astype(bf16)
    hn_lo = (hn - hn_hi.astype(f32)).astype(bf16)
    a = jnp.dot(hn_hi, wr_ref[...], preferred_element_type=f32)
    b = jnp.dot(hn_lo, wr_ref[:, :LANES], preferred_element_type=f32)
    lg_ref[...] = a[:, :LANES] + a[:, LANES:] + b + br_ref[...]


def _outproj(merged, w_out, x2d, gain, w_router_split, b_router_pad, tm=512):
    return pl.pallas_call(
        _outproj_kernel,
        grid=(N_TOK // tm,),
        in_specs=[
            pl.BlockSpec((tm, D_MODEL), lambda i: (i, 0)),
            pl.BlockSpec((D_MODEL, D_MODEL), lambda i: (0, 0)),
            pl.BlockSpec((tm, D_MODEL), lambda i: (i, 0)),
            pl.BlockSpec((1, D_MODEL), lambda i: (0, 0)),
            pl.BlockSpec((D_MODEL, 2 * LANES), lambda i: (0, 0)),
            pl.BlockSpec((1, LANES), lambda i: (0, 0)),
        ],
        out_specs=[
            pl.BlockSpec((tm, D_MODEL), lambda i: (i, 0)),
            pl.BlockSpec((tm, D_MODEL), lambda i: (i, 0)),
            pl.BlockSpec((tm, LANES), lambda i: (i, 0)),
        ],
        out_shape=[
            jax.ShapeDtypeStruct((N_TOK, D_MODEL), f32),
            jax.ShapeDtypeStruct((N_TOK, D_MODEL), f32),
            jax.ShapeDtypeStruct((N_TOK, LANES), f32),
        ],
        compiler_params=_params(("parallel",)),
        name="outproj_router",
    )(merged, w_out, x2d, gain, w_router_split, b_router_pad)


def _moe_kernel(item_e_ref, item_start_ref, item_n_ref, tok_ref, dest_ref,
                hn_hbm, wg_ref, wl_ref, bg_ref, bl_ref, wd_ref, bd_ref, y_hbm,
                acc_ref, xg_ref, xb_ref, wgb_ref, wlb_ref, wdb_ref, gsem, ssem):
    w = pl.program_id(0)
    t = pl.program_id(1)
    last_w = MOE_ITEMS - 1
    n = item_n_ref[w]
    start = item_start_ref[w]
    slot = lax.rem(w, 2)
    w_next = jnp.minimum(w + 1, last_w)
    start_next = item_start_ref[w_next]
    w_prev = jnp.maximum(w - 1, 0)
    start_prev = item_start_ref[w_prev]
    n_prev = jnp.where(w > 0, item_n_ref[w_prev], 0)
    is_last_live = jnp.logical_or(w == last_w, item_n_ref[w_next] == 0)

    def gather_row(row, base):
        tk = tok_ref[base + row]
        pltpu.make_async_copy(hn_hbm.at[pl.ds(tk, 1), :], xg_ref.at[pl.ds(row, 1), :], gsem).start()

    def gather_wait():
        pltpu.make_async_copy(hn_hbm.at[pl.ds(0, MOE_ROWS), :], xg_ref, gsem).wait()

    def scatter_row(row, base, cnt, sl, priority=0):
        d = jnp.where(row < cnt, dest_ref[base + row], N_ASSIGN + row)
        pltpu.make_async_copy(acc_ref.at[sl, pl.ds(row, 1), :], y_hbm.at[pl.ds(d, 1), :],
                              ssem).start(priority=priority)

    def scatter_wait(sl):
        pltpu.make_async_copy(acc_ref.at[sl], y_hbm.at[pl.ds(0, MOE_ROWS), :], ssem).wait()

    @pl.when(jnp.logical_and(w == 0, t == 0))
    def _():
        acc_ref[1] = jnp.zeros((MOE_ROWS, D_MODEL), f32)

        def issue(r, c):
            gather_row(r, start)
            return c
        lax.fori_loop(0, MOE_ROWS, issue, 0, unroll=8)

    @pl.when(jnp.logical_and(t == 0, n > 0))
    def _():
        gather_wait()
        xb_ref[...] = xg_ref[...].astype(bf16)
        acc_ref[slot] = jnp.broadcast_to(bd_ref[...], (MOE_ROWS, D_MODEL))

    def dma_chunk(r):
        row0 = t * (MOE_NSUB * MOE_CHUNK) + r * MOE_CHUNK
        for k in range(MOE_CHUNK):
            gather_row(row0 + k, start_next)
        for k in range(MOE_CHUNK):
            scatter_row(row0 + k, start_prev, n_prev, 1 - slot, priority=k % 2)

    def ffn_up(r0, rows):
        xs = xb_ref[r0:r0 + rows, :]
        hg = jnp.dot(xs, wgb_ref[...], preferred_element_type=f32) + bg_ref[...]
        hl = jnp.dot(xs, wlb_ref[...], preferred_element_type=f32) + bl_ref[...]
        return hg, hl

    def ffn_down(r0, rows, hg, hl):
        hg = jnp.minimum(hg, SWIGLU_LIMIT)
        hl = jnp.clip(hl, -SWIGLU_LIMIT, SWIGLU_LIMIT)
        act = hg * jax.nn.sigmoid(SWIGLU_ALPHA * hg) * (hl + 1.0)
        acc_ref[slot, r0:r0 + rows, :] += jnp.dot(act.astype(bf16), wdb_ref[...], preferred_element_type=f32)

    def ffn_rows(r0, rows):
        ffn_down(r0, rows, *ffn_up(r0, rows))

    @pl.when(n > 0)
    def _():
        nb_full = n // MOE_SUB
        rem = n - nb_full * MOE_SUB
        ext = jnp.logical_and(jnp.logical_and(rem > 0, rem <= MOE_HALF), nb_full >= 1)
        own_block = jnp.logical_or(rem > MOE_HALF, jnp.logical_and(rem > 0, nb_full == 0))
        nb = nb_full + jnp.where(own_block, 1, 0)
        tiny = jnp.logical_and(nb_full == 0, rem <= MOE_HALF)

        def cast_weights(r):
            if r == 0:
                wgb_ref[...] = wg_ref[...].astype(bf16)
                wlb_ref[...] = wl_ref[...].astype(bf16)
                wdb_ref[...] = wd_ref[...].astype(bf16)

        def single(r, rows, chunks=1):
            def body():
                for c in range(chunks):
                    dma_chunk(r + c)
                cast_weights(r)
                ffn_rows(r * MOE_SUB, rows)
            return body

        def pair(r, rows_b):
            def body():
                dma_chunk(r)
                dma_chunk(r + 1)
                cast_weights(r)
                ra, rb = r * MOE_SUB, (r + 1) * MOE_SUB
                ha = ffn_up(ra, MOE_SUB)
                hb = ffn_up(rb, rows_b)
                ffn_down(ra, MOE_SUB, *ha)
                ffn_down(rb, rows_b, *hb)
            return body

        def idle(*rs):
            def body():
                for r in rs:
                    dma_chunk(r)
            return body

        def last_single(r):
            return lambda: lax.cond(
                ext, single(r, MOE_SUB + MOE_HALF, 2),
                lambda: lax.cond(tiny, single(r, MOE_HALF, 2), single(r, MOE_SUB, 2)))

        for r in range(0, MOE_NSUB - 1, 2):
            is_last_pair = jnp.logical_and(ext, nb == r + 2)
            lax.cond(
                nb >= r + 2,
                lambda r=r, is_last_pair=is_last_pair: lax.cond(
                    is_last_pair, pair(r, MOE_SUB + MOE_HALF), pair(r, MOE_SUB)),
                lambda r=r: lax.cond(nb == r + 1, last_single(r), idle(r, r + 1)))
        for r in range(MOE_NSUB - MOE_NSUB % 2, MOE_NSUB):
            lax.cond(nb == r + 1, single(r, MOE_SUB), idle(r))

    @pl.when(jnp.logical_and(t == MOE_NF - 1, n > 0))
    def _():
        scatter_wait(1 - slot)

        @pl.when(is_last_live)
        def _():
            def issue(r, c):
                scatter_row(r, start, n, slot)
                return c
            lax.fori_loop(0, MOE_ROWS, issue, 0, unroll=8)
            scatter_wait(slot)
            gather_wait()


def _moe(item_e, item_start, item_n, tok_sorted, dest_sorted, hn2, w_gate_up, b_gate_up, w_down, b_down):
    tf = MOE_TF
    live_t = lambda w, t, n: jnp.where(n[w] > 0, t, MOE_NF - 1)
    grid_spec = pltpu.PrefetchScalarGridSpec(
        num_scalar_prefetch=5,
        grid=(MOE_ITEMS, MOE_NF),
        in_specs=[
            pl.BlockSpec(memory_space=pl.ANY),
            pl.BlockSpec((None, D_MODEL, tf), lambda w, t, e, s, n, tk, ds: (e[w], 0, live_t(w, t, n))),
            pl.BlockSpec((None, D_MODEL, tf), lambda w, t, e, s, n, tk, ds: (e[w], 0, MOE_NF + live_t(w, t, n))),
            pl.BlockSpec((None, 1, tf), lambda w, t, e, s, n, tk, ds: (e[w], 0, live_t(w, t, n))),
            pl.BlockSpec((None, 1, tf), lambda w, t, e, s, n, tk, ds: (e[w], 0, MOE_NF + live_t(w, t, n))),
            pl.BlockSpec((None, tf, D_MODEL), lambda w, t, e, s, n, tk, ds: (e[w], live_t(w, t, n), 0)),
            pl.BlockSpec((None, 1, D_MODEL), lambda w, t, e, s, n, tk, ds: (e[w], 0, 0)),
        ],
        out_specs=pl.BlockSpec(memory_space=pl.ANY),
        scratch_shapes=[
            pltpu.VMEM((2, MOE_ROWS, D_MODEL), f32),
            pltpu.VMEM((MOE_ROWS, D_MODEL), f32),
            pltpu.VMEM((MOE_ROWS, D_MODEL), bf16),
            pltpu.VMEM((D_MODEL, tf), bf16),
            pltpu.VMEM((D_MODEL, tf), bf16),
            pltpu.VMEM((tf, D_MODEL), bf16),
            pltpu.SemaphoreType.DMA,
            pltpu.SemaphoreType.DMA,
        ],
    )
    return pl.pallas_call(
        _moe_kernel,
        grid_spec=grid_spec,
        out_shape=jax.ShapeDtypeStruct((Y_ROWS, D_MODEL), f32),
        compiler_params=_params(("arbitrary", "arbitrary"), BIG_VMEM_LIMIT_BYTES),
        name="moe_experts",
    )(item_e, item_start, item_n, tok_sorted, dest_sorted,
      hn2, w_gate_up, w_gate_up, b_gate_up, b_gate_up, w_down, b_down)


def _combine_kernel(y0_ref, y1_ref, y2_ref, y3_ref, h_ref, gate_ref, g_ref, o_ref):
    h = h_ref[...]
    gates = gate_ref[...]
    for k, y_ref in enumerate((y0_ref, y1_ref, y2_ref, y3_ref)):
        h = h + y_ref[...] * gates[:, k:k + 1]
    o_ref[...] = h * lax.rsqrt(jnp.mean(h * h, axis=-1, keepdims=True) + RMS_EPS) * g_ref[...]


def _combine(y, h1, gates, gain, tm=256):
    slot_spec = lambda k: pl.BlockSpec((tm, D_MODEL), lambda i, k=k: (k * (N_TOK // tm) + i, 0))
    return pl.pallas_call(
        _combine_kernel,
        grid=(N_TOK // tm,),
        in_specs=[slot_spec(k) for k in range(TOP_K)] + [
            pl.BlockSpec((tm, D_MODEL), lambda i: (i, 0)),
            pl.BlockSpec((tm, TOP_K), lambda i: (i, 0)),
            pl.BlockSpec((1, D_MODEL), lambda i: (0, 0)),
        ],
        out_specs=pl.BlockSpec((tm, D_MODEL), lambda i: (i, 0)),
        out_shape=jax.ShapeDtypeStruct((N_TOK, D_MODEL), f32),
        compiler_params=_params(("parallel",)),
        name="combine_norm",
    )(y, y, y, y, h1, gates, gain)


def _route(logits):
    top_vals, top_idx = lax.top_k(logits, TOP_K)
    gates = jax.nn.softmax(top_vals, axis=-1)
    expert_flat = top_idx.reshape(-1).astype(jnp.int32)
    order = jnp.argsort(expert_flat, stable=True).astype(jnp.int32)
    counts = jnp.zeros((N_EXPERTS,), jnp.int32).at[expert_flat].add(1)
    starts = jnp.cumsum(counts) - counts
    tok_sorted = order // TOP_K
    dest_sorted = (order % TOP_K) * N_TOK + tok_sorted
    tok_sorted = jnp.pad(tok_sorted, (0, MOE_ROWS))
    dest_sorted = jnp.pad(dest_sorted, (0, MOE_ROWS))
    chunks = (counts + MOE_ROWS - 1) // MOE_ROWS
    chunk_end = jnp.cumsum(chunks)
    n_items = chunk_end[-1]
    w = jnp.arange(MOE_ITEMS, dtype=jnp.int32)
    w_eff = jnp.minimum(w, n_items - 1)
    e = jnp.sum(w_eff[:, None] >= chunk_end[None, :], axis=1).astype(jnp.int32)
    c = w_eff - (chunk_end - chunks)[e]
    item_start = starts[e] + c * MOE_ROWS
    item_n = jnp.where(w < n_items, jnp.clip(counts[e] - c * MOE_ROWS, 0, MOE_ROWS), 0)
    return gates, e, item_start.astype(jnp.int32), item_n.astype(jnp.int32), tok_sorted, dest_sorted


def kernel(x, meta_tokens, rel_bias_table, attn_norm, w_in, fox_forget_bias, lam_q1, lam_k1, lam_q2, lam_k2,
           diff_subln, w_branch_fox, w_branch_diff, w_out, ffn_norm, w_router, b_router, w_gate_up, b_gate_up,
           w_down, b_down, final_norm):
    x2d = x.reshape(N_TOK, D_MODEL)
    w_t = jnp.swapaxes(w_in[0], 0, 1)
    fb_pad = jnp.pad(fox_forget_bias[0], (0, LANES - FOX_HEADS)).reshape(1, LANES)

    meta_pad = jnp.pad(meta_tokens, ((0, N_META_PAD - N_META), (0, 0)))
    proj, proj_meta, ff_real, ff_meta = _inproj(x2d, meta_pad, attn_norm, w_t, tm=1024, tn=512)

    ck, ckm = _forget_cumsum(ff_real, ff_meta, fb_pad)
    o_fox = _fox_attention(proj, proj_meta, ck, ckm)

    near, mbias = _bias_tiles(rel_bias_table)
    lam_vecs = jnp.concatenate([lam_q1, lam_k1, lam_q2, lam_k2], axis=0)
    o_diff = _diff_attention(proj, proj_meta, near, mbias, lam_vecs, diff_subln)

    merged = _gated_merge(o_fox, o_diff, w_branch_fox[0].astype(bf16), w_branch_diff[0].astype(bf16), proj)
    w_router_pad = jnp.pad(w_router[0], ((0, 0), (0, LANES - N_EXPERTS)))
    w_router_hi = w_router_pad.astype(bf16)
    w_router_lo = (w_router_pad - w_router_hi.astype(f32)).astype(bf16)
    w_router_split = jnp.concatenate([w_router_hi, w_router_lo], axis=1)
    b_router_pad = jnp.pad(b_router[0], (0, LANES - N_EXPERTS)).reshape(1, LANES)
    h1, hn2, logits = _outproj(merged, w_out[0].astype(bf16), x2d, ffn_norm, w_router_split, b_router_pad)

    gates, item_e, item_start, item_n, tok_sorted, dest_sorted = _route(logits[:, :N_EXPERTS])
    y = _moe(item_e, item_start, item_n, tok_sorted, dest_sorted, hn2,
             w_gate_up[0], b_gate_up[0].reshape(N_EXPERTS, 1, 2 * D_EXPERT),
             w_down[0], b_down[0].reshape(N_EXPERTS, 1, D_MODEL))
    out = _combine(y, h1, gates, final_norm.reshape(1, D_MODEL))
    return out.reshape(BATCH, SEQ, D_MODEL)
```

```python
import functools
import math

import jax
import jax.numpy as jnp
from jax import lax
from jax.experimental import pallas as pl
from jax.experimental.pallas import tpu as pltpu

D_MODEL = 2048
BATCH = 4
SEQ = 2048
N_TOK = BATCH * SEQ
N_META = 16
N_META_PAD = 128
HEAD_DIM = 128
FOX_HEADS = 8
DIFF_HEADS = 4
DIFF_V_DIM = 2 * HEAD_DIM
FOX_WIDTH = FOX_HEADS * HEAD_DIM
DIFF_QK_WIDTH = DIFF_HEADS * 2 * HEAD_DIM
DIFF_WIDTH = DIFF_HEADS * DIFF_V_DIM
N_BUCKETS = 32
MAX_DISTANCE = 128
N_EXPERTS = 32
TOP_K = 4
D_EXPERT = D_MODEL
SWIGLU_LIMIT = 7.0
SWIGLU_ALPHA = 1.702
RMS_EPS = 1e-5
NEG_INF = -1e30
LAMBDA_INIT = 0.8 - 0.6 * math.exp(-0.3 * 0)
ATTN_SCALE = HEAD_DIM ** -0.5
LOG2E = math.log2(math.e)
Q_PRESCALE = ATTN_SCALE * LOG2E

LANES = 128
SUBLANES = 8
VMEM_LIMIT_BYTES = 56 * 1024 * 1024
BIG_VMEM_LIMIT_BYTES = 60 * 1024 * 1024

COL_FQ = 0
COL_FK = COL_FQ + FOX_WIDTH
COL_FV = COL_FK + FOX_WIDTH
COL_DQ = COL_FV + FOX_WIDTH
COL_DK = COL_DQ + DIFF_QK_WIDTH
COL_DV = COL_DK + DIFF_QK_WIDTH
COL_GF = COL_DV + DIFF_WIDTH
COL_GD = COL_GF + D_MODEL
D_PROJ = COL_GD + D_MODEL

INPROJ_XT = 256
ATTN_TILE = 512
N_QBLK = SEQ // ATTN_TILE

MOE_ROWS = 1280
MOE_SUB = 256
MOE_HALF = MOE_SUB // 2
MOE_NSUB = MOE_ROWS // MOE_SUB
MOE_TF = 256
MOE_NF = D_EXPERT // MOE_TF
MOE_CHUNK = MOE_ROWS // (MOE_NF * MOE_NSUB)
N_ASSIGN = N_TOK * TOP_K
MOE_ITEMS = N_EXPERTS + N_ASSIGN // MOE_ROWS
Y_ROWS = N_ASSIGN + MOE_ROWS

f32 = jnp.float32
bf16 = jnp.bfloat16


def _params(sem, vmem=VMEM_LIMIT_BYTES):
    return pltpu.CompilerParams(dimension_semantics=sem, vmem_limit_bytes=vmem)


def _rms_bf16(x, g):
    return (x * lax.rsqrt(jnp.mean(x * x, axis=-1, keepdims=True) + RMS_EPS) * g).astype(bf16)


def _inproj_kernel(x_hbm, xm_ref, g_ref, wt_ref, wfft_ref, o_ref, om_ref, ff_ref, ffm_ref,
                   xn_ref, xnm_ref, xt_ref, wb_ref, wffb_ref, sem, *, tm, tn):
    j = pl.program_id(0)
    i = pl.program_id(1)
    rows = pl.ds(pl.multiple_of(i * tm, tm), tm)
    is_q = jnp.logical_or(j < COL_FK // tn, jnp.logical_and(j >= COL_DQ // tn, j < COL_DK // tn))
    qs = jnp.where(is_q, Q_PRESCALE, 1.0)

    @pl.when(jnp.logical_and(j == 0, i == 0))
    def _():
        wffb_ref[...] = wfft_ref[...].T.astype(bf16)
        xnm_ref[...] = _rms_bf16(xm_ref[...], g_ref[...])
        ffm_ref[...] = jnp.dot(xnm_ref[...], wffb_ref[...], preferred_element_type=f32)

    @pl.when(i == 0)
    def _():
        wb_ref[...] = wt_ref[...].T.astype(bf16)
        ym = jnp.dot(xnm_ref[...], wb_ref[...], preferred_element_type=f32)
        om_ref[...] = (ym * qs).astype(bf16)

    @pl.when(j == 0)
    def _():
        def piece(s):
            r0 = pl.multiple_of(i * tm + s * INPROJ_XT, INPROJ_XT)
            return pltpu.make_async_copy(x_hbm.at[pl.ds(r0, INPROJ_XT), :], xt_ref.at[s % 2], sem.at[s % 2])

        n_pieces = tm // INPROJ_XT
        piece(0).start()
        for s in range(n_pieces):
            if s + 1 < n_pieces:
                piece(s + 1).start()
            piece(s).wait()
            xn = _rms_bf16(xt_ref[s % 2], g_ref[...])
            xn_ref[pl.ds(pl.multiple_of(i * tm + s * INPROJ_XT, INPROJ_XT), INPROJ_XT), :] = xn
            ff_ref[s * INPROJ_XT:(s + 1) * INPROJ_XT, :] = jnp.dot(xn, wffb_ref[...], preferred_element_type=f32)

    y = jnp.dot(xn_ref[rows, :], wb_ref[...], preferred_element_type=f32)
    o_ref[...] = (y * qs).astype(bf16)


def _inproj(x2d, meta_pad, gain, w_t, tm, tn):
    n = x2d.shape[0]
    n_i = n // tm
    w_row = lambda j: SUBLANES * (j * (tn // SUBLANES) + jnp.where(j * tn >= COL_DQ, FOX_HEADS // SUBLANES, 0))
    first_pass_row = lambda j, i: jnp.where(j == 0, i, n_i - 1)
    return pl.pallas_call(
        functools.partial(_inproj_kernel, tm=tm, tn=tn),
        grid=(D_PROJ // tn, n_i),
        in_specs=[
            pl.BlockSpec(memory_space=pl.ANY),
            pl.BlockSpec((N_META_PAD, D_MODEL), lambda j, i: (0, 0)),
            pl.BlockSpec((1, D_MODEL), lambda j, i: (0, 0)),
            pl.BlockSpec((pl.Element(tn), pl.Element(D_MODEL)), lambda j, i: (w_row(j), 0)),
            pl.BlockSpec((pl.Element(LANES), pl.Element(D_MODEL)), lambda j, i: (COL_DQ, 0)),
        ],
        out_specs=[
            pl.BlockSpec((tm, tn), lambda j, i: (i, j)),
            pl.BlockSpec((N_META_PAD, tn), lambda j, i: (0, j)),
            pl.BlockSpec((tm, LANES), lambda j, i: (first_pass_row(j, i), 0)),
            pl.BlockSpec((N_META_PAD, LANES), lambda j, i: (0, 0)),
        ],
        out_shape=[
            jax.ShapeDtypeStruct((n, D_PROJ), bf16),
            jax.ShapeDtypeStruct((N_META_PAD, D_PROJ), bf16),
            jax.ShapeDtypeStruct((n, LANES), f32),
            jax.ShapeDtypeStruct((N_META_PAD, LANES), f32),
        ],
        scratch_shapes=[
            pltpu.VMEM((n, D_MODEL), bf16),
            pltpu.VMEM((N_META_PAD, D_MODEL), bf16),
            pltpu.VMEM((2, INPROJ_XT, D_MODEL), f32),
            pltpu.VMEM((D_MODEL, tn), bf16),
            pltpu.VMEM((D_MODEL, LANES), bf16),
            pltpu.SemaphoreType.DMA((2,)),
        ],
        compiler_params=_params(("arbitrary", "arbitrary"), BIG_VMEM_LIMIT_BYTES),
        name="inproj",
    )(x2d, meta_pad, gain, w_t, w_t)


def _log_sigmoid(x):
    return jnp.minimum(x, 0.0) - jnp.log(1.0 + jnp.exp(-jnp.abs(x)))


def _lane_cumsum(x):
    n = x.shape[-1]
    lane = lax.broadcasted_iota(jnp.int32, x.shape, x.ndim - 1)
    s = 1
    while s < n:
        x = x + jnp.where(lane >= s, pltpu.roll(x, s, x.ndim - 1), 0.0)
        s *= 2
    return x


def _cum_kernel(ff_ref, ffm_ref, fb_ref, ck_ref, ckm_ref):
    fb = fb_ref[...]
    row = lax.broadcasted_iota(jnp.int32, (LANES, LANES), 0)
    lfm = jnp.where(row < N_META, _log_sigmoid(ffm_ref[...] + fb), 0.0)
    cm = _lane_cumsum(lfm.T)
    m_total = cm[:, N_META - 1:N_META]
    lf = _log_sigmoid(ff_ref[...] + fb)
    cr = _lane_cumsum(lf.T) + m_total
    ck_ref[...] = cr[:SUBLANES] * LOG2E
    ckm_ref[...] = cm[:SUBLANES] * LOG2E


def _forget_cumsum(ff_real, ff_meta_pad, fb_pad):
    return pl.pallas_call(
        _cum_kernel,
        grid=(BATCH,),
        in_specs=[
            pl.BlockSpec((SEQ, LANES), lambda b: (b, 0)),
            pl.BlockSpec((LANES, LANES), lambda b: (0, 0)),
            pl.BlockSpec((1, LANES), lambda b: (0, 0)),
        ],
        out_specs=[
            pl.BlockSpec((None, SUBLANES, SEQ), lambda b: (b, 0, 0)),
            pl.BlockSpec((None, SUBLANES, LANES), lambda b: (b, 0, 0)),
        ],
        out_shape=[
            jax.ShapeDtypeStruct((BATCH, FOX_HEADS, SEQ), f32),
            jax.ShapeDtypeStruct((BATCH, FOX_HEADS, LANES), f32),
        ],
        compiler_params=_params(("parallel",)),
        name="forget_cumsum",
    )(ff_real, ff_meta_pad, fb_pad)


def _qk(q, k):
    return lax.dot_general(q, k, (((1,), (1,)), ((), ())), preferred_element_type=f32)


def _row_to_col(row):
    t = row.shape[-1]
    r = lax.broadcasted_iota(jnp.int32, (t, t), 0)
    c = lax.broadcasted_iota(jnp.int32, (t, t), 1)
    return jnp.sum(jnp.where(r == c, row, 0.0), axis=1, keepdims=True)


def _online_update(carry, s, v):
    m, l, acc = carry
    m_new = jnp.maximum(m, jnp.max(s, axis=-1, keepdims=True))
    alpha = jnp.exp2(m - m_new)
    p = jnp.exp2(s - m_new)
    l = alpha * l + jnp.sum(p, axis=-1, keepdims=True)
    acc = alpha * acc + jnp.dot(p.astype(bf16), v, preferred_element_type=f32)
    return m_new, l, acc


def _first_update(s, v):
    m = jnp.max(s, axis=-1, keepdims=True)
    p = jnp.exp2(s - m)
    l = jnp.sum(p, axis=-1, keepdims=True)
    acc = jnp.dot(p.astype(bf16), v, preferred_element_type=f32)
    return m, l, acc


FOX_GROUP = 2


def _fox_update(carry, u, cq, v):
    m_blk = jnp.max(u, axis=-1, keepdims=True) + cq
    if carry is None:
        m_new = m_blk
    else:
        m, l, acc = carry
        m_new = jnp.maximum(m, m_blk)
    p = jnp.exp2(u - (m_new - cq))
    ps = jnp.sum(p, axis=-1, keepdims=True)
    pv = jnp.dot(p.astype(bf16), v, preferred_element_type=f32)
    if carry is None:
        return m_new, ps, pv
    alpha = jnp.exp2(m - m_new)
    return m_new, alpha * l + ps, alpha * acc + pv


def _fox_kernel(q_ref, k_ref, v_ref, km_ref, vm_ref, ck_ref, ckm_ref, o_ref):
    hg = pl.program_id(1)
    i = pl.program_id(2)
    t = ATTN_TILE
    q0 = pl.multiple_of(i * t, t)
    heads = range(FOX_GROUP)
    col = lambda g: slice(g * HEAD_DIM, (g + 1) * HEAD_DIM)
    ck_row = lambda g, k0: ck_ref[pl.ds(hg * FOX_GROUP + g, 1), pl.ds(k0, t)]
    qs = [q_ref[:, col(g)] for g in heads]
    cqs = [_row_to_col(ck_row(g, q0)) for g in heads]

    meta_ok = lax.broadcasted_iota(jnp.int32, (t, N_META_PAD), 1) < N_META
    us = [_qk(qs[g], km_ref[:, col(g)]) - ckm_ref[pl.ds(hg * FOX_GROUP + g, 1), :] for g in heads]
    carry = tuple(_fox_update(None, jnp.where(meta_ok, us[g], NEG_INF), cqs[g], vm_ref[:, col(g)]) for g in heads)

    def body(j, carry):
        k0 = pl.multiple_of(j * t, t)
        us = [_qk(qs[g], k_ref[pl.ds(k0, t), col(g)]) - ck_row(g, k0) for g in heads]
        return tuple(_fox_update(carry[g], us[g], cqs[g], v_ref[pl.ds(k0, t), col(g)]) for g in heads)

    carry = lax.fori_loop(0, i, body, carry)

    r = lax.broadcasted_iota(jnp.int32, (t, t), 0)
    c = lax.broadcasted_iota(jnp.int32, (t, t), 1)
    us = [_qk(qs[g], k_ref[pl.ds(q0, t), col(g)]) - ck_row(g, q0) for g in heads]
    for g in heads:
        _, l, acc = _fox_update(carry[g], jnp.where(c <= r, us[g], NEG_INF), cqs[g], v_ref[pl.ds(q0, t), col(g)])
        o_ref[:, col(g)] = (acc / l).astype(bf16)


def _fox_attention(proj, proj_meta, ck, ckm):
    t = ATTN_TILE
    w = FOX_GROUP * HEAD_DIM
    cb = lambda col: col // w
    return pl.pallas_call(
        _fox_kernel,
        grid=(BATCH, FOX_HEADS // FOX_GROUP, N_QBLK),
        in_specs=[
            pl.BlockSpec((t, w), lambda b, h, i: (b * N_QBLK + i, cb(COL_FQ) + h)),
            pl.BlockSpec((SEQ, w), lambda b, h, i: (b, cb(COL_FK) + h)),
            pl.BlockSpec((SEQ, w), lambda b, h, i: (b, cb(COL_FV) + h)),
            pl.BlockSpec((N_META_PAD, w), lambda b, h, i: (0, cb(COL_FK) + h)),
            pl.BlockSpec((N_META_PAD, w), lambda b, h, i: (0, cb(COL_FV) + h)),
            pl.BlockSpec((None, FOX_HEADS, SEQ), lambda b, h, i: (b, 0, 0)),
            pl.BlockSpec((None, FOX_HEADS, LANES), lambda b, h, i: (b, 0, 0)),
        ],
        out_specs=pl.BlockSpec((t, w), lambda b, h, i: (b * N_QBLK + i, h)),
        out_shape=jax.ShapeDtypeStruct((N_TOK, FOX_WIDTH), bf16),
        compiler_params=_params(("parallel", "parallel", "arbitrary")),
        name="fox_attention",
    )(proj, proj, proj, proj_meta, proj_meta, ck, ckm)


def _t5_bias(dist, table_ref, h):
    n = jnp.maximum(dist, 0)
    max_exact = N_BUCKETS // 2
    log_part = jnp.log(jnp.maximum(n, 1).astype(f32) / max_exact) / math.log(MAX_DISTANCE / max_exact)
    v = log_part * (N_BUCKETS - max_exact)
    far = table_ref[N_BUCKETS - 1, h]
    val = lambda b: (table_ref[b, h] - far) * LOG2E
    large = jnp.zeros(dist.shape, f32)
    for b in range(N_BUCKETS - 2, max_exact - 1, -1):
        large = jnp.where(v < b + 1 - max_exact, val(b), large)
    out = large
    for b in range(max_exact):
        out = jnp.where(n == b, val(b), out)
    return out


def _bias_kernel(table_ref, near_ref, meta_ref):
    h = pl.program_id(0)
    t = ATTN_TILE
    r = lax.broadcasted_iota(jnp.int32, (t, 2 * t), 0)
    c = lax.broadcasted_iota(jnp.int32, (t, 2 * t), 1)
    near_ref[...] = _t5_bias(r + t - c, table_ref, h)
    r = lax.broadcasted_iota(jnp.int32, (t, LANES), 0)
    c = lax.broadcasted_iota(jnp.int32, (t, LANES), 1)
    meta_ref[...] = _t5_bias(N_META + r - c, table_ref, h)


def _bias_tiles(table):
    t = ATTN_TILE
    return pl.pallas_call(
        _bias_kernel,
        grid=(DIFF_HEADS,),
        in_specs=[pl.BlockSpec(memory_space=pltpu.SMEM)],
        out_specs=[
            pl.BlockSpec((None, t, 2 * t), lambda h: (h, 0, 0)),
            pl.BlockSpec((None, t, LANES), lambda h: (h, 0, 0)),
        ],
        out_shape=[
            jax.ShapeDtypeStruct((DIFF_HEADS, t, 2 * t), f32),
            jax.ShapeDtypeStruct((DIFF_HEADS, t, LANES), f32),
        ],
        compiler_params=_params(("arbitrary",)),
        name="t5_bias_tiles",
    )(table)


def _diff_kernel(q1_ref, q2_ref, k1_ref, k2_ref, v_ref, k1m_ref, k2m_ref, vm_ref,
                 near_ref, mbias_ref, lam_ref, subln_ref, o_ref):
    i = pl.program_id(2)
    t = ATTN_TILE
    q1 = q1_ref[...]
    q2 = q2_ref[...]
    q0 = pl.multiple_of(i * t, t)
    lam = (jnp.exp(jnp.sum(lam_ref[0:1, :] * lam_ref[1:2, :], axis=-1, keepdims=True))
           - jnp.exp(jnp.sum(lam_ref[2:3, :] * lam_ref[3:4, :], axis=-1, keepdims=True))
           + LAMBDA_INIT)

    mb = jnp.where(i == 0, mbias_ref[...], 0.0)
    meta_ok = lax.broadcasted_iota(jnp.int32, (t, N_META_PAD), 1) < N_META
    vm = vm_ref[...]
    c1 = _first_update(jnp.where(meta_ok, _qk(q1, k1m_ref[...]) + mb, NEG_INF), vm)
    c2 = _first_update(jnp.where(meta_ok, _qk(q2, k2m_ref[...]) + mb, NEG_INF), vm)

    def far_body(j, carry):
        c1, c2 = carry
        k0 = pl.multiple_of(j * t, t)
        v = v_ref[pl.ds(k0, t), :]
        c1 = _online_update(c1, _qk(q1, k1_ref[pl.ds(k0, t), :]), v)
        c2 = _online_update(c2, _qk(q2, k2_ref[pl.ds(k0, t), :]), v)
        return c1, c2

    c1, c2 = lax.fori_loop(0, jnp.maximum(i - 1, 0), far_body, (c1, c2))

    def prev_block(carry):
        c1, c2 = carry
        k0 = pl.multiple_of((i - 1) * t, t)
        v = v_ref[pl.ds(k0, t), :]
        bias = near_ref[:, :t]
        c1 = _online_update(c1, _qk(q1, k1_ref[pl.ds(k0, t), :]) + bias, v)
        c2 = _online_update(c2, _qk(q2, k2_ref[pl.ds(k0, t), :]) + bias, v)
        return c1, c2

    c1, c2 = lax.cond(i > 0, prev_block, lambda carry: carry, (c1, c2))

    r = lax.broadcasted_iota(jnp.int32, (t, t), 0)
    c = lax.broadcasted_iota(jnp.int32, (t, t), 1)
    mask = c <= r
    bias = near_ref[:, t:]
    v = v_ref[pl.ds(q0, t), :]
    s1 = jnp.where(mask, _qk(q1, k1_ref[pl.ds(q0, t), :]) + bias, NEG_INF)
    s2 = jnp.where(mask, _qk(q2, k2_ref[pl.ds(q0, t), :]) + bias, NEG_INF)
    _, l1, a1 = _online_update(c1, s1, v)
    _, l2, a2 = _online_update(c2, s2, v)

    o = a1 / l1 - lam * (a2 / l2)
    y = o * lax.rsqrt(jnp.mean(o * o, axis=-1, keepdims=True) + RMS_EPS) * subln_ref[...]
    o_ref[...] = (y * (1.0 - LAMBDA_INIT)).astype(bf16)


def _diff_attention(proj, proj_meta, near, mbias, lam_vecs, subln):
    t = ATTN_TILE
    cb = lambda col: col // HEAD_DIM
    vb = lambda col: col // DIFF_V_DIM
    row = lambda b, h, i: b * N_QBLK + i
    return pl.pallas_call(
        _diff_kernel,
        grid=(BATCH, DIFF_HEADS, N_QBLK),
        in_specs=[
            pl.BlockSpec((t, HEAD_DIM), lambda b, h, i: (row(b, h, i), cb(COL_DQ) + 2 * h)),
            pl.BlockSpec((t, HEAD_DIM), lambda b, h, i: (row(b, h, i), cb(COL_DQ) + 2 * h + 1)),
            pl.BlockSpec((SEQ, HEAD_DIM), lambda b, h, i: (b, cb(COL_DK) + 2 * h)),
            pl.BlockSpec((SEQ, HEAD_DIM), lambda b, h, i: (b, cb(COL_DK) + 2 * h + 1)),
            pl.BlockSpec((SEQ, DIFF_V_DIM), lambda b, h, i: (b, vb(COL_DV) + h)),
            pl.BlockSpec((N_META_PAD, HEAD_DIM), lambda b, h, i: (0, cb(COL_DK) + 2 * h)),
            pl.BlockSpec((N_META_PAD, HEAD_DIM), lambda b, h, i: (0, cb(COL_DK) + 2 * h + 1)),
            pl.BlockSpec((N_META_PAD, DIFF_V_DIM), lambda b, h, i: (0, vb(COL_DV) + h)),
            pl.BlockSpec((None, t, 2 * t), lambda b, h, i: (h, 0, 0)),
            pl.BlockSpec((None, t, LANES), lambda b, h, i: (h, 0, 0)),
            pl.BlockSpec((4, HEAD_DIM), lambda b, h, i: (0, 0)),
            pl.BlockSpec((1, DIFF_V_DIM), lambda b, h, i: (0, 0)),
        ],
        out_specs=pl.BlockSpec((t, DIFF_V_DIM), lambda b, h, i: (row(b, h, i), h)),
        out_shape=jax.ShapeDtypeStruct((N_TOK, DIFF_WIDTH), bf16),
        compiler_params=_params(("parallel", "parallel", "arbitrary")),
        name="diff_attention",
    )(proj, proj, proj, proj, proj, proj_meta, proj_meta, proj_meta, near, mbias, lam_vecs, subln)


def _merge_kernel(of_ref, od_ref, wf_ref, wd_ref, gf_ref, gd_ref, o_ref):
    yf = jnp.dot(of_ref[...], wf_ref[...], preferred_element_type=f32)
    yd = jnp.dot(od_ref[...], wd_ref[...], preferred_element_type=f32)
    gf = jax.nn.sigmoid(gf_ref[...].astype(f32))
    gd = jax.nn.sigmoid(gd_ref[...].astype(f32))
    o_ref[...] = (gf * yf + gd * yd).astype(bf16)


def _gated_merge(o_fox, o_diff, w_bf, w_bd, proj, tm=1024, tn=512):
    return pl.pallas_call(
        _merge_kernel,
        grid=(D_MODEL // tn, N_TOK // tm),
        in_specs=[
            pl.BlockSpec((tm, FOX_WIDTH), lambda j, i: (i, 0)),
            pl.BlockSpec((tm, DIFF_WIDTH), lambda j, i: (i, 0)),
            pl.BlockSpec((FOX_WIDTH, tn), lambda j, i: (0, j)),
            pl.BlockSpec((DIFF_WIDTH, tn), lambda j, i: (0, j)),
            pl.BlockSpec((tm, tn), lambda j, i: (i, COL_GF // tn + j)),
            pl.BlockSpec((tm, tn), lambda j, i: (i, COL_GD // tn + j)),
        ],
        out_specs=pl.BlockSpec((tm, tn), lambda j, i: (i, j)),
        out_shape=jax.ShapeDtypeStruct((N_TOK, D_MODEL), bf16),
        compiler_params=_params(("parallel", "arbitrary")),
        name="gated_merge",
    )(o_fox, o_diff, w_bf, w_bd, proj, proj)


def _outproj_kernel(m_ref, w_ref, x_ref, g_ref, wr_ref, br_ref, h_ref, hn_ref, gate_ref, idx_ref):
    h1 = x_ref[...] + jnp.dot(m_ref[...], w_ref[...], preferred_element_type=f32)
    h_ref[...] = h1
    hn = h1 * lax.rsqrt(jnp.mean(h1 * h1, axis=-1, keepdims=True) + RMS_EPS) * g_ref[...]
    hn_ref[...] = hn
    hn_hi = hn.astype(bf16)
    hn_lo = (hn - hn_hi.astype(f32)).astype(bf16)
    a = jnp.dot(hn_hi, wr_ref[...], preferred_element_type=f32)
    b = jnp.dot(hn_lo, wr_ref[:, :LANES], preferred_element_type=f32)
    lg = a[:, :LANES] + a[:, LANES:] + b + br_ref[...]

    lane = lax.broadcasted_iota(jnp.int32, lg.shape, 1)
    cur = jnp.where(lane < N_EXPERTS, lg, -jnp.inf)
    vals, idxs = [], []
    for _ in range(TOP_K):
        m = jnp.max(cur, axis=-1, keepdims=True)
        idx = jnp.min(jnp.where(cur == m, lane, LANES), axis=-1, keepdims=True)
        vals.append(m)
        idxs.append(idx)
        cur = jnp.where(lane == idx, -jnp.inf, cur)
    es = [jnp.exp(v - vals[0]) for v in vals]
    den = es[0] + es[1] + es[2] + es[3]
    gate_out = jnp.zeros(lg.shape, f32)
    idx_out = jnp.zeros(lg.shape, jnp.int32)
    for k in range(TOP_K):
        gate_out = jnp.where(lane == k, es[k] / den, gate_out)
        idx_out = jnp.where(lane == k, idxs[k], idx_out)
    gate_ref[...] = gate_out
    idx_ref[...] = idx_out


def _outproj(merged, w_out, x2d, gain, w_router_split, b_router_pad, tm=512):
    return pl.pallas_call(
        _outproj_kernel,
        grid=(N_TOK // tm,),
        in_specs=[
            pl.BlockSpec((tm, D_MODEL), lambda i: (i, 0)),
            pl.BlockSpec((D_MODEL, D_MODEL), lambda i: (0, 0)),
            pl.BlockSpec((tm, D_MODEL), lambda i: (i, 0)),
            pl.BlockSpec((1, D_MODEL), lambda i: (0, 0)),
            pl.BlockSpec((D_MODEL, 2 * LANES), lambda i: (0, 0)),
            pl.BlockSpec((1, LANES), lambda i: (0, 0)),
        ],
        out_specs=[
            pl.BlockSpec((tm, D_MODEL), lambda i: (i, 0)),
            pl.BlockSpec((tm, D_MODEL), lambda i: (i, 0)),
            pl.BlockSpec((tm, LANES), lambda i: (i, 0)),
            pl.BlockSpec((tm, LANES), lambda i: (i, 0)),
        ],
        out_shape=[
            jax.ShapeDtypeStruct((N_TOK, D_MODEL), f32),
            jax.ShapeDtypeStruct((N_TOK, D_MODEL), f32),
            jax.ShapeDtypeStruct((N_TOK, LANES), f32),
            jax.ShapeDtypeStruct((N_TOK, LANES), jnp.int32),
        ],
        compiler_params=_params(("parallel",)),
        name="outproj_router",
    )(merged, w_out, x2d, gain, w_router_split, b_router_pad)


def _moe_kernel(item_e_ref, item_start_ref, item_n_ref, tok_ref, dest_ref,
                hn_hbm, wg_ref, wl_ref, bg_ref, bl_ref, wd_ref, bd_ref, y_hbm,
                acc_ref, xg_ref, xb_ref, wgb_ref, wlb_ref, wdb_ref, gsem, ssem):
    w = pl.program_id(0)
    t = pl.program_id(1)
    last_w = MOE_ITEMS - 1
    n = item_n_ref[w]
    start = item_start_ref[w]
    slot = lax.rem(w, 2)
    w_next = jnp.minimum(w + 1, last_w)
    start_next = item_start_ref[w_next]
    w_prev = jnp.maximum(w - 1, 0)
    start_prev = item_start_ref[w_prev]
    n_prev = jnp.where(w > 0, item_n_ref[w_prev], 0)
    is_last_live = jnp.logical_or(w == last_w, item_n_ref[w_next] == 0)

    def gather_row(row, base):
        tk = tok_ref[base + row]
        pltpu.make_async_copy(hn_hbm.at[pl.ds(tk, 1), :], xg_ref.at[pl.ds(row, 1), :], gsem).start()

    def gather_wait():
        pltpu.make_async_copy(hn_hbm.at[pl.ds(0, MOE_ROWS), :], xg_ref, gsem).wait()

    def scatter_row(row, base, cnt, sl, priority=0):
        d = jnp.where(row < cnt, dest_ref[base + row], N_ASSIGN + row)
        pltpu.make_async_copy(acc_ref.at[sl, pl.ds(row, 1), :], y_hbm.at[pl.ds(d, 1), :],
                              ssem).start(priority=priority)

    def scatter_wait(sl):
        pltpu.make_async_copy(acc_ref.at[sl], y_hbm.at[pl.ds(0, MOE_ROWS), :], ssem).wait()

    @pl.when(jnp.logical_and(w == 0, t == 0))
    def _():
        acc_ref[1] = jnp.zeros((MOE_ROWS, D_MODEL), f32)

        def issue(r, c):
            gather_row(r, start)
            return c
        lax.fori_loop(0, MOE_ROWS, issue, 0, unroll=8)

    @pl.when(jnp.logical_and(t == 0, n > 0))
    def _():
        gather_wait()
        xb_ref[...] = xg_ref[...].astype(bf16)
        acc_ref[slot] = jnp.broadcast_to(bd_ref[...], (MOE_ROWS, D_MODEL))

    def dma_chunk(r):
        row0 = t * (MOE_NSUB * MOE_CHUNK) + r * MOE_CHUNK
        for k in range(MOE_CHUNK):
            gather_row(row0 + k, start_next)
        for k in range(MOE_CHUNK):
            scatter_row(row0 + k, start_prev, n_prev, 1 - slot, priority=k % 2)

    def ffn_up(r0, rows):
        xs = xb_ref[r0:r0 + rows, :]
        hg = jnp.dot(xs, wgb_ref[...], preferred_element_type=f32) + bg_ref[...]
        hl = jnp.dot(xs, wlb_ref[...], preferred_element_type=f32) + bl_ref[...]
        return hg, hl

    def ffn_down(r0, rows, hg, hl):
        hg = jnp.minimum(hg, SWIGLU_LIMIT)
        hl = jnp.clip(hl, -SWIGLU_LIMIT, SWIGLU_LIMIT)
        act = hg * jax.nn.sigmoid(SWIGLU_ALPHA * hg) * (hl + 1.0)
        acc_ref[slot, r0:r0 + rows, :] += jnp.dot(act.astype(bf16), wdb_ref[...], preferred_element_type=f32)

    def ffn_rows(r0, rows):
        ffn_down(r0, rows, *ffn_up(r0, rows))

    @pl.when(n > 0)
    def _():
        nb_full = n // MOE_SUB
        rem = n - nb_full * MOE_SUB
        ext = jnp.logical_and(jnp.logical_and(rem > 0, rem <= MOE_HALF), nb_full >= 1)
        own_block = jnp.logical_or(rem > MOE_HALF, jnp.logical_and(rem > 0, nb_full == 0))
        nb = nb_full + jnp.where(own_block, 1, 0)
        tiny = jnp.logical_and(nb_full == 0, rem <= MOE_HALF)

        def cast_weights(r):
            if r == 0:
                wgb_ref[...] = wg_ref[...].astype(bf16)
                wlb_ref[...] = wl_ref[...].astype(bf16)
                wdb_ref[...] = wd_ref[...].astype(bf16)

        def single(r, rows, chunks=1):
            def body():
                for c in range(chunks):
                    dma_chunk(r + c)
                cast_weights(r)
                ffn_rows(r * MOE_SUB, rows)
            return body

        def pair(r, rows_b):
            def body():
                dma_chunk(r)
                dma_chunk(r + 1)
                cast_weights(r)
                ra, rb = r * MOE_SUB, (r + 1) * MOE_SUB
                ha = ffn_up(ra, MOE_SUB)
                hb = ffn_up(rb, rows_b)
                ffn_down(ra, MOE_SUB, *ha)
                ffn_down(rb, rows_b, *hb)
            return body

        def idle(*rs):
            def body():
                for r in rs:
                    dma_chunk(r)
            return body

        def last_single(r):
            return lambda: lax.cond(
                ext, single(r, MOE_SUB + MOE_HALF, 2),
                lambda: lax.cond(tiny, single(r, MOE_HALF, 2), single(r, MOE_SUB, 2)))

        for r in range(0, MOE_NSUB - 1, 2):
            is_last_pair = jnp.logical_and(ext, nb == r + 2)
            lax.cond(
                nb >= r + 2,
                lambda r=r, is_last_pair=is_last_pair: lax.cond(
                    is_last_pair, pair(r, MOE_SUB + MOE_HALF), pair(r, MOE_SUB)),
                lambda r=r: lax.cond(nb == r + 1, last_single(r), idle(r, r + 1)))
        for r in range(MOE_NSUB - MOE_NSUB % 2, MOE_NSUB):
            lax.cond(nb == r + 1, single(r, MOE_SUB), idle(r))

    @pl.when(jnp.logical_and(t == MOE_NF - 1, n > 0))
    def _():
        scatter_wait(1 - slot)

        @pl.when(is_last_live)
        def _():
            def issue(r, c):
                scatter_row(r, start, n, slot)
                return c
            lax.fori_loop(0, MOE_ROWS, issue, 0, unroll=8)
            scatter_wait(slot)
            gather_wait()


def _moe(item_e, item_start, item_n, tok_sorted, dest_sorted, hn2, w_gate_up, b_gate_up, w_down, b_down):
    tf = MOE_TF
    live_t = lambda w, t, n: jnp.where(n[w] > 0, t, MOE_NF - 1)
    grid_spec = pltpu.PrefetchScalarGridSpec(
        num_scalar_prefetch=5,
        grid=(MOE_ITEMS, MOE_NF),
        in_specs=[
            pl.BlockSpec(memory_space=pl.ANY),
            pl.BlockSpec((None, D_MODEL, tf), lambda w, t, e, s, n, tk, ds: (e[w], 0, live_t(w, t, n))),
            pl.BlockSpec((None, D_MODEL, tf), lambda w, t, e, s, n, tk, ds: (e[w], 0, MOE_NF + live_t(w, t, n))),
            pl.BlockSpec((None, 1, tf), lambda w, t, e, s, n, tk, ds: (e[w], 0, live_t(w, t, n))),
            pl.BlockSpec((None, 1, tf), lambda w, t, e, s, n, tk, ds: (e[w], 0, MOE_NF + live_t(w, t, n))),
            pl.BlockSpec((None, tf, D_MODEL), lambda w, t, e, s, n, tk, ds: (e[w], live_t(w, t, n), 0)),
            pl.BlockSpec((None, 1, D_MODEL), lambda w, t, e, s, n, tk, ds: (e[w], 0, 0)),
        ],
        out_specs=pl.BlockSpec(memory_space=pl.ANY),
        scratch_shapes=[
            pltpu.VMEM((2, MOE_ROWS, D_MODEL), f32),
            pltpu.VMEM((MOE_ROWS, D_MODEL), f32),
            pltpu.VMEM((MOE_ROWS, D_MODEL), bf16),
            pltpu.VMEM((D_MODEL, tf), bf16),
            pltpu.VMEM((D_MODEL, tf), bf16),
            pltpu.VMEM((tf, D_MODEL), bf16),
            pltpu.SemaphoreType.DMA,
            pltpu.SemaphoreType.DMA,
        ],
    )
    return pl.pallas_call(
        _moe_kernel,
        grid_spec=grid_spec,
        out_shape=jax.ShapeDtypeStruct((Y_ROWS, D_MODEL), f32),
        compiler_params=_params(("arbitrary", "arbitrary"), BIG_VMEM_LIMIT_BYTES),
        name="moe_experts",
    )(item_e, item_start, item_n, tok_sorted, dest_sorted,
      hn2, w_gate_up, w_gate_up, b_gate_up, b_gate_up, w_down, b_down)


def _combine_kernel(y0_ref, y1_ref, y2_ref, y3_ref, h_ref, gate_ref, g_ref, o_ref):
    h = h_ref[...]
    gates = gate_ref[...]
    for k, y_ref in enumerate((y0_ref, y1_ref, y2_ref, y3_ref)):
        h = h + y_ref[...] * gates[:, k:k + 1]
    o_ref[...] = h * lax.rsqrt(jnp.mean(h * h, axis=-1, keepdims=True) + RMS_EPS) * g_ref[...]


def _combine(y, h1, gates, gain, tm=256):
    slot_spec = lambda k: pl.BlockSpec((tm, D_MODEL), lambda i, k=k: (k * (N_TOK // tm) + i, 0))
    return pl.pallas_call(
        _combine_kernel,
        grid=(N_TOK // tm,),
        in_specs=[slot_spec(k) for k in range(TOP_K)] + [
            pl.BlockSpec((tm, D_MODEL), lambda i: (i, 0)),
            pl.BlockSpec((tm, TOP_K), lambda i: (i, 0)),
            pl.BlockSpec((1, D_MODEL), lambda i: (0, 0)),
        ],
        out_specs=pl.BlockSpec((tm, D_MODEL), lambda i: (i, 0)),
        out_shape=jax.ShapeDtypeStruct((N_TOK, D_MODEL), f32),
        compiler_params=_params(("parallel",)),
        name="combine_norm",
    )(y, y, y, y, h1, gates, gain)


def _route(top_idx):
    expert_flat = top_idx.reshape(-1)
    order = jnp.argsort(expert_flat, stable=True).astype(jnp.int32)
    counts = jnp.zeros((N_EXPERTS,), jnp.int32).at[expert_flat].add(1)
    starts = jnp.cumsum(counts) - counts
    tok_sorted = order // TOP_K
    dest_sorted = (order % TOP_K) * N_TOK + tok_sorted
    tok_sorted = jnp.pad(tok_sorted, (0, MOE_ROWS))
    dest_sorted = jnp.pad(dest_sorted, (0, MOE_ROWS))
    chunks = (counts + MOE_ROWS - 1) // MOE_ROWS
    chunk_end = jnp.cumsum(chunks)
    n_items = chunk_end[-1]
    w = jnp.arange(MOE_ITEMS, dtype=jnp.int32)
    w_eff = jnp.minimum(w, n_items - 1)
    e = jnp.sum(w_eff[:, None] >= chunk_end[None, :], axis=1).astype(jnp.int32)
    c = w_eff - (chunk_end - chunks)[e]
    item_start = starts[e] + c * MOE_ROWS
    item_n = jnp.where(w < n_items, jnp.clip(counts[e] - c * MOE_ROWS, 0, MOE_ROWS), 0)
    return e, item_start.astype(jnp.int32), item_n.astype(jnp.int32), tok_sorted, dest_sorted


def kernel(x, meta_tokens, rel_bias_table, attn_norm, w_in, fox_forget_bias, lam_q1, lam_k1, lam_q2, lam_k2,
           diff_subln, w_branch_fox, w_branch_diff, w_out, ffn_norm, w_router, b_router, w_gate_up, b_gate_up,
           w_down, b_down, final_norm):
    x2d = x.reshape(N_TOK, D_MODEL)
    w_t = jnp.swapaxes(w_in[0], 0, 1)
    fb_pad = jnp.pad(fox_forget_bias[0], (0, LANES - FOX_HEADS)).reshape(1, LANES)

    meta_pad = jnp.pad(meta_tokens, ((0, N_META_PAD - N_META), (0, 0)))
    proj, proj_meta, ff_real, ff_meta = _inproj(x2d, meta_pad, attn_norm, w_t, tm=1024, tn=512)

    ck, ckm = _forget_cumsum(ff_real, ff_meta, fb_pad)
    o_fox = _fox_attention(proj, proj_meta, ck, ckm)

    near, mbias = _bias_tiles(rel_bias_table)
    lam_vecs = jnp.concatenate([lam_q1, lam_k1, lam_q2, lam_k2], axis=0)
    o_diff = _diff_attention(proj, proj_meta, near, mbias, lam_vecs, diff_subln)

    merged = _gated_merge(o_fox, o_diff, w_branch_fox[0].astype(bf16), w_branch_diff[0].astype(bf16), proj)
    w_router_pad = jnp.pad(w_router[0], ((0, 0), (0, LANES - N_EXPERTS)))
    w_router_hi = w_router_pad.astype(bf16)
    w_router_lo = (w_router_pad - w_router_hi.astype(f32)).astype(bf16)
    w_router_split = jnp.concatenate([w_router_hi, w_router_lo], axis=1)
    b_router_pad = jnp.pad(b_router[0], (0, LANES - N_EXPERTS)).reshape(1, LANES)
    h1, hn2, gates_pad, idx_pad = _outproj(merged, w_out[0].astype(bf16), x2d, ffn_norm, w_router_split, b_router_pad)
    gates = gates_pad[:, :TOP_K]

    item_e, item_start, item_n, tok_sorted, dest_sorted = _route(idx_pad[:, :TOP_K])
    y = _moe(item_e, item_start, item_n, tok_sorted, dest_sorted, hn2,
             w_gate_up[0], b_gate_up[0].reshape(N_EXPERTS, 1, 2 * D_EXPERT),
             w_down[0], b_down[0].reshape(N_EXPERTS, 1, D_MODEL))
    out = _combine(y, h1, gates, final_norm.reshape(1, D_MODEL))
    return out.reshape(BATCH, SEQ, D_MODEL)
```

```python
import functools
import math

import jax
import jax.numpy as jnp
from jax import lax
from jax.experimental import pallas as pl
from jax.experimental.pallas import tpu as pltpu

D_MODEL = 2048
BATCH = 4
SEQ = 2048
N_TOK = BATCH * SEQ
N_META = 16
N_META_PAD = 128
HEAD_DIM = 128
FOX_HEADS = 8
DIFF_HEADS = 4
DIFF_V_DIM = 2 * HEAD_DIM
FOX_WIDTH = FOX_HEADS * HEAD_DIM
DIFF_QK_WIDTH = DIFF_HEADS * 2 * HEAD_DIM
DIFF_WIDTH = DIFF_HEADS * DIFF_V_DIM
N_BUCKETS = 32
MAX_DISTANCE = 128
N_EXPERTS = 32
TOP_K = 4
D_EXPERT = D_MODEL
SWIGLU_LIMIT = 7.0
SWIGLU_ALPHA = 1.702
RMS_EPS = 1e-5
NEG_INF = -1e30
LAMBDA_INIT = 0.8 - 0.6 * math.exp(-0.3 * 0)
ATTN_SCALE = HEAD_DIM ** -0.5
LOG2E = math.log2(math.e)
Q_PRESCALE = ATTN_SCALE * LOG2E

LANES = 128
SUBLANES = 8
VMEM_LIMIT_BYTES = 56 * 1024 * 1024
BIG_VMEM_LIMIT_BYTES = 60 * 1024 * 1024

COL_FQ = 0
COL_FK = COL_FQ + FOX_WIDTH
COL_FV = COL_FK + FOX_WIDTH
COL_DQ = COL_FV + FOX_WIDTH
COL_DK = COL_DQ + DIFF_QK_WIDTH
COL_DV = COL_DK + DIFF_QK_WIDTH
COL_GF = COL_DV + DIFF_WIDTH
COL_GD = COL_GF + D_MODEL
D_PROJ = COL_GD + D_MODEL

INPROJ_XT = 256
ATTN_TILE = 512
N_QBLK = SEQ // ATTN_TILE

MOE_ROWS = 1280
MOE_SUB = 256
MOE_HALF = MOE_SUB // 2
MOE_NSUB = MOE_ROWS // MOE_SUB
MOE_TF = 256
MOE_NF = D_EXPERT // MOE_TF
MOE_CHUNK = MOE_ROWS // (MOE_NF * MOE_NSUB)
N_ASSIGN = N_TOK * TOP_K
MOE_ITEMS = N_EXPERTS + N_ASSIGN // MOE_ROWS
Y_ROWS = N_ASSIGN + MOE_ROWS

f32 = jnp.float32
bf16 = jnp.bfloat16


def _params(sem, vmem=VMEM_LIMIT_BYTES):
    return pltpu.CompilerParams(dimension_semantics=sem, vmem_limit_bytes=vmem)


def _rms_bf16(x, g):
    return (x * lax.rsqrt(jnp.mean(x * x, axis=-1, keepdims=True) + RMS_EPS) * g).astype(bf16)


def _inproj_kernel(x_hbm, xm_ref, g_ref, wt_ref, wfft_ref, o_ref, om_ref, ff_ref, ffm_ref,
                   xn_ref, xnm_ref, xt_ref, wb_ref, wffb_ref, sem, *, tm, tn):
    j = pl.program_id(0)
    i = pl.program_id(1)
    rows = pl.ds(pl.multiple_of(i * tm, tm), tm)
    is_q = jnp.logical_or(j < COL_FK // tn, jnp.logical_and(j >= COL_DQ // tn, j < COL_DK // tn))
    qs = jnp.where(is_q, Q_PRESCALE, 1.0)

    @pl.when(jnp.logical_and(j == 0, i == 0))
    def _():
        wffb_ref[...] = wfft_ref[...].T.astype(bf16)
        xnm_ref[...] = _rms_bf16(xm_ref[...], g_ref[...])
        ffm_ref[...] = jnp.dot(xnm_ref[...], wffb_ref[...], preferred_element_type=f32)

    @pl.when(i == 0)
    def _():
        wb_ref[...] = wt_ref[...].T.astype(bf16)
        ym = jnp.dot(xnm_ref[...], wb_ref[...], preferred_element_type=f32)
        om_ref[...] = (ym * qs).astype(bf16)

    @pl.when(j == 0)
    def _():
        def piece(s):
            r0 = pl.multiple_of(i * tm + s * INPROJ_XT, INPROJ_XT)
            return pltpu.make_async_copy(x_hbm.at[pl.ds(r0, INPROJ_XT), :], xt_ref.at[s % 2], sem.at[s % 2])

        n_pieces = tm // INPROJ_XT
        piece(0).start()
        for s in range(n_pieces):
            if s + 1 < n_pieces:
                piece(s + 1).start()
            piece(s).wait()
            xn = _rms_bf16(xt_ref[s % 2], g_ref[...])
            xn_ref[pl.ds(pl.multiple_of(i * tm + s * INPROJ_XT, INPROJ_XT), INPROJ_XT), :] = xn
            ff_ref[s * INPROJ_XT:(s + 1) * INPROJ_XT, :] = jnp.dot(xn, wffb_ref[...], preferred_element_type=f32)

    y = jnp.dot(xn_ref[rows, :], wb_ref[...], preferred_element_type=f32)
    o_ref[...] = (y * qs).astype(bf16)


def _inproj(x2d, meta_pad, gain, w_t, tm, tn):
    n = x2d.shape[0]
    n_i = n // tm
    w_row = lambda j: SUBLANES * (j * (tn // SUBLANES) + jnp.where(j * tn >= COL_DQ, FOX_HEADS // SUBLANES, 0))
    first_pass_row = lambda j, i: jnp.where(j == 0, i, n_i - 1)
    return pl.pallas_call(
        functools.partial(_inproj_kernel, tm=tm, tn=tn),
        grid=(D_PROJ // tn, n_i),
        in_specs=[
            pl.BlockSpec(memory_space=pl.ANY),
            pl.BlockSpec((N_META_PAD, D_MODEL), lambda j, i: (0, 0)),
            pl.BlockSpec((1, D_MODEL), lambda j, i: (0, 0)),
            pl.BlockSpec((pl.Element(tn), pl.Element(D_MODEL)), lambda j, i: (w_row(j), 0)),
            pl.BlockSpec((pl.Element(LANES), pl.Element(D_MODEL)), lambda j, i: (COL_DQ, 0)),
        ],
        out_specs=[
            pl.BlockSpec((tm, tn), lambda j, i: (i, j)),
            pl.BlockSpec((N_META_PAD, tn), lambda j, i: (0, j)),
            pl.BlockSpec((tm, LANES), lambda j, i: (first_pass_row(j, i), 0)),
            pl.BlockSpec((N_META_PAD, LANES), lambda j, i: (0, 0)),
        ],
        out_shape=[
            jax.ShapeDtypeStruct((n, D_PROJ), bf16),
            jax.ShapeDtypeStruct((N_META_PAD, D_PROJ), bf16),
            jax.ShapeDtypeStruct((n, LANES), f32),
            jax.ShapeDtypeStruct((N_META_PAD, LANES), f32),
        ],
        scratch_shapes=[
            pltpu.VMEM((n, D_MODEL), bf16),
            pltpu.VMEM((N_META_PAD, D_MODEL), bf16),
            pltpu.VMEM((2, INPROJ_XT, D_MODEL), f32),
            pltpu.VMEM((D_MODEL, tn), bf16),
            pltpu.VMEM((D_MODEL, LANES), bf16),
            pltpu.SemaphoreType.DMA((2,)),
        ],
        compiler_params=_params(("arbitrary", "arbitrary"), BIG_VMEM_LIMIT_BYTES),
        name="inproj",
    )(x2d, meta_pad, gain, w_t, w_t)


def _log_sigmoid(x):
    return jnp.minimum(x, 0.0) - jnp.log(1.0 + jnp.exp(-jnp.abs(x)))


def _lane_cumsum(x):
    n = x.shape[-1]
    lane = lax.broadcasted_iota(jnp.int32, x.shape, x.ndim - 1)
    s = 1
    while s < n:
        x = x + jnp.where(lane >= s, pltpu.roll(x, s, x.ndim - 1), 0.0)
        s *= 2
    return x


def _cum_kernel(ff_ref, ffm_ref, fb_ref, ck_ref, ckm_ref):
    fb = fb_ref[...]
    row = lax.broadcasted_iota(jnp.int32, (LANES, LANES), 0)
    lfm = jnp.where(row < N_META, _log_sigmoid(ffm_ref[...] + fb), 0.0)
    cm = _lane_cumsum(lfm.T)
    m_total = cm[:, N_META - 1:N_META]
    lf = _log_sigmoid(ff_ref[...] + fb)
    cr = _lane_cumsum(lf.T) + m_total
    ck_ref[...] = cr[:SUBLANES] * LOG2E
    ckm_ref[...] = cm[:SUBLANES] * LOG2E


def _forget_cumsum(ff_real, ff_meta_pad, fb_pad):
    return pl.pallas_call(
        _cum_kernel,
        grid=(BATCH,),
        in_specs=[
            pl.BlockSpec((SEQ, LANES), lambda b: (b, 0)),
            pl.BlockSpec((LANES, LANES), lambda b: (0, 0)),
            pl.BlockSpec((1, LANES), lambda b: (0, 0)),
        ],
        out_specs=[
            pl.BlockSpec((None, SUBLANES, SEQ), lambda b: (b, 0, 0)),
            pl.BlockSpec((None, SUBLANES, LANES), lambda b: (b, 0, 0)),
        ],
        out_shape=[
            jax.ShapeDtypeStruct((BATCH, FOX_HEADS, SEQ), f32),
            jax.ShapeDtypeStruct((BATCH, FOX_HEADS, LANES), f32),
        ],
        compiler_params=_params(("parallel",)),
        name="forget_cumsum",
    )(ff_real, ff_meta_pad, fb_pad)


def _qk(q, k):
    return lax.dot_general(q, k, (((1,), (1,)), ((), ())), preferred_element_type=f32)


def _row_to_col(row):
    t = row.shape[-1]
    r = lax.broadcasted_iota(jnp.int32, (t, t), 0)
    c = lax.broadcasted_iota(jnp.int32, (t, t), 1)
    return jnp.sum(jnp.where(r == c, row, 0.0), axis=1, keepdims=True)


def _online_update(carry, s, v):
    m, l, acc = carry
    m_new = jnp.maximum(m, jnp.max(s, axis=-1, keepdims=True))
    alpha = jnp.exp2(m - m_new)
    p = jnp.exp2(s - m_new)
    l = alpha * l + jnp.sum(p, axis=-1, keepdims=True)
    acc = alpha * acc + jnp.dot(p.astype(bf16), v, preferred_element_type=f32)
    return m_new, l, acc


def _first_update(s, v):
    m = jnp.max(s, axis=-1, keepdims=True)
    p = jnp.exp2(s - m)
    l = jnp.sum(p, axis=-1, keepdims=True)
    acc = jnp.dot(p.astype(bf16), v, preferred_element_type=f32)
    return m, l, acc


FOX_GROUP = 2


def _fox_update(carry, u, cq, v):
    m_blk = jnp.max(u, axis=-1, keepdims=True) + cq
    if carry is None:
        m_new = m_blk
    else:
        m, l, acc = carry
        m_new = jnp.maximum(m, m_blk)
    p = jnp.exp2(u - (m_new - cq))
    ps = jnp.sum(p, axis=-1, keepdims=True)
    pv = jnp.dot(p.astype(bf16), v, preferred_element_type=f32)
    if carry is None:
        return m_new, ps, pv
    alpha = jnp.exp2(m - m_new)
    return m_new, alpha * l + ps, alpha * acc + pv


def _fox_kernel(q_ref, k_ref, v_ref, km_ref, vm_ref, ck_ref, ckm_ref, o_ref):
    hg = pl.program_id(1)
    i = pl.program_id(2)
    t = ATTN_TILE
    q0 = pl.multiple_of(i * t, t)
    heads = range(FOX_GROUP)
    col = lambda g: slice(g * HEAD_DIM, (g + 1) * HEAD_DIM)
    ck_row = lambda g, k0: ck_ref[pl.ds(hg * FOX_GROUP + g, 1), pl.ds(k0, t)]
    qs = [q_ref[:, col(g)] for g in heads]
    cqs = [_row_to_col(ck_row(g, q0)) for g in heads]

    meta_ok = lax.broadcasted_iota(jnp.int32, (t, N_META_PAD), 1) < N_META
    us = [_qk(qs[g], km_ref[:, col(g)]) - ckm_ref[pl.ds(hg * FOX_GROUP + g, 1), :] for g in heads]
    carry = tuple(_fox_update(None, jnp.where(meta_ok, us[g], NEG_INF), cqs[g], vm_ref[:, col(g)]) for g in heads)

    def body(j, carry):
        k0 = pl.multiple_of(j * t, t)
        us = [_qk(qs[g], k_ref[pl.ds(k0, t), col(g)]) - ck_row(g, k0) for g in heads]
        return tuple(_fox_update(carry[g], us[g], cqs[g], v_ref[pl.ds(k0, t), col(g)]) for g in heads)

    carry = lax.fori_loop(0, i, body, carry)

    r = lax.broadcasted_iota(jnp.int32, (t, t), 0)
    c = lax.broadcasted_iota(jnp.int32, (t, t), 1)
    us = [_qk(qs[g], k_ref[pl.ds(q0, t), col(g)]) - ck_row(g, q0) for g in heads]
    for g in heads:
        _, l, acc = _fox_update(carry[g], jnp.where(c <= r, us[g], NEG_INF), cqs[g], v_ref[pl.ds(q0, t), col(g)])
        o_ref[:, col(g)] = (acc / l).astype(bf16)


def _fox_attention(proj, proj_meta, ck, ckm):
    t = ATTN_TILE
    w = FOX_GROUP * HEAD_DIM
    cb = lambda col: col // w
    return pl.pallas_call(
        _fox_kernel,
        grid=(BATCH, FOX_HEADS // FOX_GROUP, N_QBLK),
        in_specs=[
            pl.BlockSpec((t, w), lambda b, h, i: (b * N_QBLK + i, cb(COL_FQ) + h)),
            pl.BlockSpec((SEQ, w), lambda b, h, i: (b, cb(COL_FK) + h)),
            pl.BlockSpec((SEQ, w), lambda b, h, i: (b, cb(COL_FV) + h)),
            pl.BlockSpec((N_META_PAD, w), lambda b, h, i: (0, cb(COL_FK) + h)),
            pl.BlockSpec((N_META_PAD, w), lambda b, h, i: (0, cb(COL_FV) + h)),
            pl.BlockSpec((None, FOX_HEADS, SEQ), lambda b, h, i: (b, 0, 0)),
            pl.BlockSpec((None, FOX_HEADS, LANES), lambda b, h, i: (b, 0, 0)),
        ],
        out_specs=pl.BlockSpec((t, w), lambda b, h, i: (b * N_QBLK + i, h)),
        out_shape=jax.ShapeDtypeStruct((N_TOK, FOX_WIDTH), bf16),
        compiler_params=_params(("parallel", "parallel", "arbitrary")),
        name="fox_attention",
    )(proj, proj, proj, proj_meta, proj_meta, ck, ckm)


def _t5_bias(dist, table_ref, h):
    n = jnp.maximum(dist, 0)
    max_exact = N_BUCKETS // 2
    log_part = jnp.log(jnp.maximum(n, 1).astype(f32) / max_exact) / math.log(MAX_DISTANCE / max_exact)
    v = log_part * (N_BUCKETS - max_exact)
    far = table_ref[N_BUCKETS - 1, h]
    val = lambda b: (table_ref[b, h] - far) * LOG2E
    large = jnp.zeros(dist.shape, f32)
    for b in range(N_BUCKETS - 2, max_exact - 1, -1):
        large = jnp.where(v < b + 1 - max_exact, val(b), large)
    out = large
    for b in range(max_exact):
        out = jnp.where(n == b, val(b), out)
    return out


def _bias_kernel(table_ref, near_ref, meta_ref):
    h = pl.program_id(0)
    t = ATTN_TILE
    r = lax.broadcasted_iota(jnp.int32, (t, 2 * t), 0)
    c = lax.broadcasted_iota(jnp.int32, (t, 2 * t), 1)
    near_ref[...] = _t5_bias(r + t - c, table_ref, h)
    r = lax.broadcasted_iota(jnp.int32, (t, LANES), 0)
    c = lax.broadcasted_iota(jnp.int32, (t, LANES), 1)
    meta_ref[...] = _t5_bias(N_META + r - c, table_ref, h)


def _bias_tiles(table):
    t = ATTN_TILE
    return pl.pallas_call(
        _bias_kernel,
        grid=(DIFF_HEADS,),
        in_specs=[pl.BlockSpec(memory_space=pltpu.SMEM)],
        out_specs=[
            pl.BlockSpec((None, t, 2 * t), lambda h: (h, 0, 0)),
            pl.BlockSpec((None, t, LANES), lambda h: (h, 0, 0)),
        ],
        out_shape=[
            jax.ShapeDtypeStruct((DIFF_HEADS, t, 2 * t), f32),
            jax.ShapeDtypeStruct((DIFF_HEADS, t, LANES), f32),
        ],
        compiler_params=_params(("arbitrary",)),
        name="t5_bias_tiles",
    )(table)


def _diff_kernel(q1_ref, q2_ref, k1_ref, k2_ref, v_ref, k1m_ref, k2m_ref, vm_ref,
                 near_ref, mbias_ref, lam_ref, subln_ref, o_ref):
    i = pl.program_id(2)
    t = ATTN_TILE
    q1 = q1_ref[...]
    q2 = q2_ref[...]
    q0 = pl.multiple_of(i * t, t)
    lam = (jnp.exp(jnp.sum(lam_ref[0:1, :] * lam_ref[1:2, :], axis=-1, keepdims=True))
           - jnp.exp(jnp.sum(lam_ref[2:3, :] * lam_ref[3:4, :], axis=-1, keepdims=True))
           + LAMBDA_INIT)

    mb = jnp.where(i == 0, mbias_ref[...], 0.0)
    meta_ok = lax.broadcasted_iota(jnp.int32, (t, N_META_PAD), 1) < N_META
    vm = vm_ref[...]
    c1 = _first_update(jnp.where(meta_ok, _qk(q1, k1m_ref[...]) + mb, NEG_INF), vm)
    c2 = _first_update(jnp.where(meta_ok, _qk(q2, k2m_ref[...]) + mb, NEG_INF), vm)

    def far_body(j, carry):
        c1, c2 = carry
        k0 = pl.multiple_of(j * t, t)
        v = v_ref[pl.ds(k0, t), :]
        c1 = _online_update(c1, _qk(q1, k1_ref[pl.ds(k0, t), :]), v)
        c2 = _online_update(c2, _qk(q2, k2_ref[pl.ds(k0, t), :]), v)
        return c1, c2

    c1, c2 = lax.fori_loop(0, jnp.maximum(i - 1, 0), far_body, (c1, c2))

    def prev_block(carry):
        c1, c2 = carry
        k0 = pl.multiple_of((i - 1) * t, t)
        v = v_ref[pl.ds(k0, t), :]
        bias = near_ref[:, :t]
        c1 = _online_update(c1, _qk(q1, k1_ref[pl.ds(k0, t), :]) + bias, v)
        c2 = _online_update(c2, _qk(q2, k2_ref[pl.ds(k0, t), :]) + bias, v)
        return c1, c2

    c1, c2 = lax.cond(i > 0, prev_block, lambda carry: carry, (c1, c2))

    r = lax.broadcasted_iota(jnp.int32, (t, t), 0)
    c = lax.broadcasted_iota(jnp.int32, (t, t), 1)
    mask = c <= r
    bias = near_ref[:, t:]
    v = v_ref[pl.ds(q0, t), :]
    s1 = jnp.where(mask, _qk(q1, k1_ref[pl.ds(q0, t), :]) + bias, NEG_INF)
    s2 = jnp.where(mask, _qk(q2, k2_ref[pl.ds(q0, t), :]) + bias, NEG_INF)
    _, l1, a1 = _online_update(c1, s1, v)
    _, l2, a2 = _online_update(c2, s2, v)

    o = a1 / l1 - lam * (a2 / l2)
    y = o * lax.rsqrt(jnp.mean(o * o, axis=-1, keepdims=True) + RMS_EPS) * subln_ref[...]
    o_ref[...] = (y * (1.0 - LAMBDA_INIT)).astype(bf16)


def _diff_attention(proj, proj_meta, near, mbias, lam_vecs, subln):
    t = ATTN_TILE
    cb = lambda col: col // HEAD_DIM
    vb = lambda col: col // DIFF_V_DIM
    row = lambda b, h, i: b * N_QBLK + i
    return pl.pallas_call(
        _diff_kernel,
        grid=(BATCH, DIFF_HEADS, N_QBLK),
        in_specs=[
            pl.BlockSpec((t, HEAD_DIM), lambda b, h, i: (row(b, h, i), cb(COL_DQ) + 2 * h)),
            pl.BlockSpec((t, HEAD_DIM), lambda b, h, i: (row(b, h, i), cb(COL_DQ) + 2 * h + 1)),
            pl.BlockSpec((SEQ, HEAD_DIM), lambda b, h, i: (b, cb(COL_DK) + 2 * h)),
            pl.BlockSpec((SEQ, HEAD_DIM), lambda b, h, i: (b, cb(COL_DK) + 2 * h + 1)),
            pl.BlockSpec((SEQ, DIFF_V_DIM), lambda b, h, i: (b, vb(COL_DV) + h)),
            pl.BlockSpec((N_META_PAD, HEAD_DIM), lambda b, h, i: (0, cb(COL_DK) + 2 * h)),
            pl.BlockSpec((N_META_PAD, HEAD_DIM), lambda b, h, i: (0, cb(COL_DK) + 2 * h + 1)),
            pl.BlockSpec((N_META_PAD, DIFF_V_DIM), lambda b, h, i: (0, vb(COL_DV) + h)),
            pl.BlockSpec((None, t, 2 * t), lambda b, h, i: (h, 0, 0)),
            pl.BlockSpec((None, t, LANES), lambda b, h, i: (h, 0, 0)),
            pl.BlockSpec((4, HEAD_DIM), lambda b, h, i: (0, 0)),
            pl.BlockSpec((1, DIFF_V_DIM), lambda b, h, i: (0, 0)),
        ],
        out_specs=pl.BlockSpec((t, DIFF_V_DIM), lambda b, h, i: (row(b, h, i), h)),
        out_shape=jax.ShapeDtypeStruct((N_TOK, DIFF_WIDTH), bf16),
        compiler_params=_params(("parallel", "parallel", "arbitrary")),
        name="diff_attention",
    )(proj, proj, proj, proj, proj, proj_meta, proj_meta, proj_meta, near, mbias, lam_vecs, subln)


def _merge_kernel(of_ref, od_ref, wf_ref, wd_ref, gf_ref, gd_ref, o_ref):
    yf = jnp.dot(of_ref[...], wf_ref[...], preferred_element_type=f32)
    yd = jnp.dot(od_ref[...], wd_ref[...], preferred_element_type=f32)
    gf = jax.nn.sigmoid(gf_ref[...].astype(f32))
    gd = jax.nn.sigmoid(gd_ref[...].astype(f32))
    o_ref[...] = (gf * yf + gd * yd).astype(bf16)


def _gated_merge(o_fox, o_diff, w_bf, w_bd, proj, tm=1024, tn=512):
    return pl.pallas_call(
        _merge_kernel,
        grid=(D_MODEL // tn, N_TOK // tm),
        in_specs=[
            pl.BlockSpec((tm, FOX_WIDTH), lambda j, i: (i, 0)),
            pl.BlockSpec((tm, DIFF_WIDTH), lambda j, i: (i, 0)),
            pl.BlockSpec((FOX_WIDTH, tn), lambda j, i: (0, j)),
            pl.BlockSpec((DIFF_WIDTH, tn), lambda j, i: (0, j)),
            pl.BlockSpec((tm, tn), lambda j, i: (i, COL_GF // tn + j)),
            pl.BlockSpec((tm, tn), lambda j, i: (i, COL_GD // tn + j)),
        ],
        out_specs=pl.BlockSpec((tm, tn), lambda j, i: (i, j)),
        out_shape=jax.ShapeDtypeStruct((N_TOK, D_MODEL), bf16),
        compiler_params=_params(("parallel", "arbitrary")),
        name="gated_merge",
    )(o_fox, o_diff, w_bf, w_bd, proj, proj)


def _outproj_kernel(m_ref, w_ref, x_ref, g_ref, wr_ref, br_ref, h_ref, hn_ref, gate_ref, idx_ref):
    h1 = x_ref[...] + jnp.dot(m_ref[...], w_ref[...], preferred_element_type=f32)
    h_ref[...] = h1
    hn = h1 * lax.rsqrt(jnp.mean(h1 * h1, axis=-1, keepdims=True) + RMS_EPS) * g_ref[...]
    hn_ref[...] = hn
    hn_hi = hn.astype(bf16)
    hn_lo = (hn - hn_hi.astype(f32)).astype(bf16)
    a = jnp.dot(hn_hi, wr_ref[...], preferred_element_type=f32)
    b = jnp.dot(hn_lo, wr_ref[:, :LANES], preferred_element_type=f32)
    lg = a[:, :LANES] + a[:, LANES:] + b + br_ref[...]

    lane = lax.broadcasted_iota(jnp.int32, lg.shape, 1)
    cur = jnp.where(lane < N_EXPERTS, lg, -jnp.inf)
    vals, idxs = [], []
    for _ in range(TOP_K):
        m = jnp.max(cur, axis=-1, keepdims=True)
        idx = jnp.min(jnp.where(cur == m, lane, LANES), axis=-1, keepdims=True)
        vals.append(m)
        idxs.append(idx)
        cur = jnp.where(lane == idx, -jnp.inf, cur)
    es = [jnp.exp(v - vals[0]) for v in vals]
    den = es[0] + es[1] + es[2] + es[3]
    gate_out = jnp.zeros(lg.shape, f32)
    idx_out = jnp.zeros(lg.shape, jnp.int32)
    for k in range(TOP_K):
        gate_out = jnp.where(lane == k, es[k] / den, gate_out)
        idx_out = jnp.where(lane == k, idxs[k], idx_out)
    gate_ref[...] = gate_out
    idx_ref[...] = idx_out


def _outproj(merged, w_out, x2d, gain, w_router_split, b_router_pad, tm=512):
    return pl.pallas_call(
        _outproj_kernel,
        grid=(N_TOK // tm,),
        in_specs=[
            pl.BlockSpec((tm, D_MODEL), lambda i: (i, 0)),
            pl.BlockSpec((D_MODEL, D_MODEL), lambda i: (0, 0)),
            pl.BlockSpec((tm, D_MODEL), lambda i: (i, 0)),
            pl.BlockSpec((1, D_MODEL), lambda i: (0, 0)),
            pl.BlockSpec((D_MODEL, 2 * LANES), lambda i: (0, 0)),
            pl.BlockSpec((1, LANES), lambda i: (0, 0)),
        ],
        out_specs=[
            pl.BlockSpec((tm, D_MODEL), lambda i: (i, 0)),
            pl.BlockSpec((tm, D_MODEL), lambda i: (i, 0)),
            pl.BlockSpec((tm, LANES), lambda i: (i, 0)),
            pl.BlockSpec((tm, LANES), lambda i: (i, 0)),
        ],
        out_shape=[
            jax.ShapeDtypeStruct((N_TOK, D_MODEL), f32),
            jax.ShapeDtypeStruct((N_TOK, D_MODEL), f32),
            jax.ShapeDtypeStruct((N_TOK, LANES), f32),
            jax.ShapeDtypeStruct((N_TOK, LANES), jnp.int32),
        ],
        compiler_params=_params(("parallel",)),
        name="outproj_router",
    )(merged, w_out, x2d, gain, w_router_split, b_router_pad)


def _moe_kernel(item_e_ref, item_start_ref, item_n_ref, tok_ref, dest_ref,
                hn_hbm, wg_ref, wl_ref, bg_ref, bl_ref, wd_ref, bd_ref, y_hbm,
                acc_ref, xg_ref, xb_ref, wgb_ref, wlb_ref, wdb_ref, gsem, ssem):
    w = pl.program_id(0)
    t = pl.program_id(1)
    last_w = pl.num_programs(0) - 1
    n = item_n_ref[w]
    start = item_start_ref[w]
    slot = lax.rem(w, 2)
    w_next = jnp.minimum(w + 1, last_w)
    start_next = item_start_ref[w_next]
    w_prev = jnp.maximum(w - 1, 0)
    start_prev = item_start_ref[w_prev]
    n_prev = jnp.where(w > 0, item_n_ref[w_prev], 0)
    is_last_live = jnp.logical_or(w == last_w, item_n_ref[w_next] == 0)

    def gather_row(row, base):
        tk = tok_ref[base + row]
        pltpu.make_async_copy(hn_hbm.at[pl.ds(tk, 1), :], xg_ref.at[pl.ds(row, 1), :], gsem).start()

    def gather_wait():
        pltpu.make_async_copy(hn_hbm.at[pl.ds(0, MOE_ROWS), :], xg_ref, gsem).wait()

    def scatter_row(row, base, cnt, sl, priority=0):
        d = jnp.where(row < cnt, dest_ref[base + row], N_ASSIGN + row)
        pltpu.make_async_copy(acc_ref.at[sl, pl.ds(row, 1), :], y_hbm.at[pl.ds(d, 1), :],
                              ssem).start(priority=priority)

    def scatter_wait(sl):
        pltpu.make_async_copy(acc_ref.at[sl], y_hbm.at[pl.ds(0, MOE_ROWS), :], ssem).wait()

    @pl.when(jnp.logical_and(w == 0, t == 0))
    def _():
        acc_ref[1] = jnp.zeros((MOE_ROWS, D_MODEL), f32)

        def issue(r, c):
            gather_row(r, start)
            return c
        lax.fori_loop(0, MOE_ROWS, issue, 0, unroll=8)

    @pl.when(jnp.logical_and(t == 0, n > 0))
    def _():
        gather_wait()
        xb_ref[...] = xg_ref[...].astype(bf16)
        acc_ref[slot] = jnp.broadcast_to(bd_ref[...], (MOE_ROWS, D_MODEL))

    def dma_chunk(r):
        row0 = t * (MOE_NSUB * MOE_CHUNK) + r * MOE_CHUNK
        for k in range(MOE_CHUNK):
            gather_row(row0 + k, start_next)
        for k in range(MOE_CHUNK):
            scatter_row(row0 + k, start_prev, n_prev, 1 - slot, priority=k % 2)

    def ffn_up(r0, rows):
        xs = xb_ref[r0:r0 + rows, :]
        hg = jnp.dot(xs, wgb_ref[...], preferred_element_type=f32) + bg_ref[...]
        hl = jnp.dot(xs, wlb_ref[...], preferred_element_type=f32) + bl_ref[...]
        return hg, hl

    def ffn_down(r0, rows, hg, hl):
        hg = jnp.minimum(hg, SWIGLU_LIMIT)
        hl = jnp.clip(hl, -SWIGLU_LIMIT, SWIGLU_LIMIT)
        act = hg * jax.nn.sigmoid(SWIGLU_ALPHA * hg) * (hl + 1.0)
        acc_ref[slot, r0:r0 + rows, :] += jnp.dot(act.astype(bf16), wdb_ref[...], preferred_element_type=f32)

    def ffn_rows(r0, rows):
        ffn_down(r0, rows, *ffn_up(r0, rows))

    @pl.when(n > 0)
    def _():
        nb_full = n // MOE_SUB
        rem = n - nb_full * MOE_SUB
        ext = jnp.logical_and(jnp.logical_and(rem > 0, rem <= MOE_HALF), nb_full >= 1)
        own_block = jnp.logical_or(rem > MOE_HALF, jnp.logical_and(rem > 0, nb_full == 0))
        nb = nb_full + jnp.where(own_block, 1, 0)
        tiny = jnp.logical_and(nb_full == 0, rem <= MOE_HALF)

        def cast_weights(r):
            if r == 0:
                wgb_ref[...] = wg_ref[...].astype(bf16)
                wlb_ref[...] = wl_ref[...].astype(bf16)
                wdb_ref[...] = wd_ref[...].astype(bf16)

        def single(r, rows, chunks=1):
            def body():
                for c in range(chunks):
                    dma_chunk(r + c)
                cast_weights(r)
                ffn_rows(r * MOE_SUB, rows)
            return body

        def pair(r, rows_b):
            def body():
                dma_chunk(r)
                dma_chunk(r + 1)
                cast_weights(r)
                ra, rb = r * MOE_SUB, (r + 1) * MOE_SUB
                ha = ffn_up(ra, MOE_SUB)
                hb = ffn_up(rb, rows_b)
                ffn_down(ra, MOE_SUB, *ha)
                ffn_down(rb, rows_b, *hb)
            return body

        def idle(*rs):
            def body():
                for r in rs:
                    dma_chunk(r)
            return body

        def last_single(r):
            return lambda: lax.cond(
                ext, single(r, MOE_SUB + MOE_HALF, 2),
                lambda: lax.cond(tiny, single(r, MOE_HALF, 2), single(r, MOE_SUB, 2)))

        for r in range(0, MOE_NSUB - 1, 2):
            is_last_pair = jnp.logical_and(ext, nb == r + 2)
            lax.cond(
                nb >= r + 2,
                lambda r=r, is_last_pair=is_last_pair: lax.cond(
                    is_last_pair, pair(r, MOE_SUB + MOE_HALF), pair(r, MOE_SUB)),
                lambda r=r: lax.cond(nb == r + 1, last_single(r), idle(r, r + 1)))
        for r in range(MOE_NSUB - MOE_NSUB % 2, MOE_NSUB):
            lax.cond(nb == r + 1, single(r, MOE_SUB), idle(r))

    @pl.when(jnp.logical_and(t == MOE_NF - 1, n > 0))
    def _():
        scatter_wait(1 - slot)

        @pl.when(is_last_live)
        def _():
            def issue(r, c):
                scatter_row(r, start, n, slot)
                return c
            lax.fori_loop(0, MOE_ROWS, issue, 0, unroll=8)
            scatter_wait(slot)
            gather_wait()


def _moe(n_items, item_e, item_start, item_n, tok_sorted, dest_sorted, hn2, w_gate_up, b_gate_up, w_down, b_down):
    tf = MOE_TF
    live_t = lambda w, t, n: jnp.where(n[w] > 0, t, MOE_NF - 1)
    grid_spec = pltpu.PrefetchScalarGridSpec(
        num_scalar_prefetch=5,
        grid=(n_items, MOE_NF),
        in_specs=[
            pl.BlockSpec(memory_space=pl.ANY),
            pl.BlockSpec((None, D_MODEL, tf), lambda w, t, e, s, n, tk, ds: (e[w], 0, live_t(w, t, n))),
            pl.BlockSpec((None, D_MODEL, tf), lambda w, t, e, s, n, tk, ds: (e[w], 0, MOE_NF + live_t(w, t, n))),
            pl.BlockSpec((None, 1, tf), lambda w, t, e, s, n, tk, ds: (e[w], 0, live_t(w, t, n))),
            pl.BlockSpec((None, 1, tf), lambda w, t, e, s, n, tk, ds: (e[w], 0, MOE_NF + live_t(w, t, n))),
            pl.BlockSpec((None, tf, D_MODEL), lambda w, t, e, s, n, tk, ds: (e[w], live_t(w, t, n), 0)),
            pl.BlockSpec((None, 1, D_MODEL), lambda w, t, e, s, n, tk, ds: (e[w], 0, 0)),
        ],
        out_specs=pl.BlockSpec(memory_space=pl.ANY),
        scratch_shapes=[
            pltpu.VMEM((2, MOE_ROWS, D_MODEL), f32),
            pltpu.VMEM((MOE_ROWS, D_MODEL), f32),
            pltpu.VMEM((MOE_ROWS, D_MODEL), bf16),
            pltpu.VMEM((D_MODEL, tf), bf16),
            pltpu.VMEM((D_MODEL, tf), bf16),
            pltpu.VMEM((tf, D_MODEL), bf16),
            pltpu.SemaphoreType.DMA,
            pltpu.SemaphoreType.DMA,
        ],
    )
    return pl.pallas_call(
        _moe_kernel,
        grid_spec=grid_spec,
        out_shape=jax.ShapeDtypeStruct((Y_ROWS, D_MODEL), f32),
        compiler_params=_params(("arbitrary", "arbitrary"), BIG_VMEM_LIMIT_BYTES),
        name="moe_experts",
    )(item_e, item_start, item_n, tok_sorted, dest_sorted,
      hn2, w_gate_up, w_gate_up, b_gate_up, b_gate_up, w_down, b_down)


def _combine_kernel(y0_ref, y1_ref, y2_ref, y3_ref, h_ref, gate_ref, g_ref, o_ref):
    h = h_ref[...]
    gates = gate_ref[...]
    for k, y_ref in enumerate((y0_ref, y1_ref, y2_ref, y3_ref)):
        h = h + y_ref[...] * gates[:, k:k + 1]
    o_ref[...] = h * lax.rsqrt(jnp.mean(h * h, axis=-1, keepdims=True) + RMS_EPS) * g_ref[...]


def _combine(y, h1, gates, gain, tm=256):
    slot_spec = lambda k: pl.BlockSpec((tm, D_MODEL), lambda i, k=k: (k * (N_TOK // tm) + i, 0))
    return pl.pallas_call(
        _combine_kernel,
        grid=(N_TOK // tm,),
        in_specs=[slot_spec(k) for k in range(TOP_K)] + [
            pl.BlockSpec((tm, D_MODEL), lambda i: (i, 0)),
            pl.BlockSpec((tm, TOP_K), lambda i: (i, 0)),
            pl.BlockSpec((1, D_MODEL), lambda i: (0, 0)),
        ],
        out_specs=pl.BlockSpec((tm, D_MODEL), lambda i: (i, 0)),
        out_shape=jax.ShapeDtypeStruct((N_TOK, D_MODEL), f32),
        compiler_params=_params(("parallel",)),
        name="combine_norm",
    )(y, y, y, y, h1, gates, gain)


def _route(top_idx):
    expert_flat = top_idx.reshape(-1)
    order = jnp.argsort(expert_flat, stable=True).astype(jnp.int32)
    counts = jnp.zeros((N_EXPERTS,), jnp.int32).at[expert_flat].add(1)
    starts = jnp.cumsum(counts) - counts
    tok_sorted = order // TOP_K
    dest_sorted = (order % TOP_K) * N_TOK + tok_sorted
    tok_sorted = jnp.pad(tok_sorted, (0, MOE_ROWS))
    dest_sorted = jnp.pad(dest_sorted, (0, MOE_ROWS))
    chunks = (counts + MOE_ROWS - 1) // MOE_ROWS
    chunk_end = jnp.cumsum(chunks)
    n_items = chunk_end[-1]
    w = jnp.arange(MOE_ITEMS, dtype=jnp.int32)
    w_eff = jnp.minimum(w, n_items - 1)
    e = jnp.sum(w_eff[:, None] >= chunk_end[None, :], axis=1).astype(jnp.int32)
    c = w_eff - (chunk_end - chunks)[e]
    item_start = starts[e] + c * MOE_ROWS
    item_n = jnp.where(w < n_items, jnp.clip(counts[e] - c * MOE_ROWS, 0, MOE_ROWS), 0)
    return n_items, e, item_start.astype(jnp.int32), item_n.astype(jnp.int32), tok_sorted, dest_sorted


def kernel(x, meta_tokens, rel_bias_table, attn_norm, w_in, fox_forget_bias, lam_q1, lam_k1, lam_q2, lam_k2,
           diff_subln, w_branch_fox, w_branch_diff, w_out, ffn_norm, w_router, b_router, w_gate_up, b_gate_up,
           w_down, b_down, final_norm):
    x2d = x.reshape(N_TOK, D_MODEL)
    w_t = jnp.swapaxes(w_in[0], 0, 1)
    fb_pad = jnp.pad(fox_forget_bias[0], (0, LANES - FOX_HEADS)).reshape(1, LANES)

    meta_pad = jnp.pad(meta_tokens, ((0, N_META_PAD - N_META), (0, 0)))
    proj, proj_meta, ff_real, ff_meta = _inproj(x2d, meta_pad, attn_norm, w_t, tm=1024, tn=512)

    ck, ckm = _forget_cumsum(ff_real, ff_meta, fb_pad)
    o_fox = _fox_attention(proj, proj_meta, ck, ckm)

    near, mbias = _bias_tiles(rel_bias_table)
    lam_vecs = jnp.concatenate([lam_q1, lam_k1, lam_q2, lam_k2], axis=0)
    o_diff = _diff_attention(proj, proj_meta, near, mbias, lam_vecs, diff_subln)

    merged = _gated_merge(o_fox, o_diff, w_branch_fox[0].astype(bf16), w_branch_diff[0].astype(bf16), proj)
    w_router_pad = jnp.pad(w_router[0], ((0, 0), (0, LANES - N_EXPERTS)))
    w_router_hi = w_router_pad.astype(bf16)
    w_router_lo = (w_router_pad - w_router_hi.astype(f32)).astype(bf16)
    w_router_split = jnp.concatenate([w_router_hi, w_router_lo], axis=1)
    b_router_pad = jnp.pad(b_router[0], (0, LANES - N_EXPERTS)).reshape(1, LANES)
    h1, hn2, gates_pad, idx_pad = _outproj(merged, w_out[0].astype(bf16), x2d, ffn_norm, w_router_split, b_router_pad)
    gates = gates_pad[:, :TOP_K]

    n_items, item_e, item_start, item_n, tok_sorted, dest_sorted = _route(idx_pad[:, :TOP_K])
    y = _moe(n_items, item_e, item_start, item_n, tok_sorted, dest_sorted, hn2,
             w_gate_up[0], b_gate_up[0].reshape(N_EXPERTS, 1, 2 * D_EXPERT),
             w_down[0], b_down[0].reshape(N_EXPERTS, 1, D_MODEL))
    out = _combine(y, h1, gates, final_norm.reshape(1, D_MODEL))
    return out.reshape(BATCH, SEQ, D_MODEL)
```

```python
import functools
import math

import jax
import jax.numpy as jnp
from jax import lax
from jax.experimental import pallas as pl
from jax.experimental.pallas import tpu as pltpu

D_MODEL = 2048
BATCH = 4
SEQ = 2048
N_TOK = BATCH * SEQ
N_META = 16
N_META_PAD = 128
HEAD_DIM = 128
FOX_HEADS = 8
DIFF_HEADS = 4
DIFF_V_DIM = 2 * HEAD_DIM
FOX_WIDTH = FOX_HEADS * HEAD_DIM
DIFF_QK_WIDTH = DIFF_HEADS * 2 * HEAD_DIM
DIFF_WIDTH = DIFF_HEADS * DIFF_V_DIM
N_BUCKETS = 32
MAX_DISTANCE = 128
N_EXPERTS = 32
TOP_K = 4
D_EXPERT = D_MODEL
SWIGLU_LIMIT = 7.0
SWIGLU_ALPHA = 1.702
RMS_EPS = 1e-5
NEG_INF = -1e30
LAMBDA_INIT = 0.8 - 0.6 * math.exp(-0.3 * 0)
ATTN_SCALE = HEAD_DIM ** -0.5
LOG2E = math.log2(math.e)
Q_PRESCALE = ATTN_SCALE * LOG2E

LANES = 128
SUBLANES = 8
VMEM_LIMIT_BYTES = 56 * 1024 * 1024
BIG_VMEM_LIMIT_BYTES = 60 * 1024 * 1024

COL_FQ = 0
COL_FK = COL_FQ + FOX_WIDTH
COL_FV = COL_FK + FOX_WIDTH
COL_DQ = COL_FV + FOX_WIDTH
COL_DK = COL_DQ + DIFF_QK_WIDTH
COL_DV = COL_DK + DIFF_QK_WIDTH
COL_GF = COL_DV + DIFF_WIDTH
COL_GD = COL_GF + D_MODEL
D_PROJ = COL_GD + D_MODEL

INPROJ_XT = 256
ATTN_TILE = 512
N_QBLK = SEQ // ATTN_TILE

MOE_ROWS = 1280
MOE_SUB = 256
MOE_HALF = MOE_SUB // 2
MOE_NSUB = MOE_ROWS // MOE_SUB
MOE_TF = 256
MOE_NF = D_EXPERT // MOE_TF
MOE_CHUNK = MOE_ROWS // (MOE_NF * MOE_NSUB)
N_ASSIGN = N_TOK * TOP_K
MOE_ITEMS = N_EXPERTS + N_ASSIGN // MOE_ROWS
Y_ROWS = N_ASSIGN + MOE_ROWS

f32 = jnp.float32
bf16 = jnp.bfloat16


def _params(sem, vmem=VMEM_LIMIT_BYTES):
    return pltpu.CompilerParams(dimension_semantics=sem, vmem_limit_bytes=vmem)


def _rms_bf16(x, g):
    return (x * lax.rsqrt(jnp.mean(x * x, axis=-1, keepdims=True) + RMS_EPS) * g).astype(bf16)


def _inproj_kernel(x_hbm, xm_ref, g_ref, wt_ref, wfft_ref, o_ref, om_ref, ff_ref, ffm_ref,
                   xn_ref, xnm_ref, xt_ref, wb_ref, wffb_ref, sem, *, tm, tn):
    j = pl.program_id(0)
    i = pl.program_id(1)
    rows = pl.ds(pl.multiple_of(i * tm, tm), tm)
    is_q = jnp.logical_or(j < COL_FK // tn, jnp.logical_and(j >= COL_DQ // tn, j < COL_DK // tn))
    qs = jnp.where(is_q, Q_PRESCALE, 1.0)

    @pl.when(jnp.logical_and(j == 0, i == 0))
    def _():
        wffb_ref[...] = wfft_ref[...].T.astype(bf16)
        xnm_ref[...] = _rms_bf16(xm_ref[...], g_ref[...])
        ffm_ref[...] = jnp.dot(xnm_ref[...], wffb_ref[...], preferred_element_type=f32)

    @pl.when(i == 0)
    def _():
        wb_ref[...] = wt_ref[...].T.astype(bf16)
        ym = jnp.dot(xnm_ref[...], wb_ref[...], preferred_element_type=f32)
        om_ref[...] = (ym * qs).astype(bf16)

    @pl.when(j == 0)
    def _():
        def piece(s):
            r0 = pl.multiple_of(i * tm + s * INPROJ_XT, INPROJ_XT)
            return pltpu.make_async_copy(x_hbm.at[pl.ds(r0, INPROJ_XT), :], xt_ref.at[s % 2], sem.at[s % 2])

        n_pieces = tm // INPROJ_XT
        piece(0).start()
        for s in range(n_pieces):
            if s + 1 < n_pieces:
                piece(s + 1).start()
            piece(s).wait()
            xn = _rms_bf16(xt_ref[s % 2], g_ref[...])
            xn_ref[pl.ds(pl.multiple_of(i * tm + s * INPROJ_XT, INPROJ_XT), INPROJ_XT), :] = xn
            ff_ref[s * INPROJ_XT:(s + 1) * INPROJ_XT, :] = jnp.dot(xn, wffb_ref[...], preferred_element_type=f32)

    y = jnp.dot(xn_ref[rows, :], wb_ref[...], preferred_element_type=f32)
    o_ref[...] = (y * qs).astype(bf16)


def _inproj(x2d, meta_pad, gain, w_t, tm, tn):
    n = x2d.shape[0]
    n_i = n // tm
    w_row = lambda j: SUBLANES * (j * (tn // SUBLANES) + jnp.where(j * tn >= COL_DQ, FOX_HEADS // SUBLANES, 0))
    first_pass_row = lambda j, i: jnp.where(j == 0, i, n_i - 1)
    return pl.pallas_call(
        functools.partial(_inproj_kernel, tm=tm, tn=tn),
        grid=(D_PROJ // tn, n_i),
        in_specs=[
            pl.BlockSpec(memory_space=pl.ANY),
            pl.BlockSpec((N_META_PAD, D_MODEL), lambda j, i: (0, 0)),
            pl.BlockSpec((1, D_MODEL), lambda j, i: (0, 0)),
            pl.BlockSpec((pl.Element(tn), pl.Element(D_MODEL)), lambda j, i: (w_row(j), 0)),
            pl.BlockSpec((pl.Element(LANES), pl.Element(D_MODEL)), lambda j, i: (COL_DQ, 0)),
        ],
        out_specs=[
            pl.BlockSpec((tm, tn), lambda j, i: (i, j)),
            pl.BlockSpec((N_META_PAD, tn), lambda j, i: (0, j)),
            pl.BlockSpec((tm, LANES), lambda j, i: (first_pass_row(j, i), 0)),
            pl.BlockSpec((N_META_PAD, LANES), lambda j, i: (0, 0)),
        ],
        out_shape=[
            jax.ShapeDtypeStruct((n, D_PROJ), bf16),
            jax.ShapeDtypeStruct((N_META_PAD, D_PROJ), bf16),
            jax.ShapeDtypeStruct((n, LANES), f32),
            jax.ShapeDtypeStruct((N_META_PAD, LANES), f32),
        ],
        scratch_shapes=[
            pltpu.VMEM((n, D_MODEL), bf16),
            pltpu.VMEM((N_META_PAD, D_MODEL), bf16),
            pltpu.VMEM((2, INPROJ_XT, D_MODEL), f32),
            pltpu.VMEM((D_MODEL, tn), bf16),
            pltpu.VMEM((D_MODEL, LANES), bf16),
            pltpu.SemaphoreType.DMA((2,)),
        ],
        compiler_params=_params(("arbitrary", "arbitrary"), BIG_VMEM_LIMIT_BYTES),
        name="inproj",
    )(x2d, meta_pad, gain, w_t, w_t)


def _log_sigmoid(x):
    return jnp.minimum(x, 0.0) - jnp.log(1.0 + jnp.exp(-jnp.abs(x)))


def _lane_cumsum(x):
    n = x.shape[-1]
    lane = lax.broadcasted_iota(jnp.int32, x.shape, x.ndim - 1)
    s = 1
    while s < n:
        x = x + jnp.where(lane >= s, pltpu.roll(x, s, x.ndim - 1), 0.0)
        s *= 2
    return x


def _cum_kernel(ff_ref, ffm_ref, fb_ref, ck_ref, ckm_ref):
    fb = fb_ref[...]
    row = lax.broadcasted_iota(jnp.int32, (LANES, LANES), 0)
    lfm = jnp.where(row < N_META, _log_sigmoid(ffm_ref[...] + fb), 0.0)
    cm = _lane_cumsum(lfm.T)
    m_total = cm[:, N_META - 1:N_META]
    lf = _log_sigmoid(ff_ref[...] + fb)
    cr = _lane_cumsum(lf.T) + m_total
    ck_ref[...] = cr[:SUBLANES] * LOG2E
    ckm_ref[...] = cm[:SUBLANES] * LOG2E


def _forget_cumsum(ff_real, ff_meta_pad, fb_pad):
    return pl.pallas_call(
        _cum_kernel,
        grid=(BATCH,),
        in_specs=[
            pl.BlockSpec((SEQ, LANES), lambda b: (b, 0)),
            pl.BlockSpec((LANES, LANES), lambda b: (0, 0)),
            pl.BlockSpec((1, LANES), lambda b: (0, 0)),
        ],
        out_specs=[
            pl.BlockSpec((None, SUBLANES, SEQ), lambda b: (b, 0, 0)),
            pl.BlockSpec((None, SUBLANES, LANES), lambda b: (b, 0, 0)),
        ],
        out_shape=[
            jax.ShapeDtypeStruct((BATCH, FOX_HEADS, SEQ), f32),
            jax.ShapeDtypeStruct((BATCH, FOX_HEADS, LANES), f32),
        ],
        compiler_params=_params(("parallel",)),
        name="forget_cumsum",
    )(ff_real, ff_meta_pad, fb_pad)


def _qk(q, k):
    return lax.dot_general(q, k, (((1,), (1,)), ((), ())), preferred_element_type=f32)


def _row_to_col(row):
    t = row.shape[-1]
    r = lax.broadcasted_iota(jnp.int32, (t, t), 0)
    c = lax.broadcasted_iota(jnp.int32, (t, t), 1)
    return jnp.sum(jnp.where(r == c, row, 0.0), axis=1, keepdims=True)


def _softmax_pieces(pieces, row_shift=None):
    m = None
    for s, _ in pieces:
        mi = jnp.max(s, axis=-1, keepdims=True)
        m = mi if m is None else jnp.maximum(m, mi)
    if row_shift is not None:
        m = (m + row_shift) - row_shift
    l = acc = None
    for s, v in pieces:
        p = jnp.exp2(s - m)
        li = jnp.sum(p, axis=-1, keepdims=True)
        ai = jnp.dot(p.astype(bf16), v, preferred_element_type=f32)
        l = li if l is None else l + li
        acc = ai if acc is None else acc + ai
    return acc / l


def _switch_qblock(i, branches):
    def pick(lo, hi):
        if lo == hi:
            return branches[lo]
        mid = (lo + hi) // 2
        return lambda: lax.cond(i <= mid, pick(lo, mid), pick(mid + 1, hi))
    pick(0, len(branches) - 1)()


FOX_GROUP = 2


def _fox_kernel(q_ref, k_ref, v_ref, km_ref, vm_ref, ck_ref, ckm_ref, o_ref):
    hg = pl.program_id(1)
    i = pl.program_id(2)
    t = ATTN_TILE
    q0 = pl.multiple_of(i * t, t)
    meta_ok = lax.broadcasted_iota(jnp.int32, (t, N_META_PAD), 1) < N_META
    r = lax.broadcasted_iota(jnp.int32, (t, t), 0)
    c = lax.broadcasted_iota(jnp.int32, (t, t), 1)

    def attend(n_far):
        def body():
            for g in range(FOX_GROUP):
                col = slice(g * HEAD_DIM, (g + 1) * HEAD_DIM)
                head = pl.ds(hg * FOX_GROUP + g, 1)
                q = q_ref[:, col]
                cq = _row_to_col(ck_ref[head, pl.ds(q0, t)])
                u_meta = jnp.where(meta_ok, _qk(q, km_ref[:, col]) - ckm_ref[head, :], NEG_INF)
                pieces = [(u_meta, vm_ref[:, col])]
                if n_far:
                    pieces.append((_qk(q, k_ref[0:n_far, col]) - ck_ref[head, 0:n_far], v_ref[0:n_far, col]))
                u_diag = _qk(q, k_ref[n_far:n_far + t, col]) - ck_ref[head, n_far:n_far + t]
                pieces.append((jnp.where(c <= r, u_diag, NEG_INF), v_ref[n_far:n_far + t, col]))
                o_ref[:, col] = _softmax_pieces(pieces, row_shift=cq).astype(bf16)
        return body

    _switch_qblock(i, [attend(ib * t) for ib in range(N_QBLK)])


def _fox_attention(proj, proj_meta, ck, ckm):
    t = ATTN_TILE
    w = FOX_GROUP * HEAD_DIM
    cb = lambda col: col // w
    return pl.pallas_call(
        _fox_kernel,
        grid=(BATCH, FOX_HEADS // FOX_GROUP, N_QBLK),
        in_specs=[
            pl.BlockSpec((t, w), lambda b, h, i: (b * N_QBLK + i, cb(COL_FQ) + h)),
            pl.BlockSpec((SEQ, w), lambda b, h, i: (b, cb(COL_FK) + h)),
            pl.BlockSpec((SEQ, w), lambda b, h, i: (b, cb(COL_FV) + h)),
            pl.BlockSpec((N_META_PAD, w), lambda b, h, i: (0, cb(COL_FK) + h)),
            pl.BlockSpec((N_META_PAD, w), lambda b, h, i: (0, cb(COL_FV) + h)),
            pl.BlockSpec((None, FOX_HEADS, SEQ), lambda b, h, i: (b, 0, 0)),
            pl.BlockSpec((None, FOX_HEADS, LANES), lambda b, h, i: (b, 0, 0)),
        ],
        out_specs=pl.BlockSpec((t, w), lambda b, h, i: (b * N_QBLK + i, h)),
        out_shape=jax.ShapeDtypeStruct((N_TOK, FOX_WIDTH), bf16),
        compiler_params=_params(("parallel", "parallel", "arbitrary")),
        name="fox_attention",
    )(proj, proj, proj, proj_meta, proj_meta, ck, ckm)


def _t5_bias(dist, table_ref, h):
    n = jnp.maximum(dist, 0)
    max_exact = N_BUCKETS // 2
    log_part = jnp.log(jnp.maximum(n, 1).astype(f32) / max_exact) / math.log(MAX_DISTANCE / max_exact)
    v = log_part * (N_BUCKETS - max_exact)
    far = table_ref[N_BUCKETS - 1, h]
    val = lambda b: (table_ref[b, h] - far) * LOG2E
    large = jnp.zeros(dist.shape, f32)
    for b in range(N_BUCKETS - 2, max_exact - 1, -1):
        large = jnp.where(v < b + 1 - max_exact, val(b), large)
    out = large
    for b in range(max_exact):
        out = jnp.where(n == b, val(b), out)
    return out


def _bias_kernel(table_ref, near_ref, meta_ref):
    h = pl.program_id(0)
    t = ATTN_TILE
    r = lax.broadcasted_iota(jnp.int32, (t, 2 * t), 0)
    c = lax.broadcasted_iota(jnp.int32, (t, 2 * t), 1)
    near_ref[...] = _t5_bias(r + t - c, table_ref, h)
    r = lax.broadcasted_iota(jnp.int32, (t, LANES), 0)
    c = lax.broadcasted_iota(jnp.int32, (t, LANES), 1)
    meta_ref[...] = _t5_bias(N_META + r - c, table_ref, h)


def _bias_tiles(table):
    t = ATTN_TILE
    return pl.pallas_call(
        _bias_kernel,
        grid=(DIFF_HEADS,),
        in_specs=[pl.BlockSpec(memory_space=pltpu.SMEM)],
        out_specs=[
            pl.BlockSpec((None, t, 2 * t), lambda h: (h, 0, 0)),
            pl.BlockSpec((None, t, LANES), lambda h: (h, 0, 0)),
        ],
        out_shape=[
            jax.ShapeDtypeStruct((DIFF_HEADS, t, 2 * t), f32),
            jax.ShapeDtypeStruct((DIFF_HEADS, t, LANES), f32),
        ],
        compiler_params=_params(("arbitrary",)),
        name="t5_bias_tiles",
    )(table)


def _diff_kernel(q1_ref, q2_ref, k1_ref, k2_ref, v_ref, k1m_ref, k2m_ref, vm_ref,
                 near_ref, mbias_ref, lam_ref, subln_ref, o_ref):
    i = pl.program_id(2)
    t = ATTN_TILE
    q1 = q1_ref[...]
    q2 = q2_ref[...]
    q0 = pl.multiple_of(i * t, t)
    lam = (jnp.exp(jnp.sum(lam_ref[0:1, :] * lam_ref[1:2, :], axis=-1, keepdims=True))
           - jnp.exp(jnp.sum(lam_ref[2:3, :] * lam_ref[3:4, :], axis=-1, keepdims=True))
           + LAMBDA_INIT)

    mb = jnp.where(i == 0, mbias_ref[...], 0.0)
    meta_ok = lax.broadcasted_iota(jnp.int32, (t, N_META_PAD), 1) < N_META
    vm = vm_ref[...]

    def attend(n_far, n_near):
        def body():
            r = lax.broadcasted_iota(jnp.int32, (t, n_near), 0)
            c = lax.broadcasted_iota(jnp.int32, (t, n_near), 1)
            mask = c <= r + (n_near - t)
            bias = near_ref[:, 2 * t - n_near:]
            outs = []
            for q, k_ref, km_ref in ((q1, k1_ref, k1m_ref), (q2, k2_ref, k2m_ref)):
                pieces = [(jnp.where(meta_ok, _qk(q, km_ref[...]) + mb, NEG_INF), vm)]
                if n_far:
                    pieces.append((_qk(q, k_ref[0:n_far, :]), v_ref[0:n_far, :]))
                s_near = jnp.where(mask, _qk(q, k_ref[n_far:n_far + n_near, :]) + bias, NEG_INF)
                pieces.append((s_near, v_ref[n_far:n_far + n_near, :]))
                outs.append(_softmax_pieces(pieces))
            o = outs[0] - lam * outs[1]
            y = o * lax.rsqrt(jnp.mean(o * o, axis=-1, keepdims=True) + RMS_EPS) * subln_ref[...]
            o_ref[...] = (y * (1.0 - LAMBDA_INIT)).astype(bf16)
        return body

    branches = []
    for ib in range(N_QBLK):
        n_near = min(2 * t, (ib + 1) * t)
        branches.append(attend((ib + 1) * t - n_near, n_near))
    _switch_qblock(i, branches)


def _diff_attention(proj, proj_meta, near, mbias, lam_vecs, subln):
    t = ATTN_TILE
    cb = lambda col: col // HEAD_DIM
    vb = lambda col: col // DIFF_V_DIM
    row = lambda b, h, i: b * N_QBLK + i
    return pl.pallas_call(
        _diff_kernel,
        grid=(BATCH, DIFF_HEADS, N_QBLK),
        in_specs=[
            pl.BlockSpec((t, HEAD_DIM), lambda b, h, i: (row(b, h, i), cb(COL_DQ) + 2 * h)),
            pl.BlockSpec((t, HEAD_DIM), lambda b, h, i: (row(b, h, i), cb(COL_DQ) + 2 * h + 1)),
            pl.BlockSpec((SEQ, HEAD_DIM), lambda b, h, i: (b, cb(COL_DK) + 2 * h)),
            pl.BlockSpec((SEQ, HEAD_DIM), lambda b, h, i: (b, cb(COL_DK) + 2 * h + 1)),
            pl.BlockSpec((SEQ, DIFF_V_DIM), lambda b, h, i: (b, vb(COL_DV) + h)),
            pl.BlockSpec((N_META_PAD, HEAD_DIM), lambda b, h, i: (0, cb(COL_DK) + 2 * h)),
            pl.BlockSpec((N_META_PAD, HEAD_DIM), lambda b, h, i: (0, cb(COL_DK) + 2 * h + 1)),
            pl.BlockSpec((N_META_PAD, DIFF_V_DIM), lambda b, h, i: (0, vb(COL_DV) + h)),
            pl.BlockSpec((None, t, 2 * t), lambda b, h, i: (h, 0, 0)),
            pl.BlockSpec((None, t, LANES), lambda b, h, i: (h, 0, 0)),
            pl.BlockSpec((4, HEAD_DIM), lambda b, h, i: (0, 0)),
            pl.BlockSpec((1, DIFF_V_DIM), lambda b, h, i: (0, 0)),
        ],
        out_specs=pl.BlockSpec((t, DIFF_V_DIM), lambda b, h, i: (row(b, h, i), h)),
        out_shape=jax.ShapeDtypeStruct((N_TOK, DIFF_WIDTH), bf16),
        compiler_params=_params(("parallel", "parallel", "arbitrary")),
        name="diff_attention",
    )(proj, proj, proj, proj, proj, proj_meta, proj_meta, proj_meta, near, mbias, lam_vecs, subln)


def _merge_kernel(of_ref, od_ref, wf_ref, wd_ref, gf_ref, gd_ref, o_ref):
    yf = jnp.dot(of_ref[...], wf_ref[...], preferred_element_type=f32)
    yd = jnp.dot(od_ref[...], wd_ref[...], preferred_element_type=f32)
    gf = jax.nn.sigmoid(gf_ref[...].astype(f32))
    gd = jax.nn.sigmoid(gd_ref[...].astype(f32))
    o_ref[...] = (gf * yf + gd * yd).astype(bf16)


def _gated_merge(o_fox, o_diff, w_bf, w_bd, proj, tm=1024, tn=512):
    return pl.pallas_call(
        _merge_kernel,
        grid=(D_MODEL // tn, N_TOK // tm),
        in_specs=[
            pl.BlockSpec((tm, FOX_WIDTH), lambda j, i: (i, 0)),
            pl.BlockSpec((tm, DIFF_WIDTH), lambda j, i: (i, 0)),
            pl.BlockSpec((FOX_WIDTH, tn), lambda j, i: (0, j)),
            pl.BlockSpec((DIFF_WIDTH, tn), lambda j, i: (0, j)),
            pl.BlockSpec((tm, tn), lambda j, i: (i, COL_GF // tn + j)),
            pl.BlockSpec((tm, tn), lambda j, i: (i, COL_GD // tn + j)),
        ],
        out_specs=pl.BlockSpec((tm, tn), lambda j, i: (i, j)),
        out_shape=jax.ShapeDtypeStruct((N_TOK, D_MODEL), bf16),
        compiler_params=_params(("parallel", "arbitrary")),
        name="gated_merge",
    )(o_fox, o_diff, w_bf, w_bd, proj, proj)


def _outproj_kernel(m_ref, w_ref, x_ref, g_ref, wr_ref, br_ref, h_ref, hn_ref, gate_ref, idx_ref):
    h1 = x_ref[...] + jnp.dot(m_ref[...], w_ref[...], preferred_element_type=f32)
    h_ref[...] = h1
    hn = h1 * lax.rsqrt(jnp.mean(h1 * h1, axis=-1, keepdims=True) + RMS_EPS) * g_ref[...]
    hn_ref[...] = hn
    hn_hi = hn.astype(bf16)
    hn_lo = (hn - hn_hi.astype(f32)).astype(bf16)
    a = jnp.dot(hn_hi, wr_ref[...], preferred_element_type=f32)
    b = jnp.dot(hn_lo, wr_ref[:, :LANES], preferred_element_type=f32)
    lg = a[:, :LANES] + a[:, LANES:] + b + br_ref[...]

    lane = lax.broadcasted_iota(jnp.int32, lg.shape, 1)
    cur = jnp.where(lane < N_EXPERTS, lg, -jnp.inf)
    vals, idxs = [], []
    for _ in range(TOP_K):
        m = jnp.max(cur, axis=-1, keepdims=True)
        idx = jnp.min(jnp.where(cur == m, lane, LANES), axis=-1, keepdims=True)
        vals.append(m)
        idxs.append(idx)
        cur = jnp.where(lane == idx, -jnp.inf, cur)
    es = [jnp.exp(v - vals[0]) for v in vals]
    den = es[0] + es[1] + es[2] + es[3]
    gate_out = jnp.zeros(lg.shape, f32)
    idx_out = jnp.zeros(lg.shape, jnp.int32)
    for k in range(TOP_K):
        gate_out = jnp.where(lane == k, es[k] / den, gate_out)
        idx_out = jnp.where(lane == k, idxs[k], idx_out)
    gate_ref[...] = gate_out
    idx_ref[...] = idx_out


def _outproj(merged, w_out, x2d, gain, w_router_split, b_router_pad, tm=512):
    return pl.pallas_call(
        _outproj_kernel,
        grid=(N_TOK // tm,),
        in_specs=[
            pl.BlockSpec((tm, D_MODEL), lambda i: (i, 0)),
            pl.BlockSpec((D_MODEL, D_MODEL), lambda i: (0, 0)),
            pl.BlockSpec((tm, D_MODEL), lambda i: (i, 0)),
            pl.BlockSpec((1, D_MODEL), lambda i: (0, 0)),
            pl.BlockSpec((D_MODEL, 2 * LANES), lambda i: (0, 0)),
            pl.BlockSpec((1, LANES), lambda i: (0, 0)),
        ],
        out_specs=[
            pl.BlockSpec((tm, D_MODEL), lambda i: (i, 0)),
            pl.BlockSpec((tm, D_MODEL), lambda i: (i, 0)),
            pl.BlockSpec((tm, LANES), lambda i: (i, 0)),
            pl.BlockSpec((tm, LANES), lambda i: (i, 0)),
        ],
        out_shape=[
            jax.ShapeDtypeStruct((N_TOK, D_MODEL), f32),
            jax.ShapeDtypeStruct((N_TOK, D_MODEL), f32),
            jax.ShapeDtypeStruct((N_TOK, LANES), f32),
            jax.ShapeDtypeStruct((N_TOK, LANES), jnp.int32),
        ],
        compiler_params=_params(("parallel",)),
        name="outproj_router",
    )(merged, w_out, x2d, gain, w_router_split, b_router_pad)


def _moe_kernel(item_e_ref, item_start_ref, item_n_ref, tok_ref, dest_ref,
                hn_hbm, wg_ref, wl_ref, bg_ref, bl_ref, wd_ref, bd_ref, y_hbm,
                acc_ref, xg_ref, xb_ref, wgb_ref, wlb_ref, wdb_ref, gsem, ssem):
    w = pl.program_id(0)
    t = pl.program_id(1)
    last_w = pl.num_programs(0) - 1
    n = item_n_ref[w]
    start = item_start_ref[w]
    slot = lax.rem(w, 2)
    w_next = jnp.minimum(w + 1, last_w)
    start_next = item_start_ref[w_next]
    w_prev = jnp.maximum(w - 1, 0)
    start_prev = item_start_ref[w_prev]
    n_prev = jnp.where(w > 0, item_n_ref[w_prev], 0)
    is_last_live = jnp.logical_or(w == last_w, item_n_ref[w_next] == 0)

    def gather_row(row, base):
        tk = tok_ref[base + row]
        pltpu.make_async_copy(hn_hbm.at[pl.ds(tk, 1), :], xg_ref.at[pl.ds(row, 1), :], gsem).start()

    def gather_wait():
        pltpu.make_async_copy(hn_hbm.at[pl.ds(0, MOE_ROWS), :], xg_ref, gsem).wait()

    def scatter_row(row, base, cnt, sl, priority=0):
        d = jnp.where(row < cnt, dest_ref[base + row], N_ASSIGN + row)
        pltpu.make_async_copy(acc_ref.at[sl, pl.ds(row, 1), :], y_hbm.at[pl.ds(d, 1), :],
                              ssem).start(priority=priority)

    def scatter_wait(sl):
        pltpu.make_async_copy(acc_ref.at[sl], y_hbm.at[pl.ds(0, MOE_ROWS), :], ssem).wait()

    @pl.when(jnp.logical_and(w == 0, t == 0))
    def _():
        acc_ref[1] = jnp.zeros((MOE_ROWS, D_MODEL), f32)

        def issue(r, c):
            gather_row(r, start)
            return c
        lax.fori_loop(0, MOE_ROWS, issue, 0, unroll=8)

    @pl.when(jnp.logical_and(t == 0, n > 0))
    def _():
        gather_wait()
        xb_ref[...] = xg_ref[...].astype(bf16)
        acc_ref[slot] = jnp.broadcast_to(bd_ref[...], (MOE_ROWS, D_MODEL))

    def dma_chunk(r):
        row0 = t * (MOE_NSUB * MOE_CHUNK) + r * MOE_CHUNK
        for k in range(MOE_CHUNK):
            gather_row(row0 + k, start_next)
        for k in range(MOE_CHUNK):
            scatter_row(row0 + k, start_prev, n_prev, 1 - slot, priority=k % 2)

    def ffn_up(r0, rows):
        xs = xb_ref[r0:r0 + rows, :]
        hg = jnp.dot(xs, wgb_ref[...], preferred_element_type=f32) + bg_ref[...]
        hl = jnp.dot(xs, wlb_ref[...], preferred_element_type=f32) + bl_ref[...]
        return hg, hl

    def ffn_down(r0, rows, hg, hl):
        hg = jnp.minimum(hg, SWIGLU_LIMIT)
        hl = jnp.clip(hl, -SWIGLU_LIMIT, SWIGLU_LIMIT)
        act = hg * jax.nn.sigmoid(SWIGLU_ALPHA * hg) * (hl + 1.0)
        acc_ref[slot, r0:r0 + rows, :] += jnp.dot(act.astype(bf16), wdb_ref[...], preferred_element_type=f32)

    def ffn_rows(r0, rows):
        ffn_down(r0, rows, *ffn_up(r0, rows))

    @pl.when(n > 0)
    def _():
        nb_full = n // MOE_SUB
        rem = n - nb_full * MOE_SUB
        ext = jnp.logical_and(jnp.logical_and(rem > 0, rem <= MOE_HALF), nb_full >= 1)
        own_block = jnp.logical_or(rem > MOE_HALF, jnp.logical_and(rem > 0, nb_full == 0))
        nb = nb_full + jnp.where(own_block, 1, 0)
        tiny = jnp.logical_and(nb_full == 0, rem <= MOE_HALF)

        def cast_weights(r):
            if r == 0:
                wgb_ref[...] = wg_ref[...].astype(bf16)
                wlb_ref[...] = wl_ref[...].astype(bf16)
                wdb_ref[...] = wd_ref[...].astype(bf16)

        def single(r, rows, chunks=1):
            def body():
                for c in range(chunks):
                    dma_chunk(r + c)
                cast_weights(r)
                ffn_rows(r * MOE_SUB, rows)
            return body

        def pair(r, rows_b):
            def body():
                dma_chunk(r)
                dma_chunk(r + 1)
                cast_weights(r)
                ra, rb = r * MOE_SUB, (r + 1) * MOE_SUB
                ha = ffn_up(ra, MOE_SUB)
                hb = ffn_up(rb, rows_b)
                ffn_down(ra, MOE_SUB, *ha)
                ffn_down(rb, rows_b, *hb)
            return body

        def idle(*rs):
            def body():
                for r in rs:
                    dma_chunk(r)
            return body

        def last_single(r):
            return lambda: lax.cond(
                ext, single(r, MOE_SUB + MOE_HALF, 2),
                lambda: lax.cond(tiny, single(r, MOE_HALF, 2), single(r, MOE_SUB, 2)))

        for r in range(0, MOE_NSUB - 1, 2):
            is_last_pair = jnp.logical_and(ext, nb == r + 2)
            lax.cond(
                nb >= r + 2,
                lambda r=r, is_last_pair=is_last_pair: lax.cond(
                    is_last_pair, pair(r, MOE_SUB + MOE_HALF), pair(r, MOE_SUB)),
                lambda r=r: lax.cond(nb == r + 1, last_single(r), idle(r, r + 1)))
        for r in range(MOE_NSUB - MOE_NSUB % 2, MOE_NSUB):
            lax.cond(nb == r + 1, single(r, MOE_SUB), idle(r))

    @pl.when(jnp.logical_and(t == MOE_NF - 1, n > 0))
    def _():
        scatter_wait(1 - slot)

        @pl.when(is_last_live)
        def _():
            def issue(r, c):
                scatter_row(r, start, n, slot)
                return c
            lax.fori_loop(0, MOE_ROWS, issue, 0, unroll=8)
            scatter_wait(slot)
            gather_wait()


def _moe(n_items, item_e, item_start, item_n, tok_sorted, dest_sorted, hn2, w_gate_up, b_gate_up, w_down, b_down):
    tf = MOE_TF
    live_t = lambda w, t, n: jnp.where(n[w] > 0, t, MOE_NF - 1)
    grid_spec = pltpu.PrefetchScalarGridSpec(
        num_scalar_prefetch=5,
        grid=(n_items, MOE_NF),
        in_specs=[
            pl.BlockSpec(memory_space=pl.ANY),
            pl.BlockSpec((None, D_MODEL, tf), lambda w, t, e, s, n, tk, ds: (e[w], 0, live_t(w, t, n))),
            pl.BlockSpec((None, D_MODEL, tf), lambda w, t, e, s, n, tk, ds: (e[w], 0, MOE_NF + live_t(w, t, n))),
            pl.BlockSpec((None, 1, tf), lambda w, t, e, s, n, tk, ds: (e[w], 0, live_t(w, t, n))),
            pl.BlockSpec((None, 1, tf), lambda w, t, e, s, n, tk, ds: (e[w], 0, MOE_NF + live_t(w, t, n))),
            pl.BlockSpec((None, tf, D_MODEL), lambda w, t, e, s, n, tk, ds: (e[w], live_t(w, t, n), 0)),
            pl.BlockSpec((None, 1, D_MODEL), lambda w, t, e, s, n, tk, ds: (e[w], 0, 0)),
        ],
        out_specs=pl.BlockSpec(memory_space=pl.ANY),
        scratch_shapes=[
            pltpu.VMEM((2, MOE_ROWS, D_MODEL), f32),
            pltpu.VMEM((MOE_ROWS, D_MODEL), f32),
            pltpu.VMEM((MOE_ROWS, D_MODEL), bf16),
            pltpu.VMEM((D_MODEL, tf), bf16),
            pltpu.VMEM((D_MODEL, tf), bf16),
            pltpu.VMEM((tf, D_MODEL), bf16),
            pltpu.SemaphoreType.DMA,
            pltpu.SemaphoreType.DMA,
        ],
    )
    return pl.pallas_call(
        _moe_kernel,
        grid_spec=grid_spec,
        out_shape=jax.ShapeDtypeStruct((Y_ROWS, D_MODEL), f32),
        compiler_params=_params(("arbitrary", "arbitrary"), BIG_VMEM_LIMIT_BYTES),
        name="moe_experts",
    )(item_e, item_start, item_n, tok_sorted, dest_sorted,
      hn2, w_gate_up, w_gate_up, b_gate_up, b_gate_up, w_down, b_down)


def _combine_kernel(y0_ref, y1_ref, y2_ref, y3_ref, h_ref, gate_ref, g_ref, o_ref):
    h = h_ref[...]
    gates = gate_ref[...]
    for k, y_ref in enumerate((y0_ref, y1_ref, y2_ref, y3_ref)):
        h = h + y_ref[...] * gates[:, k:k + 1]
    o_ref[...] = h * lax.rsqrt(jnp.mean(h * h, axis=-1, keepdims=True) + RMS_EPS) * g_ref[...]


def _combine(y, h1, gates, gain, tm=256):
    slot_spec = lambda k: pl.BlockSpec((tm, D_MODEL), lambda i, k=k: (k * (N_TOK // tm) + i, 0))
    return pl.pallas_call(
        _combine_kernel,
        grid=(N_TOK // tm,),
        in_specs=[slot_spec(k) for k in range(TOP_K)] + [
            pl.BlockSpec((tm, D_MODEL), lambda i: (i, 0)),
            pl.BlockSpec((tm, TOP_K), lambda i: (i, 0)),
            pl.BlockSpec((1, D_MODEL), lambda i: (0, 0)),
        ],
        out_specs=pl.BlockSpec((tm, D_MODEL), lambda i: (i, 0)),
        out_shape=jax.ShapeDtypeStruct((N_TOK, D_MODEL), f32),
        compiler_params=_params(("parallel",)),
        name="combine_norm",
    )(y, y, y, y, h1, gates, gain)


def _route(top_idx):
    expert_flat = top_idx.reshape(-1)
    order = jnp.argsort(expert_flat, stable=True).astype(jnp.int32)
    counts = jnp.zeros((N_EXPERTS,), jnp.int32).at[expert_flat].add(1)
    starts = jnp.cumsum(counts) - counts
    tok_sorted = order // TOP_K
    dest_sorted = (order % TOP_K) * N_TOK + tok_sorted
    tok_sorted = jnp.pad(tok_sorted, (0, MOE_ROWS))
    dest_sorted = jnp.pad(dest_sorted, (0, MOE_ROWS))
    chunks = (counts + MOE_ROWS - 1) // MOE_ROWS
    chunk_end = jnp.cumsum(chunks)
    n_items = chunk_end[-1]
    w = jnp.arange(MOE_ITEMS, dtype=jnp.int32)
    w_eff = jnp.minimum(w, n_items - 1)
    e = jnp.sum(w_eff[:, None] >= chunk_end[None, :], axis=1).astype(jnp.int32)
    c = w_eff - (chunk_end - chunks)[e]
    item_start = starts[e] + c * MOE_ROWS
    item_n = jnp.where(w < n_items, jnp.clip(counts[e] - c * MOE_ROWS, 0, MOE_ROWS), 0)
    return n_items, e, item_start.astype(jnp.int32), item_n.astype(jnp.int32), tok_sorted, dest_sorted


def kernel(x, meta_tokens, rel_bias_table, attn_norm, w_in, fox_forget_bias, lam_q1, lam_k1, lam_q2, lam_k2,
           diff_subln, w_branch_fox, w_branch_diff, w_out, ffn_norm, w_router, b_router, w_gate_up, b_gate_up,
           w_down, b_down, final_norm):
    x2d = x.reshape(N_TOK, D_MODEL)
    w_t = jnp.swapaxes(w_in[0], 0, 1)
    fb_pad = jnp.pad(fox_forget_bias[0], (0, LANES - FOX_HEADS)).reshape(1, LANES)

    meta_pad = jnp.pad(meta_tokens, ((0, N_META_PAD - N_META), (0, 0)))
    proj, proj_meta, ff_real, ff_meta = _inproj(x2d, meta_pad, attn_norm, w_t, tm=1024, tn=512)

    ck, ckm = _forget_cumsum(ff_real, ff_meta, fb_pad)
    o_fox = _fox_attention(proj, proj_meta, ck, ckm)

    near, mbias = _bias_tiles(rel_bias_table)
    lam_vecs = jnp.concatenate([lam_q1, lam_k1, lam_q2, lam_k2], axis=0)
    o_diff = _diff_attention(proj, proj_meta, near, mbias, lam_vecs, diff_subln)

    merged = _gated_merge(o_fox, o_diff, w_branch_fox[0].astype(bf16), w_branch_diff[0].astype(bf16), proj)
    w_router_pad = jnp.pad(w_router[0], ((0, 0), (0, LANES - N_EXPERTS)))
    w_router_hi = w_router_pad.astype(bf16)
    w_router_lo = (w_router_pad - w_router_hi.astype(f32)).astype(bf16)
    w_router_split = jnp.concatenate([w_router_hi, w_router_lo], axis=1)
    b_router_pad = jnp.pad(b_router[0], (0, LANES - N_EXPERTS)).reshape(1, LANES)
    h1, hn2, gates_pad, idx_pad = _outproj(merged, w_out[0].astype(bf16), x2d, ffn_norm, w_router_split, b_router_pad)
    gates = gates_pad[:, :TOP_K]

    n_items, item_e, item_start, item_n, tok_sorted, dest_sorted = _route(idx_pad[:, :TOP_K])
    y = _moe(n_items, item_e, item_start, item_n, tok_sorted, dest_sorted, hn2,
             w_gate_up[0], b_gate_up[0].reshape(N_EXPERTS, 1, 2 * D_EXPERT),
             w_down[0], b_down[0].reshape(N_EXPERTS, 1, D_MODEL))
    out = _combine(y, h1, gates, final_norm.reshape(1, D_MODEL))
    return out.reshape(BATCH, SEQ, D_MODEL)
```

```python
import functools
import math

import jax
import jax.numpy as jnp
from jax import lax
from jax.experimental import pallas as pl
from jax.experimental.pallas import tpu as pltpu

D_MODEL = 2048
BATCH = 4
SEQ = 2048
N_TOK = BATCH * SEQ
N_META = 16
N_META_PAD = 128
HEAD_DIM = 128
FOX_HEADS = 8
DIFF_HEADS = 4
DIFF_V_DIM = 2 * HEAD_DIM
FOX_WIDTH = FOX_HEADS * HEAD_DIM
DIFF_QK_WIDTH = DIFF_HEADS * 2 * HEAD_DIM
DIFF_WIDTH = DIFF_HEADS * DIFF_V_DIM
N_BUCKETS = 32
MAX_DISTANCE = 128
N_EXPERTS = 32
TOP_K = 4
D_EXPERT = D_MODEL
SWIGLU_LIMIT = 7.0
SWIGLU_ALPHA = 1.702
RMS_EPS = 1e-5
NEG_INF = -1e30
LAMBDA_INIT = 0.8 - 0.6 * math.exp(-0.3 * 0)
ATTN_SCALE = HEAD_DIM ** -0.5
LOG2E = math.log2(math.e)
Q_PRESCALE = ATTN_SCALE * LOG2E

LANES = 128
SUBLANES = 8
VMEM_LIMIT_BYTES = 56 * 1024 * 1024
BIG_VMEM_LIMIT_BYTES = 60 * 1024 * 1024

COL_FQ = 0
COL_FK = COL_FQ + FOX_WIDTH
COL_FV = COL_FK + FOX_WIDTH
COL_DQ = COL_FV + FOX_WIDTH
COL_DK = COL_DQ + DIFF_QK_WIDTH
COL_DV = COL_DK + DIFF_QK_WIDTH
COL_GF = COL_DV + DIFF_WIDTH
COL_GD = COL_GF + D_MODEL
D_PROJ = COL_GD + D_MODEL

INPROJ_XT = 256
ATTN_TILE = 512
N_QBLK = SEQ // ATTN_TILE

MOE_ROWS = 1280
MOE_SUB = 256
MOE_HALF = MOE_SUB // 2
MOE_NSUB = MOE_ROWS // MOE_SUB
MOE_TF = 256
MOE_NF = D_EXPERT // MOE_TF
MOE_CHUNK = MOE_ROWS // (MOE_NF * MOE_NSUB)
N_ASSIGN = N_TOK * TOP_K
MOE_ITEMS = N_EXPERTS + N_ASSIGN // MOE_ROWS
Y_ROWS = N_ASSIGN + MOE_ROWS

f32 = jnp.float32
bf16 = jnp.bfloat16


def _params(sem, vmem=VMEM_LIMIT_BYTES):
    return pltpu.CompilerParams(dimension_semantics=sem, vmem_limit_bytes=vmem)


def _rms_bf16(x, g):
    return (x * lax.rsqrt(jnp.mean(x * x, axis=-1, keepdims=True) + RMS_EPS) * g).astype(bf16)


def _inproj_kernel(x_hbm, xm_ref, g_ref, wt_ref, wfft_ref, o_ref, om_ref, ff_ref, ffm_ref,
                   xn_ref, xnm_ref, xt_ref, wb_ref, wffb_ref, sem, *, tm, tn):
    j = pl.program_id(0)
    i = pl.program_id(1)
    rows = pl.ds(pl.multiple_of(i * tm, tm), tm)
    is_q = jnp.logical_or(j < COL_FK // tn, jnp.logical_and(j >= COL_DQ // tn, j < COL_DK // tn))
    qs = jnp.where(is_q, Q_PRESCALE, 1.0)

    @pl.when(jnp.logical_and(j == 0, i == 0))
    def _():
        wffb_ref[...] = wfft_ref[...].T.astype(bf16)
        xnm_ref[...] = _rms_bf16(xm_ref[...], g_ref[...])
        ffm_ref[...] = jnp.dot(xnm_ref[...], wffb_ref[...], preferred_element_type=f32)

    @pl.when(i == 0)
    def _():
        wb_ref[...] = wt_ref[...].T.astype(bf16)
        ym = jnp.dot(xnm_ref[...], wb_ref[...], preferred_element_type=f32)
        om_ref[...] = (ym * qs).astype(bf16)

    @pl.when(j == 0)
    def _():
        def piece(s):
            r0 = pl.multiple_of(i * tm + s * INPROJ_XT, INPROJ_XT)
            return pltpu.make_async_copy(x_hbm.at[pl.ds(r0, INPROJ_XT), :], xt_ref.at[s % 2], sem.at[s % 2])

        n_pieces = tm // INPROJ_XT
        piece(0).start()
        for s in range(n_pieces):
            if s + 1 < n_pieces:
                piece(s + 1).start()
            piece(s).wait()
            xn = _rms_bf16(xt_ref[s % 2], g_ref[...])
            xn_ref[pl.ds(pl.multiple_of(i * tm + s * INPROJ_XT, INPROJ_XT), INPROJ_XT), :] = xn
            ff_ref[s * INPROJ_XT:(s + 1) * INPROJ_XT, :] = jnp.dot(xn, wffb_ref[...], preferred_element_type=f32)

    y = jnp.dot(xn_ref[rows, :], wb_ref[...], preferred_element_type=f32)
    o_ref[...] = (y * qs).astype(bf16)


def _inproj(x2d, meta_pad, gain, w_t, tm, tn):
    n = x2d.shape[0]
    n_i = n // tm
    w_row = lambda j: SUBLANES * (j * (tn // SUBLANES) + jnp.where(j * tn >= COL_DQ, FOX_HEADS // SUBLANES, 0))
    first_pass_row = lambda j, i: jnp.where(j == 0, i, n_i - 1)
    return pl.pallas_call(
        functools.partial(_inproj_kernel, tm=tm, tn=tn),
        grid=(D_PROJ // tn, n_i),
        in_specs=[
            pl.BlockSpec(memory_space=pl.ANY),
            pl.BlockSpec((N_META_PAD, D_MODEL), lambda j, i: (0, 0)),
            pl.BlockSpec((1, D_MODEL), lambda j, i: (0, 0)),
            pl.BlockSpec((pl.Element(tn), pl.Element(D_MODEL)), lambda j, i: (w_row(j), 0)),
            pl.BlockSpec((pl.Element(LANES), pl.Element(D_MODEL)), lambda j, i: (COL_DQ, 0)),
        ],
        out_specs=[
            pl.BlockSpec((tm, tn), lambda j, i: (i, j)),
            pl.BlockSpec((N_META_PAD, tn), lambda j, i: (0, j)),
            pl.BlockSpec((tm, LANES), lambda j, i: (first_pass_row(j, i), 0)),
            pl.BlockSpec((N_META_PAD, LANES), lambda j, i: (0, 0)),
        ],
        out_shape=[
            jax.ShapeDtypeStruct((n, D_PROJ), bf16),
            jax.ShapeDtypeStruct((N_META_PAD, D_PROJ), bf16),
            jax.ShapeDtypeStruct((n, LANES), f32),
            jax.ShapeDtypeStruct((N_META_PAD, LANES), f32),
        ],
        scratch_shapes=[
            pltpu.VMEM((n, D_MODEL), bf16),
            pltpu.VMEM((N_META_PAD, D_MODEL), bf16),
            pltpu.VMEM((2, INPROJ_XT, D_MODEL), f32),
            pltpu.VMEM((D_MODEL, tn), bf16),
            pltpu.VMEM((D_MODEL, LANES), bf16),
            pltpu.SemaphoreType.DMA((2,)),
        ],
        compiler_params=_params(("arbitrary", "arbitrary"), BIG_VMEM_LIMIT_BYTES),
        name="inproj",
    )(x2d, meta_pad, gain, w_t, w_t)


def _log_sigmoid(x):
    return jnp.minimum(x, 0.0) - jnp.log(1.0 + jnp.exp(-jnp.abs(x)))


def _lane_cumsum(x):
    n = x.shape[-1]
    lane = lax.broadcasted_iota(jnp.int32, x.shape, x.ndim - 1)
    s = 1
    while s < n:
        x = x + jnp.where(lane >= s, pltpu.roll(x, s, x.ndim - 1), 0.0)
        s *= 2
    return x


def _cum_kernel(ff_ref, ffm_ref, fb_ref, ck_ref, ckm_ref):
    fb = fb_ref[...]
    row = lax.broadcasted_iota(jnp.int32, (LANES, LANES), 0)
    lfm = jnp.where(row < N_META, _log_sigmoid(ffm_ref[...] + fb), 0.0)
    cm = _lane_cumsum(lfm.T)
    m_total = cm[:, N_META - 1:N_META]
    lf = _log_sigmoid(ff_ref[...] + fb)
    cr = _lane_cumsum(lf.T) + m_total
    ck_ref[...] = cr[:SUBLANES] * LOG2E
    ckm_ref[...] = cm[:SUBLANES] * LOG2E


def _forget_cumsum(ff_real, ff_meta_pad, fb_pad):
    return pl.pallas_call(
        _cum_kernel,
        grid=(BATCH,),
        in_specs=[
            pl.BlockSpec((SEQ, LANES), lambda b: (b, 0)),
            pl.BlockSpec((LANES, LANES), lambda b: (0, 0)),
            pl.BlockSpec((1, LANES), lambda b: (0, 0)),
        ],
        out_specs=[
            pl.BlockSpec((None, SUBLANES, SEQ), lambda b: (b, 0, 0)),
            pl.BlockSpec((None, SUBLANES, LANES), lambda b: (b, 0, 0)),
        ],
        out_shape=[
            jax.ShapeDtypeStruct((BATCH, FOX_HEADS, SEQ), f32),
            jax.ShapeDtypeStruct((BATCH, FOX_HEADS, LANES), f32),
        ],
        compiler_params=_params(("parallel",)),
        name="forget_cumsum",
    )(ff_real, ff_meta_pad, fb_pad)


def _qk(q, k):
    return lax.dot_general(q, k, (((1,), (1,)), ((), ())), preferred_element_type=f32)


def _row_to_col(row):
    t = row.shape[-1]
    r = lax.broadcasted_iota(jnp.int32, (t, t), 0)
    c = lax.broadcasted_iota(jnp.int32, (t, t), 1)
    return jnp.sum(jnp.where(r == c, row, 0.0), axis=1, keepdims=True)


def _softmax_pieces(pieces, row_shift=None):
    m = None
    for s, _ in pieces:
        mi = jnp.max(s, axis=-1, keepdims=True)
        m = mi if m is None else jnp.maximum(m, mi)
    if row_shift is not None:
        m = (m + row_shift) - row_shift
    l = acc = None
    for s, v in pieces:
        p = jnp.exp2(s - m)
        li = jnp.sum(p, axis=-1, keepdims=True)
        ai = jnp.dot(p.astype(bf16), v, preferred_element_type=f32)
        l = li if l is None else l + li
        acc = ai if acc is None else acc + ai
    return acc / l


def _switch_qblock(i, branches):
    def pick(lo, hi):
        if lo == hi:
            return branches[lo]
        mid = (lo + hi) // 2
        return lambda: lax.cond(i <= mid, pick(lo, mid), pick(mid + 1, hi))
    pick(0, len(branches) - 1)()


FOX_GROUP = 2


def _fox_kernel(q_ref, k_ref, v_ref, km_ref, vm_ref, ck_ref, ckm_ref, o_ref):
    hg = pl.program_id(1)
    i = pl.program_id(2)
    t = ATTN_TILE
    q0 = pl.multiple_of(i * t, t)
    meta_ok = lax.broadcasted_iota(jnp.int32, (t, N_META_PAD), 1) < N_META
    r = lax.broadcasted_iota(jnp.int32, (t, t), 0)
    c = lax.broadcasted_iota(jnp.int32, (t, t), 1)

    def attend(n_far):
        def body():
            for g in range(FOX_GROUP):
                col = slice(g * HEAD_DIM, (g + 1) * HEAD_DIM)
                head = pl.ds(hg * FOX_GROUP + g, 1)
                q = q_ref[:, col]
                cq = _row_to_col(ck_ref[head, pl.ds(q0, t)])
                u_meta = jnp.where(meta_ok, _qk(q, km_ref[:, col]) - ckm_ref[head, :], NEG_INF)
                pieces = [(u_meta, vm_ref[:, col])]
                if n_far:
                    pieces.append((_qk(q, k_ref[0:n_far, col]) - ck_ref[head, 0:n_far], v_ref[0:n_far, col]))
                u_diag = _qk(q, k_ref[n_far:n_far + t, col]) - ck_ref[head, n_far:n_far + t]
                pieces.append((jnp.where(c <= r, u_diag, NEG_INF), v_ref[n_far:n_far + t, col]))
                o_ref[:, col] = _softmax_pieces(pieces, row_shift=cq).astype(bf16)
        return body

    _switch_qblock(i, [attend(ib * t) for ib in range(N_QBLK)])


def _fox_attention(proj, proj_meta, ck, ckm):
    t = ATTN_TILE
    w = FOX_GROUP * HEAD_DIM
    cb = lambda col: col // w
    return pl.pallas_call(
        _fox_kernel,
        grid=(BATCH, FOX_HEADS // FOX_GROUP, N_QBLK),
        in_specs=[
            pl.BlockSpec((t, w), lambda b, h, i: (b * N_QBLK + i, cb(COL_FQ) + h)),
            pl.BlockSpec((SEQ, w), lambda b, h, i: (b, cb(COL_FK) + h)),
            pl.BlockSpec((SEQ, w), lambda b, h, i: (b, cb(COL_FV) + h)),
            pl.BlockSpec((N_META_PAD, w), lambda b, h, i: (0, cb(COL_FK) + h)),
            pl.BlockSpec((N_META_PAD, w), lambda b, h, i: (0, cb(COL_FV) + h)),
            pl.BlockSpec((None, FOX_HEADS, SEQ), lambda b, h, i: (b, 0, 0)),
            pl.BlockSpec((None, FOX_HEADS, LANES), lambda b, h, i: (b, 0, 0)),
        ],
        out_specs=pl.BlockSpec((t, w), lambda b, h, i: (b * N_QBLK + i, h)),
        out_shape=jax.ShapeDtypeStruct((N_TOK, FOX_WIDTH), bf16),
        compiler_params=_params(("parallel", "parallel", "arbitrary")),
        name="fox_attention",
    )(proj, proj, proj, proj_meta, proj_meta, ck, ckm)


def _t5_bias(dist, table_ref, h):
    n = jnp.maximum(dist, 0)
    max_exact = N_BUCKETS // 2
    log_part = jnp.log(jnp.maximum(n, 1).astype(f32) / max_exact) / math.log(MAX_DISTANCE / max_exact)
    v = log_part * (N_BUCKETS - max_exact)
    far = table_ref[N_BUCKETS - 1, h]
    val = lambda b: (table_ref[b, h] - far) * LOG2E
    large = jnp.zeros(dist.shape, f32)
    for b in range(N_BUCKETS - 2, max_exact - 1, -1):
        large = jnp.where(v < b + 1 - max_exact, val(b), large)
    out = large
    for b in range(max_exact):
        out = jnp.where(n == b, val(b), out)
    return out


def _bias_kernel(table_ref, near_ref, meta_ref):
    h = pl.program_id(0)
    t = ATTN_TILE
    r = lax.broadcasted_iota(jnp.int32, (t, 2 * t), 0)
    c = lax.broadcasted_iota(jnp.int32, (t, 2 * t), 1)
    near_ref[...] = _t5_bias(r + t - c, table_ref, h)
    r = lax.broadcasted_iota(jnp.int32, (t, LANES), 0)
    c = lax.broadcasted_iota(jnp.int32, (t, LANES), 1)
    meta_ref[...] = _t5_bias(N_META + r - c, table_ref, h)


def _bias_tiles(table):
    t = ATTN_TILE
    return pl.pallas_call(
        _bias_kernel,
        grid=(DIFF_HEADS,),
        in_specs=[pl.BlockSpec(memory_space=pltpu.SMEM)],
        out_specs=[
            pl.BlockSpec((None, t, 2 * t), lambda h: (h, 0, 0)),
            pl.BlockSpec((None, t, LANES), lambda h: (h, 0, 0)),
        ],
        out_shape=[
            jax.ShapeDtypeStruct((DIFF_HEADS, t, 2 * t), f32),
            jax.ShapeDtypeStruct((DIFF_HEADS, t, LANES), f32),
        ],
        compiler_params=_params(("arbitrary",)),
        name="t5_bias_tiles",
    )(table)


def _diff_kernel(q1_ref, q2_ref, k1_ref, k2_ref, v_ref, k1m_ref, k2m_ref, vm_ref,
                 near_ref, mbias_ref, lam_ref, subln_ref, o_ref):
    i = pl.program_id(2)
    t = ATTN_TILE
    q1 = q1_ref[...]
    q2 = q2_ref[...]
    q0 = pl.multiple_of(i * t, t)
    lam = (jnp.exp(jnp.sum(lam_ref[0:1, :] * lam_ref[1:2, :], axis=-1, keepdims=True))
           - jnp.exp(jnp.sum(lam_ref[2:3, :] * lam_ref[3:4, :], axis=-1, keepdims=True))
           + LAMBDA_INIT)

    mb = jnp.where(i == 0, mbias_ref[...], 0.0)
    meta_ok = lax.broadcasted_iota(jnp.int32, (t, N_META_PAD), 1) < N_META
    vm = vm_ref[...]

    def attend(n_far, n_near):
        def body():
            r = lax.broadcasted_iota(jnp.int32, (t, n_near), 0)
            c = lax.broadcasted_iota(jnp.int32, (t, n_near), 1)
            mask = c <= r + (n_near - t)
            bias = near_ref[:, 2 * t - n_near:]
            outs = []
            for q, k_ref, km_ref in ((q1, k1_ref, k1m_ref), (q2, k2_ref, k2m_ref)):
                pieces = [(jnp.where(meta_ok, _qk(q, km_ref[...]) + mb, NEG_INF), vm)]
                if n_far:
                    pieces.append((_qk(q, k_ref[0:n_far, :]), v_ref[0:n_far, :]))
                s_near = jnp.where(mask, _qk(q, k_ref[n_far:n_far + n_near, :]) + bias, NEG_INF)
                pieces.append((s_near, v_ref[n_far:n_far + n_near, :]))
                outs.append(_softmax_pieces(pieces))
            o = outs[0] - lam * outs[1]
            y = o * lax.rsqrt(jnp.mean(o * o, axis=-1, keepdims=True) + RMS_EPS) * subln_ref[...]
            o_ref[...] = (y * (1.0 - LAMBDA_INIT)).astype(bf16)
        return body

    branches = []
    for ib in range(N_QBLK):
        n_near = min(2 * t, (ib + 1) * t)
        branches.append(attend((ib + 1) * t - n_near, n_near))
    _switch_qblock(i, branches)


def _diff_attention(proj, proj_meta, near, mbias, lam_vecs, subln):
    t = ATTN_TILE
    cb = lambda col: col // HEAD_DIM
    vb = lambda col: col // DIFF_V_DIM
    row = lambda b, h, i: b * N_QBLK + i
    return pl.pallas_call(
        _diff_kernel,
        grid=(BATCH, DIFF_HEADS, N_QBLK),
        in_specs=[
            pl.BlockSpec((t, HEAD_DIM), lambda b, h, i: (row(b, h, i), cb(COL_DQ) + 2 * h)),
            pl.BlockSpec((t, HEAD_DIM), lambda b, h, i: (row(b, h, i), cb(COL_DQ) + 2 * h + 1)),
            pl.BlockSpec((SEQ, HEAD_DIM), lambda b, h, i: (b, cb(COL_DK) + 2 * h)),
            pl.BlockSpec((SEQ, HEAD_DIM), lambda b, h, i: (b, cb(COL_DK) + 2 * h + 1)),
            pl.BlockSpec((SEQ, DIFF_V_DIM), lambda b, h, i: (b, vb(COL_DV) + h)),
            pl.BlockSpec((N_META_PAD, HEAD_DIM), lambda b, h, i: (0, cb(COL_DK) + 2 * h)),
            pl.BlockSpec((N_META_PAD, HEAD_DIM), lambda b, h, i: (0, cb(COL_DK) + 2 * h + 1)),
            pl.BlockSpec((N_META_PAD, DIFF_V_DIM), lambda b, h, i: (0, vb(COL_DV) + h)),
            pl.BlockSpec((None, t, 2 * t), lambda b, h, i: (h, 0, 0)),
            pl.BlockSpec((None, t, LANES), lambda b, h, i: (h, 0, 0)),
            pl.BlockSpec((4, HEAD_DIM), lambda b, h, i: (0, 0)),
            pl.BlockSpec((1, DIFF_V_DIM), lambda b, h, i: (0, 0)),
        ],
        out_specs=pl.BlockSpec((t, DIFF_V_DIM), lambda b, h, i: (row(b, h, i), h)),
        out_shape=jax.ShapeDtypeStruct((N_TOK, DIFF_WIDTH), bf16),
        compiler_params=_params(("parallel", "parallel", "arbitrary")),
        name="diff_attention",
    )(proj, proj, proj, proj, proj, proj_meta, proj_meta, proj_meta, near, mbias, lam_vecs, subln)


def _merge_kernel(of_ref, od_ref, wf_ref, wd_ref, gf_ref, gd_ref, o_ref):
    yf = jnp.dot(of_ref[...], wf_ref[...], preferred_element_type=f32)
    yd = jnp.dot(od_ref[...], wd_ref[...], preferred_element_type=f32)
    gf = jax.nn.sigmoid(gf_ref[...].astype(f32))
    gd = jax.nn.sigmoid(gd_ref[...].astype(f32))
    o_ref[...] = (gf * yf + gd * yd).astype(bf16)


def _gated_merge(o_fox, o_diff, w_bf, w_bd, proj, tm=1024, tn=1024):
    return pl.pallas_call(
        _merge_kernel,
        grid=(D_MODEL // tn, N_TOK // tm),
        in_specs=[
            pl.BlockSpec((tm, FOX_WIDTH), lambda j, i: (i, 0)),
            pl.BlockSpec((tm, DIFF_WIDTH), lambda j, i: (i, 0)),
            pl.BlockSpec((FOX_WIDTH, tn), lambda j, i: (0, j)),
            pl.BlockSpec((DIFF_WIDTH, tn), lambda j, i: (0, j)),
            pl.BlockSpec((tm, tn), lambda j, i: (i, COL_GF // tn + j)),
            pl.BlockSpec((tm, tn), lambda j, i: (i, COL_GD // tn + j)),
        ],
        out_specs=pl.BlockSpec((tm, tn), lambda j, i: (i, j)),
        out_shape=jax.ShapeDtypeStruct((N_TOK, D_MODEL), bf16),
        compiler_params=_params(("parallel", "arbitrary")),
        name="gated_merge",
    )(o_fox, o_diff, w_bf, w_bd, proj, proj)


def _outproj_kernel(m_ref, w_ref, x_ref, g_ref, wr_ref, br_ref, h_ref, hn_ref, gate_ref, idx_ref):
    h1 = x_ref[...] + jnp.dot(m_ref[...], w_ref[...], preferred_element_type=f32)
    h_ref[...] = h1
    hn = h1 * lax.rsqrt(jnp.mean(h1 * h1, axis=-1, keepdims=True) + RMS_EPS) * g_ref[...]
    hn_ref[...] = hn
    hn_hi = hn.astype(bf16)
    hn_lo = (hn - hn_hi.astype(f32)).astype(bf16)
    a = jnp.dot(hn_hi, wr_ref[...], preferred_element_type=f32)
    b = jnp.dot(hn_lo, wr_ref[:, :LANES], preferred_element_type=f32)
    lg = a[:, :LANES] + a[:, LANES:] + b + br_ref[...]

    lane = lax.broadcasted_iota(jnp.int32, lg.shape, 1)
    cur = jnp.where(lane < N_EXPERTS, lg, -jnp.inf)
    vals, idxs = [], []
    for _ in range(TOP_K):
        m = jnp.max(cur, axis=-1, keepdims=True)
        idx = jnp.min(jnp.where(cur == m, lane, LANES), axis=-1, keepdims=True)
        vals.append(m)
        idxs.append(idx)
        cur = jnp.where(lane == idx, -jnp.inf, cur)
    es = [jnp.exp(v - vals[0]) for v in vals]
    den = es[0] + es[1] + es[2] + es[3]
    gate_out = jnp.zeros(lg.shape, f32)
    idx_out = jnp.zeros(lg.shape, jnp.int32)
    for k in range(TOP_K):
        gate_out = jnp.where(lane == k, es[k] / den, gate_out)
        idx_out = jnp.where(lane == k, idxs[k], idx_out)
    gate_ref[...] = gate_out
    idx_ref[...] = idx_out


def _outproj(merged, w_out, x2d, gain, w_router_split, b_router_pad, tm=512):
    return pl.pallas_call(
        _outproj_kernel,
        grid=(N_TOK // tm,),
        in_specs=[
            pl.BlockSpec((tm, D_MODEL), lambda i: (i, 0)),
            pl.BlockSpec((D_MODEL, D_MODEL), lambda i: (0, 0)),
            pl.BlockSpec((tm, D_MODEL), lambda i: (i, 0)),
            pl.BlockSpec((1, D_MODEL), lambda i: (0, 0)),
            pl.BlockSpec((D_MODEL, 2 * LANES), lambda i: (0, 0)),
            pl.BlockSpec((1, LANES), lambda i: (0, 0)),
        ],
        out_specs=[
            pl.BlockSpec((tm, D_MODEL), lambda i: (i, 0)),
            pl.BlockSpec((tm, D_MODEL), lambda i: (i, 0)),
            pl.BlockSpec((tm, LANES), lambda i: (i, 0)),
            pl.BlockSpec((tm, LANES), lambda i: (i, 0)),
        ],
        out_shape=[
            jax.ShapeDtypeStruct((N_TOK, D_MODEL), f32),
            jax.ShapeDtypeStruct((N_TOK, D_MODEL), f32),
            jax.ShapeDtypeStruct((N_TOK, LANES), f32),
            jax.ShapeDtypeStruct((N_TOK, LANES), jnp.int32),
        ],
        compiler_params=_params(("parallel",)),
        name="outproj_router",
    )(merged, w_out, x2d, gain, w_router_split, b_router_pad)


def _moe_kernel(item_e_ref, item_start_ref, item_n_ref, tok_ref, dest_ref,
                hn_hbm, wg_ref, wl_ref, bg_ref, bl_ref, wd_ref, bd_ref, y_hbm,
                acc_ref, xg_ref, xb_ref, wgb_ref, wlb_ref, wdb_ref, gsem, ssem):
    w = pl.program_id(0)
    t = pl.program_id(1)
    last_w = pl.num_programs(0) - 1
    n = item_n_ref[w]
    start = item_start_ref[w]
    slot = lax.rem(w, 2)
    w_next = jnp.minimum(w + 1, last_w)
    start_next = item_start_ref[w_next]
    w_prev = jnp.maximum(w - 1, 0)
    start_prev = item_start_ref[w_prev]
    n_prev = jnp.where(w > 0, item_n_ref[w_prev], 0)
    is_last_live = jnp.logical_or(w == last_w, item_n_ref[w_next] == 0)

    def gather_row(row, base):
        tk = tok_ref[base + row]
        pltpu.make_async_copy(hn_hbm.at[pl.ds(tk, 1), :], xg_ref.at[pl.ds(row, 1), :], gsem).start()

    def gather_wait():
        pltpu.make_async_copy(hn_hbm.at[pl.ds(0, MOE_ROWS), :], xg_ref, gsem).wait()

    def scatter_row(row, base, cnt, sl, priority=0):
        d = jnp.where(row < cnt, dest_ref[base + row], N_ASSIGN + row)
        pltpu.make_async_copy(acc_ref.at[sl, pl.ds(row, 1), :], y_hbm.at[pl.ds(d, 1), :],
                              ssem).start(priority=priority)

    def scatter_wait(sl):
        pltpu.make_async_copy(acc_ref.at[sl], y_hbm.at[pl.ds(0, MOE_ROWS), :], ssem).wait()

    @pl.when(jnp.logical_and(w == 0, t == 0))
    def _():
        acc_ref[1] = jnp.zeros((MOE_ROWS, D_MODEL), f32)

        def issue(r, c):
            gather_row(r, start)
            return c
        lax.fori_loop(0, MOE_ROWS, issue, 0, unroll=8)

    @pl.when(jnp.logical_and(t == 0, n > 0))
    def _():
        gather_wait()
        xb_ref[...] = xg_ref[...].astype(bf16)
        acc_ref[slot] = jnp.broadcast_to(bd_ref[...], (MOE_ROWS, D_MODEL))

    def dma_chunk(r):
        row0 = t * (MOE_NSUB * MOE_CHUNK) + r * MOE_CHUNK
        for k in range(MOE_CHUNK):
            gather_row(row0 + k, start_next)
        for k in range(MOE_CHUNK):
            scatter_row(row0 + k, start_prev, n_prev, 1 - slot, priority=k % 2)

    def ffn_up(r0, rows):
        xs = xb_ref[r0:r0 + rows, :]
        hg = jnp.dot(xs, wgb_ref[...], preferred_element_type=f32) + bg_ref[...]
        hl = jnp.dot(xs, wlb_ref[...], preferred_element_type=f32) + bl_ref[...]
        return hg, hl

    def ffn_down(r0, rows, hg, hl):
        hg = jnp.minimum(hg, SWIGLU_LIMIT)
        hl = jnp.clip(hl, -SWIGLU_LIMIT, SWIGLU_LIMIT)
        act = hg * jax.nn.sigmoid(SWIGLU_ALPHA * hg) * (hl + 1.0)
        acc_ref[slot, r0:r0 + rows, :] += jnp.dot(act.astype(bf16), wdb_ref[...], preferred_element_type=f32)

    def ffn_rows(r0, rows):
        ffn_down(r0, rows, *ffn_up(r0, rows))

    @pl.when(n > 0)
    def _():
        nb_full = n // MOE_SUB
        rem = n - nb_full * MOE_SUB
        ext = jnp.logical_and(jnp.logical_and(rem > 0, rem <= MOE_HALF), nb_full >= 1)
        own_block = jnp.logical_or(rem > MOE_HALF, jnp.logical_and(rem > 0, nb_full == 0))
        nb = nb_full + jnp.where(own_block, 1, 0)
        tiny = jnp.logical_and(nb_full == 0, rem <= MOE_HALF)

        def cast_weights(r):
            if r == 0:
                wgb_ref[...] = wg_ref[...].astype(bf16)
                wlb_ref[...] = wl_ref[...].astype(bf16)
                wdb_ref[...] = wd_ref[...].astype(bf16)

        def single(r, rows, chunks=1):
            def body():
                for c in range(chunks):
                    dma_chunk(r + c)
                cast_weights(r)
                ffn_rows(r * MOE_SUB, rows)
            return body

        def pair(r, rows_b):
            def body():
                dma_chunk(r)
                dma_chunk(r + 1)
                cast_weights(r)
                ra, rb = r * MOE_SUB, (r + 1) * MOE_SUB
                ha = ffn_up(ra, MOE_SUB)
                hb = ffn_up(rb, rows_b)
                ffn_down(ra, MOE_SUB, *ha)
                ffn_down(rb, rows_b, *hb)
            return body

        def idle(*rs):
            def body():
                for r in rs:
                    dma_chunk(r)
            return body

        def last_single(r):
            return lambda: lax.cond(
                ext, single(r, MOE_SUB + MOE_HALF, 2),
                lambda: lax.cond(tiny, single(r, MOE_HALF, 2), single(r, MOE_SUB, 2)))

        for r in range(0, MOE_NSUB - 1, 2):
            is_last_pair = jnp.logical_and(ext, nb == r + 2)
            lax.cond(
                nb >= r + 2,
                lambda r=r, is_last_pair=is_last_pair: lax.cond(
                    is_last_pair, pair(r, MOE_SUB + MOE_HALF), pair(r, MOE_SUB)),
                lambda r=r: lax.cond(nb == r + 1, last_single(r), idle(r, r + 1)))
        for r in range(MOE_NSUB - MOE_NSUB % 2, MOE_NSUB):
            lax.cond(nb == r + 1, single(r, MOE_SUB), idle(r))

    @pl.when(jnp.logical_and(t == MOE_NF - 1, n > 0))
    def _():
        scatter_wait(1 - slot)

        @pl.when(is_last_live)
        def _():
            def issue(r, c):
                scatter_row(r, start, n, slot)
                return c
            lax.fori_loop(0, MOE_ROWS, issue, 0, unroll=8)
            scatter_wait(slot)
            gather_wait()


def _moe(n_items, item_e, item_start, item_n, tok_sorted, dest_sorted, hn2, w_gate_up, b_gate_up, w_down, b_down):
    tf = MOE_TF
    live_t = lambda w, t, n: jnp.where(n[w] > 0, t, MOE_NF - 1)
    grid_spec = pltpu.PrefetchScalarGridSpec(
        num_scalar_prefetch=5,
        grid=(n_items, MOE_NF),
        in_specs=[
            pl.BlockSpec(memory_space=pl.ANY),
            pl.BlockSpec((None, D_MODEL, tf), lambda w, t, e, s, n, tk, ds: (e[w], 0, live_t(w, t, n))),
            pl.BlockSpec((None, D_MODEL, tf), lambda w, t, e, s, n, tk, ds: (e[w], 0, MOE_NF + live_t(w, t, n))),
            pl.BlockSpec((None, 1, tf), lambda w, t, e, s, n, tk, ds: (e[w], 0, live_t(w, t, n))),
            pl.BlockSpec((None, 1, tf), lambda w, t, e, s, n, tk, ds: (e[w], 0, MOE_NF + live_t(w, t, n))),
            pl.BlockSpec((None, tf, D_MODEL), lambda w, t, e, s, n, tk, ds: (e[w], live_t(w, t, n), 0)),
            pl.BlockSpec((None, 1, D_MODEL), lambda w, t, e, s, n, tk, ds: (e[w], 0, 0)),
        ],
        out_specs=pl.BlockSpec(memory_space=pl.ANY),
        scratch_shapes=[
            pltpu.VMEM((2, MOE_ROWS, D_MODEL), f32),
            pltpu.VMEM((MOE_ROWS, D_MODEL), f32),
            pltpu.VMEM((MOE_ROWS, D_MODEL), bf16),
            pltpu.VMEM((D_MODEL, tf), bf16),
            pltpu.VMEM((D_MODEL, tf), bf16),
            pltpu.VMEM((tf, D_MODEL), bf16),
            pltpu.SemaphoreType.DMA,
            pltpu.SemaphoreType.DMA,
        ],
    )
    return pl.pallas_call(
        _moe_kernel,
        grid_spec=grid_spec,
        out_shape=jax.ShapeDtypeStruct((Y_ROWS, D_MODEL), f32),
        compiler_params=_params(("arbitrary", "arbitrary"), BIG_VMEM_LIMIT_BYTES),
        name="moe_experts",
    )(item_e, item_start, item_n, tok_sorted, dest_sorted,
      hn2, w_gate_up, w_gate_up, b_gate_up, b_gate_up, w_down, b_down)


def _combine_kernel(y0_ref, y1_ref, y2_ref, y3_ref, h_ref, gate_ref, g_ref, o_ref):
    h = h_ref[...]
    gates = gate_ref[...]
    for k, y_ref in enumerate((y0_ref, y1_ref, y2_ref, y3_ref)):
        h = h + y_ref[...] * gates[:, k:k + 1]
    o_ref[...] = h * lax.rsqrt(jnp.mean(h * h, axis=-1, keepdims=True) + RMS_EPS) * g_ref[...]


def _combine(y, h1, gates, gain, tm=512):
    slot_spec = lambda k: pl.BlockSpec((tm, D_MODEL), lambda i, k=k: (k * (N_TOK // tm) + i, 0))
    return pl.pallas_call(
        _combine_kernel,
        grid=(N_TOK // tm,),
        in_specs=[slot_spec(k) for k in range(TOP_K)] + [
            pl.BlockSpec((tm, D_MODEL), lambda i: (i, 0)),
            pl.BlockSpec((tm, TOP_K), lambda i: (i, 0)),
            pl.BlockSpec((1, D_MODEL), lambda i: (0, 0)),
        ],
        out_specs=pl.BlockSpec((tm, D_MODEL), lambda i: (i, 0)),
        out_shape=jax.ShapeDtypeStruct((N_TOK, D_MODEL), f32),
        compiler_params=_params(("parallel",)),
        name="combine_norm",
    )(y, y, y, y, h1, gates, gain)


def _route(top_idx):
    expert_flat = top_idx.reshape(-1)
    order = jnp.argsort(expert_flat, stable=True).astype(jnp.int32)
    counts = jnp.zeros((N_EXPERTS,), jnp.int32).at[expert_flat].add(1)
    starts = jnp.cumsum(counts) - counts
    tok_sorted = order // TOP_K
    dest_sorted = (order % TOP_K) * N_TOK + tok_sorted
    tok_sorted = jnp.pad(tok_sorted, (0, MOE_ROWS))
    dest_sorted = jnp.pad(dest_sorted, (0, MOE_ROWS))
    chunks = (counts + MOE_ROWS - 1) // MOE_ROWS
    chunk_end = jnp.cumsum(chunks)
    n_items = chunk_end[-1]
    w = jnp.arange(MOE_ITEMS, dtype=jnp.int32)
    w_eff = jnp.minimum(w, n_items - 1)
    e = jnp.sum(w_eff[:, None] >= chunk_end[None, :], axis=1).astype(jnp.int32)
    c = w_eff - (chunk_end - chunks)[e]
    item_start = starts[e] + c * MOE_ROWS
    item_n = jnp.where(w < n_items, jnp.clip(counts[e] - c * MOE_ROWS, 0, MOE_ROWS), 0)
    return n_items, e, item_start.astype(jnp.int32), item_n.astype(jnp.int32), tok_sorted, dest_sorted


def kernel(x, meta_tokens, rel_bias_table, attn_norm, w_in, fox_forget_bias, lam_q1, lam_k1, lam_q2, lam_k2,
           diff_subln, w_branch_fox, w_branch_diff, w_out, ffn_norm, w_router, b_router, w_gate_up, b_gate_up,
           w_down, b_down, final_norm):
    x2d = x.reshape(N_TOK, D_MODEL)
    w_t = jnp.swapaxes(w_in[0], 0, 1)
    fb_pad = jnp.pad(fox_forget_bias[0], (0, LANES - FOX_HEADS)).reshape(1, LANES)

    meta_pad = jnp.pad(meta_tokens, ((0, N_META_PAD - N_META), (0, 0)))
    proj, proj_meta, ff_real, ff_meta = _inproj(x2d, meta_pad, attn_norm, w_t, tm=1024, tn=512)

    ck, ckm = _forget_cumsum(ff_real, ff_meta, fb_pad)
    o_fox = _fox_attention(proj, proj_meta, ck, ckm)

    near, mbias = _bias_tiles(rel_bias_table)
    lam_vecs = jnp.concatenate([lam_q1, lam_k1, lam_q2, lam_k2], axis=0)
    o_diff = _diff_attention(proj, proj_meta, near, mbias, lam_vecs, diff_subln)

    merged = _gated_merge(o_fox, o_diff, w_branch_fox[0].astype(bf16), w_branch_diff[0].astype(bf16), proj)
    w_router_pad = jnp.pad(w_router[0], ((0, 0), (0, LANES - N_EXPERTS)))
    w_router_hi = w_router_pad.astype(bf16)
    w_router_lo = (w_router_pad - w_router_hi.astype(f32)).astype(bf16)
    w_router_split = jnp.concatenate([w_router_hi, w_router_lo], axis=1)
    b_router_pad = jnp.pad(b_router[0], (0, LANES - N_EXPERTS)).reshape(1, LANES)
    h1, hn2, gates_pad, idx_pad = _outproj(merged, w_out[0].astype(bf16), x2d, ffn_norm, w_router_split, b_router_pad)
    gates = gates_pad[:, :TOP_K]

    n_items, item_e, item_start, item_n, tok_sorted, dest_sorted = _route(idx_pad[:, :TOP_K])
    y = _moe(n_items, item_e, item_start, item_n, tok_sorted, dest_sorted, hn2,
             w_gate_up[0], b_gate_up[0].reshape(N_EXPERTS, 1, 2 * D_EXPERT),
             w_down[0], b_down[0].reshape(N_EXPERTS, 1, D_MODEL))
    out = _combine(y, h1, gates, final_norm.reshape(1, D_MODEL))
    return out.reshape(BATCH, SEQ, D_MODEL)
```

```python
import functools
import math

import jax
import jax.numpy as jnp
from jax import lax
from jax.experimental import pallas as pl
from jax.experimental.pallas import tpu as pltpu

D_MODEL = 2048
BATCH = 4
SEQ = 2048
N_TOK = BATCH * SEQ
N_META = 16
N_META_PAD = 128
HEAD_DIM = 128
FOX_HEADS = 8
DIFF_HEADS = 4
DIFF_V_DIM = 2 * HEAD_DIM
FOX_WIDTH = FOX_HEADS * HEAD_DIM
DIFF_QK_WIDTH = DIFF_HEADS * 2 * HEAD_DIM
DIFF_WIDTH = DIFF_HEADS * DIFF_V_DIM
N_BUCKETS = 32
MAX_DISTANCE = 128
N_EXPERTS = 32
TOP_K = 4
D_EXPERT = D_MODEL
SWIGLU_LIMIT = 7.0
SWIGLU_ALPHA = 1.702
RMS_EPS = 1e-5
NEG_INF = -1e30
LAMBDA_INIT = 0.8 - 0.6 * math.exp(-0.3 * 0)
ATTN_SCALE = HEAD_DIM ** -0.5
LOG2E = math.log2(math.e)
Q_PRESCALE = ATTN_SCALE * LOG2E

LANES = 128
SUBLANES = 8
VMEM_LIMIT_BYTES = 56 * 1024 * 1024
BIG_VMEM_LIMIT_BYTES = 60 * 1024 * 1024

COL_FQ = 0
COL_FK = COL_FQ + FOX_WIDTH
COL_FV = COL_FK + FOX_WIDTH
COL_DQ = COL_FV + FOX_WIDTH
COL_DK = COL_DQ + DIFF_QK_WIDTH
COL_DV = COL_DK + DIFF_QK_WIDTH
COL_GF = COL_DV + DIFF_WIDTH
COL_GD = COL_GF + D_MODEL
D_PROJ = COL_GD + D_MODEL

INPROJ_XT = 256
ATTN_TILE = 512
N_QBLK = SEQ // ATTN_TILE

MOE_ROWS = 1280
MOE_SUB = 256
MOE_HALF = MOE_SUB // 2
MOE_NSUB = MOE_ROWS // MOE_SUB
MOE_TF = 256
MOE_NF = D_EXPERT // MOE_TF
MOE_CHUNK = MOE_ROWS // (MOE_NF * MOE_NSUB)
N_ASSIGN = N_TOK * TOP_K
MOE_ITEMS = N_EXPERTS + N_ASSIGN // MOE_ROWS
Y_ROWS = N_ASSIGN + MOE_ROWS

f32 = jnp.float32
bf16 = jnp.bfloat16


def _params(sem, vmem=VMEM_LIMIT_BYTES):
    return pltpu.CompilerParams(dimension_semantics=sem, vmem_limit_bytes=vmem)


def _rms_bf16(x, g):
    return (x * lax.rsqrt(jnp.mean(x * x, axis=-1, keepdims=True) + RMS_EPS) * g).astype(bf16)


def _inproj_kernel(x_hbm, xm_ref, g_ref, wt_ref, wfft_ref, o_ref, om_ref, ff_ref, ffm_ref,
                   xn_ref, xnm_ref, xt_ref, wb_ref, wffb_ref, sem, *, tm, tn):
    j = pl.program_id(0)
    i = pl.program_id(1)
    rows = pl.ds(pl.multiple_of(i * tm, tm), tm)
    is_q = jnp.logical_or(j < COL_FK // tn, jnp.logical_and(j >= COL_DQ // tn, j < COL_DK // tn))
    qs = jnp.where(is_q, Q_PRESCALE, 1.0)

    @pl.when(jnp.logical_and(j == 0, i == 0))
    def _():
        wffb_ref[...] = wfft_ref[...].T.astype(bf16)
        xnm_ref[...] = _rms_bf16(xm_ref[...], g_ref[...])
        ffm_ref[...] = jnp.dot(xnm_ref[...], wffb_ref[...], preferred_element_type=f32)

    @pl.when(i == 0)
    def _():
        wb_ref[...] = wt_ref[...].T.astype(bf16)
        ym = jnp.dot(xnm_ref[...], wb_ref[...], preferred_element_type=f32)
        om_ref[...] = (ym * qs).astype(bf16)

    @pl.when(j == 0)
    def _():
        def piece(s):
            r0 = pl.multiple_of(i * tm + s * INPROJ_XT, INPROJ_XT)
            return pltpu.make_async_copy(x_hbm.at[pl.ds(r0, INPROJ_XT), :], xt_ref.at[s % 2], sem.at[s % 2])

        n_pieces = tm // INPROJ_XT
        piece(0).start()
        for s in range(n_pieces):
            if s + 1 < n_pieces:
                piece(s + 1).start()
            piece(s).wait()
            xn = _rms_bf16(xt_ref[s % 2], g_ref[...])
            xn_ref[pl.ds(pl.multiple_of(i * tm + s * INPROJ_XT, INPROJ_XT), INPROJ_XT), :] = xn
            ff_ref[s * INPROJ_XT:(s + 1) * INPROJ_XT, :] = jnp.dot(xn, wffb_ref[...], preferred_element_type=f32)

    y = jnp.dot(xn_ref[rows, :], wb_ref[...], preferred_element_type=f32)
    o_ref[...] = (y * qs).astype(bf16)


def _inproj(x2d, meta_pad, gain, w_t, tm, tn):
    n = x2d.shape[0]
    n_i = n // tm
    w_row = lambda j: SUBLANES * (j * (tn // SUBLANES) + jnp.where(j * tn >= COL_DQ, FOX_HEADS // SUBLANES, 0))
    first_pass_row = lambda j, i: jnp.where(j == 0, i, n_i - 1)
    return pl.pallas_call(
        functools.partial(_inproj_kernel, tm=tm, tn=tn),
        grid=(D_PROJ // tn, n_i),
        in_specs=[
            pl.BlockSpec(memory_space=pl.ANY),
            pl.BlockSpec((N_META_PAD, D_MODEL), lambda j, i: (0, 0)),
            pl.BlockSpec((1, D_MODEL), lambda j, i: (0, 0)),
            pl.BlockSpec((pl.Element(tn), pl.Element(D_MODEL)), lambda j, i: (w_row(j), 0)),
            pl.BlockSpec((pl.Element(LANES), pl.Element(D_MODEL)), lambda j, i: (COL_DQ, 0)),
        ],
        out_specs=[
            pl.BlockSpec((tm, tn), lambda j, i: (i, j)),
            pl.BlockSpec((N_META_PAD, tn), lambda j, i: (0, j)),
            pl.BlockSpec((tm, LANES), lambda j, i: (first_pass_row(j, i), 0)),
            pl.BlockSpec((N_META_PAD, LANES), lambda j, i: (0, 0)),
        ],
        out_shape=[
            jax.ShapeDtypeStruct((n, D_PROJ), bf16),
            jax.ShapeDtypeStruct((N_META_PAD, D_PROJ), bf16),
            jax.ShapeDtypeStruct((n, LANES), f32),
            jax.ShapeDtypeStruct((N_META_PAD, LANES), f32),
        ],
        scratch_shapes=[
            pltpu.VMEM((n, D_MODEL), bf16),
            pltpu.VMEM((N_META_PAD, D_MODEL), bf16),
            pltpu.VMEM((2, INPROJ_XT, D_MODEL), f32),
            pltpu.VMEM((D_MODEL, tn), bf16),
            pltpu.VMEM((D_MODEL, LANES), bf16),
            pltpu.SemaphoreType.DMA((2,)),
        ],
        compiler_params=_params(("arbitrary", "arbitrary"), BIG_VMEM_LIMIT_BYTES),
        name="inproj",
    )(x2d, meta_pad, gain, w_t, w_t)


def _log_sigmoid(x):
    return jnp.minimum(x, 0.0) - jnp.log(1.0 + jnp.exp(-jnp.abs(x)))


def _lane_cumsum(x):
    n = x.shape[-1]
    lane = lax.broadcasted_iota(jnp.int32, x.shape, x.ndim - 1)
    s = 1
    while s < n:
        x = x + jnp.where(lane >= s, pltpu.roll(x, s, x.ndim - 1), 0.0)
        s *= 2
    return x


def _cum_kernel(ff_ref, ffm_ref, fb_ref, ck_ref, ckm_ref):
    fb = fb_ref[...]
    row = lax.broadcasted_iota(jnp.int32, (LANES, LANES), 0)
    lfm = jnp.where(row < N_META, _log_sigmoid(ffm_ref[...] + fb), 0.0)
    cm = _lane_cumsum(lfm.T)
    m_total = cm[:, N_META - 1:N_META]
    lf = _log_sigmoid(ff_ref[...] + fb)
    cr = _lane_cumsum(lf.T) + m_total
    ck_ref[...] = cr[:SUBLANES] * LOG2E
    ckm_ref[...] = cm[:SUBLANES] * LOG2E


def _forget_cumsum(ff_real, ff_meta_pad, fb_pad):
    return pl.pallas_call(
        _cum_kernel,
        grid=(BATCH,),
        in_specs=[
            pl.BlockSpec((SEQ, LANES), lambda b: (b, 0)),
            pl.BlockSpec((LANES, LANES), lambda b: (0, 0)),
            pl.BlockSpec((1, LANES), lambda b: (0, 0)),
        ],
        out_specs=[
            pl.BlockSpec((None, SUBLANES, SEQ), lambda b: (b, 0, 0)),
            pl.BlockSpec((None, SUBLANES, LANES), lambda b: (b, 0, 0)),
        ],
        out_shape=[
            jax.ShapeDtypeStruct((BATCH, FOX_HEADS, SEQ), f32),
            jax.ShapeDtypeStruct((BATCH, FOX_HEADS, LANES), f32),
        ],
        compiler_params=_params(("parallel",)),
        name="forget_cumsum",
    )(ff_real, ff_meta_pad, fb_pad)


def _qk(q, k):
    return lax.dot_general(q, k, (((1,), (1,)), ((), ())), preferred_element_type=f32)


def _row_to_col(row):
    t = row.shape[-1]
    r = lax.broadcasted_iota(jnp.int32, (t, t), 0)
    c = lax.broadcasted_iota(jnp.int32, (t, t), 1)
    return jnp.sum(jnp.where(r == c, row, 0.0), axis=1, keepdims=True)


def _softmax_pieces(pieces, row_shift=None):
    m = None
    for s, _ in pieces:
        mi = jnp.max(s, axis=-1, keepdims=True)
        m = mi if m is None else jnp.maximum(m, mi)
    if row_shift is not None:
        m = (m + row_shift) - row_shift
    l = acc = None
    for s, v in pieces:
        p = jnp.exp2(s - m)
        li = jnp.sum(p, axis=-1, keepdims=True)
        ai = jnp.dot(p.astype(bf16), v, preferred_element_type=f32)
        l = li if l is None else l + li
        acc = ai if acc is None else acc + ai
    return acc / l


def _switch_qblock(i, branches):
    def pick(lo, hi):
        if lo == hi:
            return branches[lo]
        mid = (lo + hi) // 2
        return lambda: lax.cond(i <= mid, pick(lo, mid), pick(mid + 1, hi))
    pick(0, len(branches) - 1)()


FOX_GROUP = 2


def _fox_kernel(q_ref, k_ref, v_ref, km_ref, vm_ref, ck_ref, ckm_ref, o_ref):
    hg = pl.program_id(1)
    i = pl.program_id(2)
    t = ATTN_TILE
    q0 = pl.multiple_of(i * t, t)
    meta_ok = lax.broadcasted_iota(jnp.int32, (t, N_META_PAD), 1) < N_META
    r = lax.broadcasted_iota(jnp.int32, (t, t), 0)
    c = lax.broadcasted_iota(jnp.int32, (t, t), 1)

    def attend(n_far):
        def body():
            for g in range(FOX_GROUP):
                col = slice(g * HEAD_DIM, (g + 1) * HEAD_DIM)
                head = pl.ds(hg * FOX_GROUP + g, 1)
                q = q_ref[:, col]
                cq = _row_to_col(ck_ref[head, pl.ds(q0, t)])
                u_meta = jnp.where(meta_ok, _qk(q, km_ref[:, col]) - ckm_ref[head, :], NEG_INF)
                pieces = [(u_meta, vm_ref[:, col])]
                if n_far:
                    pieces.append((_qk(q, k_ref[0:n_far, col]) - ck_ref[head, 0:n_far], v_ref[0:n_far, col]))
                u_diag = _qk(q, k_ref[n_far:n_far + t, col]) - ck_ref[head, n_far:n_far + t]
                pieces.append((jnp.where(c <= r, u_diag, NEG_INF), v_ref[n_far:n_far + t, col]))
                o_ref[:, col] = _softmax_pieces(pieces, row_shift=cq).astype(bf16)
        return body

    _switch_qblock(i, [attend(ib * t) for ib in range(N_QBLK)])


def _fox_attention(proj, proj_meta, ck, ckm):
    t = ATTN_TILE
    w = FOX_GROUP * HEAD_DIM
    cb = lambda col: col // w
    return pl.pallas_call(
        _fox_kernel,
        grid=(BATCH, FOX_HEADS // FOX_GROUP, N_QBLK),
        in_specs=[
            pl.BlockSpec((t, w), lambda b, h, i: (b * N_QBLK + i, cb(COL_FQ) + h)),
            pl.BlockSpec((SEQ, w), lambda b, h, i: (b, cb(COL_FK) + h)),
            pl.BlockSpec((SEQ, w), lambda b, h, i: (b, cb(COL_FV) + h)),
            pl.BlockSpec((N_META_PAD, w), lambda b, h, i: (0, cb(COL_FK) + h)),
            pl.BlockSpec((N_META_PAD, w), lambda b, h, i: (0, cb(COL_FV) + h)),
            pl.BlockSpec((None, FOX_HEADS, SEQ), lambda b, h, i: (b, 0, 0)),
            pl.BlockSpec((None, FOX_HEADS, LANES), lambda b, h, i: (b, 0, 0)),
        ],
        out_specs=pl.BlockSpec((t, w), lambda b, h, i: (b * N_QBLK + i, h)),
        out_shape=jax.ShapeDtypeStruct((N_TOK, FOX_WIDTH), bf16),
        compiler_params=_params(("parallel", "parallel", "arbitrary")),
        name="fox_attention",
    )(proj, proj, proj, proj_meta, proj_meta, ck, ckm)


def _t5_bias(dist, table_ref, h):
    n = jnp.maximum(dist, 0)
    max_exact = N_BUCKETS // 2
    log_part = jnp.log(jnp.maximum(n, 1).astype(f32) / max_exact) / math.log(MAX_DISTANCE / max_exact)
    v = log_part * (N_BUCKETS - max_exact)
    far = table_ref[N_BUCKETS - 1, h]
    val = lambda b: (table_ref[b, h] - far) * LOG2E
    large = jnp.zeros(dist.shape, f32)
    for b in range(N_BUCKETS - 2, max_exact - 1, -1):
        large = jnp.where(v < b + 1 - max_exact, val(b), large)
    out = large
    for b in range(max_exact):
        out = jnp.where(n == b, val(b), out)
    return out


def _bias_kernel(table_ref, near_ref, meta_ref):
    h = pl.program_id(0)
    t = ATTN_TILE
    r = lax.broadcasted_iota(jnp.int32, (t, 2 * t), 0)
    c = lax.broadcasted_iota(jnp.int32, (t, 2 * t), 1)
    near_ref[...] = _t5_bias(r + t - c, table_ref, h)
    r = lax.broadcasted_iota(jnp.int32, (t, LANES), 0)
    c = lax.broadcasted_iota(jnp.int32, (t, LANES), 1)
    meta_ref[...] = _t5_bias(N_META + r - c, table_ref, h)


def _bias_tiles(table):
    t = ATTN_TILE
    return pl.pallas_call(
        _bias_kernel,
        grid=(DIFF_HEADS,),
        in_specs=[pl.BlockSpec(memory_space=pltpu.SMEM)],
        out_specs=[
            pl.BlockSpec((None, t, 2 * t), lambda h: (h, 0, 0)),
            pl.BlockSpec((None, t, LANES), lambda h: (h, 0, 0)),
        ],
        out_shape=[
            jax.ShapeDtypeStruct((DIFF_HEADS, t, 2 * t), f32),
            jax.ShapeDtypeStruct((DIFF_HEADS, t, LANES), f32),
        ],
        compiler_params=_params(("arbitrary",)),
        name="t5_bias_tiles",
    )(table)


def _diff_kernel(q1_ref, q2_ref, k1_ref, k2_ref, v_ref, k1m_ref, k2m_ref, vm_ref,
                 near_ref, mbias_ref, lam_ref, subln_ref, o_ref):
    i = pl.program_id(2)
    t = ATTN_TILE
    q1 = q1_ref[...]
    q2 = q2_ref[...]
    q0 = pl.multiple_of(i * t, t)
    lam = (jnp.exp(jnp.sum(lam_ref[0:1, :] * lam_ref[1:2, :], axis=-1, keepdims=True))
           - jnp.exp(jnp.sum(lam_ref[2:3, :] * lam_ref[3:4, :], axis=-1, keepdims=True))
           + LAMBDA_INIT)

    mb = jnp.where(i == 0, mbias_ref[...], 0.0)
    meta_ok = lax.broadcasted_iota(jnp.int32, (t, N_META_PAD), 1) < N_META
    vm = vm_ref[...]

    def attend(n_far, n_near):
        def body():
            r = lax.broadcasted_iota(jnp.int32, (t, n_near), 0)
            c = lax.broadcasted_iota(jnp.int32, (t, n_near), 1)
            mask = c <= r + (n_near - t)
            bias = near_ref[:, 2 * t - n_near:]
            outs = []
            for q, k_ref, km_ref in ((q1, k1_ref, k1m_ref), (q2, k2_ref, k2m_ref)):
                pieces = [(jnp.where(meta_ok, _qk(q, km_ref[...]) + mb, NEG_INF), vm)]
                if n_far:
                    pieces.append((_qk(q, k_ref[0:n_far, :]), v_ref[0:n_far, :]))
                s_near = jnp.where(mask, _qk(q, k_ref[n_far:n_far + n_near, :]) + bias, NEG_INF)
                pieces.append((s_near, v_ref[n_far:n_far + n_near, :]))
                outs.append(_softmax_pieces(pieces))
            o = outs[0] - lam * outs[1]
            y = o * lax.rsqrt(jnp.mean(o * o, axis=-1, keepdims=True) + RMS_EPS) * subln_ref[...]
            o_ref[...] = (y * (1.0 - LAMBDA_INIT)).astype(bf16)
        return body

    branches = []
    for ib in range(N_QBLK):
        n_near = min(2 * t, (ib + 1) * t)
        branches.append(attend((ib + 1) * t - n_near, n_near))
    _switch_qblock(i, branches)


def _diff_attention(proj, proj_meta, near, mbias, lam_vecs, subln):
    t = ATTN_TILE
    cb = lambda col: col // HEAD_DIM
    vb = lambda col: col // DIFF_V_DIM
    row = lambda b, h, i: b * N_QBLK + i
    return pl.pallas_call(
        _diff_kernel,
        grid=(BATCH, DIFF_HEADS, N_QBLK),
        in_specs=[
            pl.BlockSpec((t, HEAD_DIM), lambda b, h, i: (row(b, h, i), cb(COL_DQ) + 2 * h)),
            pl.BlockSpec((t, HEAD_DIM), lambda b, h, i: (row(b, h, i), cb(COL_DQ) + 2 * h + 1)),
            pl.BlockSpec((SEQ, HEAD_DIM), lambda b, h, i: (b, cb(COL_DK) + 2 * h)),
            pl.BlockSpec((SEQ, HEAD_DIM), lambda b, h, i: (b, cb(COL_DK) + 2 * h + 1)),
            pl.BlockSpec((SEQ, DIFF_V_DIM), lambda b, h, i: (b, vb(COL_DV) + h)),
            pl.BlockSpec((N_META_PAD, HEAD_DIM), lambda b, h, i: (0, cb(COL_DK) + 2 * h)),
            pl.BlockSpec((N_META_PAD, HEAD_DIM), lambda b, h, i: (0, cb(COL_DK) + 2 * h + 1)),
            pl.BlockSpec((N_META_PAD, DIFF_V_DIM), lambda b, h, i: (0, vb(COL_DV) + h)),
            pl.BlockSpec((None, t, 2 * t), lambda b, h, i: (h, 0, 0)),
            pl.BlockSpec((None, t, LANES), lambda b, h, i: (h, 0, 0)),
            pl.BlockSpec((4, HEAD_DIM), lambda b, h, i: (0, 0)),
            pl.BlockSpec((1, DIFF_V_DIM), lambda b, h, i: (0, 0)),
        ],
        out_specs=pl.BlockSpec((t, DIFF_V_DIM), lambda b, h, i: (row(b, h, i), h)),
        out_shape=jax.ShapeDtypeStruct((N_TOK, DIFF_WIDTH), bf16),
        compiler_params=_params(("parallel", "parallel", "arbitrary")),
        name="diff_attention",
    )(proj, proj, proj, proj, proj, proj_meta, proj_meta, proj_meta, near, mbias, lam_vecs, subln)


def _merge_kernel(of_ref, od_ref, wf_ref, wd_ref, gf_ref, gd_ref, o_ref):
    yf = jnp.dot(of_ref[...], wf_ref[...], preferred_element_type=f32)
    yd = jnp.dot(od_ref[...], wd_ref[...], preferred_element_type=f32)
    gf = jax.nn.sigmoid(gf_ref[...].astype(f32))
    gd = jax.nn.sigmoid(gd_ref[...].astype(f32))
    o_ref[...] = (gf * yf + gd * yd).astype(bf16)


def _gated_merge(o_fox, o_diff, w_bf, w_bd, proj, tm=1024, tn=1024):
    return pl.pallas_call(
        _merge_kernel,
        grid=(D_MODEL // tn, N_TOK // tm),
        in_specs=[
            pl.BlockSpec((tm, FOX_WIDTH), lambda j, i: (i, 0)),
            pl.BlockSpec((tm, DIFF_WIDTH), lambda j, i: (i, 0)),
            pl.BlockSpec((FOX_WIDTH, tn), lambda j, i: (0, j)),
            pl.BlockSpec((DIFF_WIDTH, tn), lambda j, i: (0, j)),
            pl.BlockSpec((tm, tn), lambda j, i: (i, COL_GF // tn + j)),
            pl.BlockSpec((tm, tn), lambda j, i: (i, COL_GD // tn + j)),
        ],
        out_specs=pl.BlockSpec((tm, tn), lambda j, i: (i, j)),
        out_shape=jax.ShapeDtypeStruct((N_TOK, D_MODEL), bf16),
        compiler_params=_params(("parallel", "arbitrary")),
        name="gated_merge",
    )(o_fox, o_diff, w_bf, w_bd, proj, proj)


def _outproj_kernel(m_ref, w_ref, x_ref, g_ref, wr_ref, br_ref, h_ref, hn_ref, gate_ref, idx_ref):
    h1 = x_ref[...] + jnp.dot(m_ref[...], w_ref[...], preferred_element_type=f32)
    h_ref[...] = h1
    hn = h1 * lax.rsqrt(jnp.mean(h1 * h1, axis=-1, keepdims=True) + RMS_EPS) * g_ref[...]
    hn_ref[...] = hn
    hn_hi = hn.astype(bf16)
    hn_lo = (hn - hn_hi.astype(f32)).astype(bf16)
    a = jnp.dot(hn_hi, wr_ref[...], preferred_element_type=f32)
    b = jnp.dot(hn_lo, wr_ref[:, :LANES], preferred_element_type=f32)
    lg = a[:, :LANES] + a[:, LANES:] + b + br_ref[...]

    lgt = lg.T[:N_EXPERTS]
    row = lax.broadcasted_iota(jnp.int32, lgt.shape, 0)
    cur = lgt
    vals, idxs = [], []
    for _ in range(TOP_K):
        m = jnp.max(cur, axis=0, keepdims=True)
        idx = jnp.min(jnp.where(cur == m, row, N_EXPERTS), axis=0, keepdims=True)
        vals.append(m)
        idxs.append(idx)
        cur = jnp.where(row == idx, -jnp.inf, cur)
    es = [jnp.exp(v - vals[0]) for v in vals]
    den = es[0] + es[1] + es[2] + es[3]
    slot = lax.broadcasted_iota(jnp.int32, gate_ref.shape, 0)
    gate_out = jnp.zeros(gate_ref.shape, f32)
    idx_out = jnp.zeros(idx_ref.shape, jnp.int32)
    for k in range(TOP_K):
        gate_out = jnp.where(slot == k, es[k] / den, gate_out)
        idx_out = jnp.where(slot == k, idxs[k], idx_out)
    gate_ref[...] = gate_out
    idx_ref[...] = idx_out


def _outproj(merged, w_out, x2d, gain, w_router_split, b_router_pad, tm=512):
    return pl.pallas_call(
        _outproj_kernel,
        grid=(N_TOK // tm,),
        in_specs=[
            pl.BlockSpec((tm, D_MODEL), lambda i: (i, 0)),
            pl.BlockSpec((D_MODEL, D_MODEL), lambda i: (0, 0)),
            pl.BlockSpec((tm, D_MODEL), lambda i: (i, 0)),
            pl.BlockSpec((1, D_MODEL), lambda i: (0, 0)),
            pl.BlockSpec((D_MODEL, 2 * LANES), lambda i: (0, 0)),
            pl.BlockSpec((1, LANES), lambda i: (0, 0)),
        ],
        out_specs=[
            pl.BlockSpec((tm, D_MODEL), lambda i: (i, 0)),
            pl.BlockSpec((tm, D_MODEL), lambda i: (i, 0)),
            pl.BlockSpec((SUBLANES, tm), lambda i: (0, i)),
            pl.BlockSpec((SUBLANES, tm), lambda i: (0, i)),
        ],
        out_shape=[
            jax.ShapeDtypeStruct((N_TOK, D_MODEL), f32),
            jax.ShapeDtypeStruct((N_TOK, D_MODEL), f32),
            jax.ShapeDtypeStruct((SUBLANES, N_TOK), f32),
            jax.ShapeDtypeStruct((SUBLANES, N_TOK), jnp.int32),
        ],
        compiler_params=_params(("parallel",)),
        name="outproj_router",
    )(merged, w_out, x2d, gain, w_router_split, b_router_pad)


def _moe_kernel(item_e_ref, item_start_ref, item_n_ref, tok_ref, dest_ref,
                hn_hbm, wg_ref, wl_ref, bg_ref, bl_ref, wd_ref, bd_ref, y_hbm,
                acc_ref, xg_ref, xb_ref, wgb_ref, wlb_ref, wdb_ref, gsem, ssem):
    w = pl.program_id(0)
    t = pl.program_id(1)
    last_w = pl.num_programs(0) - 1
    n = item_n_ref[w]
    start = item_start_ref[w]
    slot = lax.rem(w, 2)
    w_next = jnp.minimum(w + 1, last_w)
    start_next = item_start_ref[w_next]
    w_prev = jnp.maximum(w - 1, 0)
    start_prev = item_start_ref[w_prev]
    n_prev = jnp.where(w > 0, item_n_ref[w_prev], 0)
    is_last_live = jnp.logical_or(w == last_w, item_n_ref[w_next] == 0)

    def gather_row(row, base):
        tk = tok_ref[base + row]
        pltpu.make_async_copy(hn_hbm.at[pl.ds(tk, 1), :], xg_ref.at[pl.ds(row, 1), :], gsem).start()

    def gather_wait():
        pltpu.make_async_copy(hn_hbm.at[pl.ds(0, MOE_ROWS), :], xg_ref, gsem).wait()

    def scatter_row(row, base, cnt, sl, priority=0):
        d = jnp.where(row < cnt, dest_ref[base + row], N_ASSIGN + row)
        pltpu.make_async_copy(acc_ref.at[sl, pl.ds(row, 1), :], y_hbm.at[pl.ds(d, 1), :],
                              ssem).start(priority=priority)

    def scatter_wait(sl):
        pltpu.make_async_copy(acc_ref.at[sl], y_hbm.at[pl.ds(0, MOE_ROWS), :], ssem).wait()

    @pl.when(jnp.logical_and(w == 0, t == 0))
    def _():
        acc_ref[1] = jnp.zeros((MOE_ROWS, D_MODEL), f32)

        def issue(r, c):
            gather_row(r, start)
            return c
        lax.fori_loop(0, MOE_ROWS, issue, 0, unroll=8)

    @pl.when(jnp.logical_and(t == 0, n > 0))
    def _():
        gather_wait()
        xb_ref[...] = xg_ref[...].astype(bf16)
        acc_ref[slot] = jnp.broadcast_to(bd_ref[...], (MOE_ROWS, D_MODEL))

    def dma_chunk(r):
        row0 = t * (MOE_NSUB * MOE_CHUNK) + r * MOE_CHUNK
        for k in range(MOE_CHUNK):
            gather_row(row0 + k, start_next)
        for k in range(MOE_CHUNK):
            scatter_row(row0 + k, start_prev, n_prev, 1 - slot, priority=k % 2)

    def ffn_up(r0, rows):
        xs = xb_ref[r0:r0 + rows, :]
        hg = jnp.dot(xs, wgb_ref[...], preferred_element_type=f32) + bg_ref[...]
        hl = jnp.dot(xs, wlb_ref[...], preferred_element_type=f32) + bl_ref[...]
        return hg, hl

    def ffn_down(r0, rows, hg, hl):
        hg = jnp.minimum(hg, SWIGLU_LIMIT)
        hl = jnp.clip(hl, -SWIGLU_LIMIT, SWIGLU_LIMIT)
        act = hg * jax.nn.sigmoid(SWIGLU_ALPHA * hg) * (hl + 1.0)
        acc_ref[slot, r0:r0 + rows, :] += jnp.dot(act.astype(bf16), wdb_ref[...], preferred_element_type=f32)

    def ffn_rows(r0, rows):
        ffn_down(r0, rows, *ffn_up(r0, rows))

    @pl.when(n > 0)
    def _():
        nb_full = n // MOE_SUB
        rem = n - nb_full * MOE_SUB
        ext = jnp.logical_and(jnp.logical_and(rem > 0, rem <= MOE_HALF), nb_full >= 1)
        own_block = jnp.logical_or(rem > MOE_HALF, jnp.logical_and(rem > 0, nb_full == 0))
        nb = nb_full + jnp.where(own_block, 1, 0)
        tiny = jnp.logical_and(nb_full == 0, rem <= MOE_HALF)

        def cast_weights(r):
            if r == 0:
                wgb_ref[...] = wg_ref[...].astype(bf16)
                wlb_ref[...] = wl_ref[...].astype(bf16)
                wdb_ref[...] = wd_ref[...].astype(bf16)

        def single(r, rows, chunks=1):
            def body():
                for c in range(chunks):
                    dma_chunk(r + c)
                cast_weights(r)
                ffn_rows(r * MOE_SUB, rows)
            return body

        def pair(r, rows_b):
            def body():
                dma_chunk(r)
                dma_chunk(r + 1)
                cast_weights(r)
                ra, rb = r * MOE_SUB, (r + 1) * MOE_SUB
                ha = ffn_up(ra, MOE_SUB)
                hb = ffn_up(rb, rows_b)
                ffn_down(ra, MOE_SUB, *ha)
                ffn_down(rb, rows_b, *hb)
            return body

        def idle(*rs):
            def body():
                for r in rs:
                    dma_chunk(r)
            return body

        def last_single(r):
            return lambda: lax.cond(
                ext, single(r, MOE_SUB + MOE_HALF, 2),
                lambda: lax.cond(tiny, single(r, MOE_HALF, 2), single(r, MOE_SUB, 2)))

        for r in range(0, MOE_NSUB - 1, 2):
            is_last_pair = jnp.logical_and(ext, nb == r + 2)
            lax.cond(
                nb >= r + 2,
                lambda r=r, is_last_pair=is_last_pair: lax.cond(
                    is_last_pair, pair(r, MOE_SUB + MOE_HALF), pair(r, MOE_SUB)),
                lambda r=r: lax.cond(nb == r + 1, last_single(r), idle(r, r + 1)))
        for r in range(MOE_NSUB - MOE_NSUB % 2, MOE_NSUB):
            lax.cond(nb == r + 1, single(r, MOE_SUB), idle(r))

    @pl.when(jnp.logical_and(t == MOE_NF - 1, n > 0))
    def _():
        scatter_wait(1 - slot)

        @pl.when(is_last_live)
        def _():
            def issue(r, c):
                scatter_row(r, start, n, slot)
                return c
            lax.fori_loop(0, MOE_ROWS, issue, 0, unroll=8)
            scatter_wait(slot)
            gather_wait()


def _moe(n_items, item_e, item_start, item_n, tok_sorted, dest_sorted, hn2, w_gate_up, b_gate_up, w_down, b_down):
    tf = MOE_TF
    live_t = lambda w, t, n: jnp.where(n[w] > 0, t, MOE_NF - 1)
    grid_spec = pltpu.PrefetchScalarGridSpec(
        num_scalar_prefetch=5,
        grid=(n_items, MOE_NF),
        in_specs=[
            pl.BlockSpec(memory_space=pl.ANY),
            pl.BlockSpec((None, D_MODEL, tf), lambda w, t, e, s, n, tk, ds: (e[w], 0, live_t(w, t, n))),
            pl.BlockSpec((None, D_MODEL, tf), lambda w, t, e, s, n, tk, ds: (e[w], 0, MOE_NF + live_t(w, t, n))),
            pl.BlockSpec((None, 1, tf), lambda w, t, e, s, n, tk, ds: (e[w], 0, live_t(w, t, n))),
            pl.BlockSpec((None, 1, tf), lambda w, t, e, s, n, tk, ds: (e[w], 0, MOE_NF + live_t(w, t, n))),
            pl.BlockSpec((None, tf, D_MODEL), lambda w, t, e, s, n, tk, ds: (e[w], live_t(w, t, n), 0)),
            pl.BlockSpec((None, 1, D_MODEL), lambda w, t, e, s, n, tk, ds: (e[w], 0, 0)),
        ],
        out_specs=pl.BlockSpec(memory_space=pl.ANY),
        scratch_shapes=[
            pltpu.VMEM((2, MOE_ROWS, D_MODEL), f32),
            pltpu.VMEM((MOE_ROWS, D_MODEL), f32),
            pltpu.VMEM((MOE_ROWS, D_MODEL), bf16),
            pltpu.VMEM((D_MODEL, tf), bf16),
            pltpu.VMEM((D_MODEL, tf), bf16),
            pltpu.VMEM((tf, D_MODEL), bf16),
            pltpu.SemaphoreType.DMA,
            pltpu.SemaphoreType.DMA,
        ],
    )
    return pl.pallas_call(
        _moe_kernel,
        grid_spec=grid_spec,
        out_shape=jax.ShapeDtypeStruct((Y_ROWS, D_MODEL), f32),
        compiler_params=_params(("arbitrary", "arbitrary"), BIG_VMEM_LIMIT_BYTES),
        name="moe_experts",
    )(item_e, item_start, item_n, tok_sorted, dest_sorted,
      hn2, w_gate_up, w_gate_up, b_gate_up, b_gate_up, w_down, b_down)


def _combine_kernel(y0_ref, y1_ref, y2_ref, y3_ref, h_ref, gate_ref, g_ref, o_ref):
    h = h_ref[...]
    gates = gate_ref[...]
    for k, y_ref in enumerate((y0_ref, y1_ref, y2_ref, y3_ref)):
        h = h + y_ref[...] * gates[:, k:k + 1]
    o_ref[...] = h * lax.rsqrt(jnp.mean(h * h, axis=-1, keepdims=True) + RMS_EPS) * g_ref[...]


def _combine(y, h1, gates, gain, tm=256):
    slot_spec = lambda k: pl.BlockSpec((tm, D_MODEL), lambda i, k=k: (k * (N_TOK // tm) + i, 0))
    return pl.pallas_call(
        _combine_kernel,
        grid=(N_TOK // tm,),
        in_specs=[slot_spec(k) for k in range(TOP_K)] + [
            pl.BlockSpec((tm, D_MODEL), lambda i: (i, 0)),
            pl.BlockSpec((tm, TOP_K), lambda i: (i, 0)),
            pl.BlockSpec((1, D_MODEL), lambda i: (0, 0)),
        ],
        out_specs=pl.BlockSpec((tm, D_MODEL), lambda i: (i, 0)),
        out_shape=jax.ShapeDtypeStruct((N_TOK, D_MODEL), f32),
        compiler_params=_params(("parallel",)),
        name="combine_norm",
    )(y, y, y, y, h1, gates, gain)


def _route(top_idx):
    expert_flat = top_idx.reshape(-1)
    order = jnp.argsort(expert_flat, stable=True).astype(jnp.int32)
    counts = jnp.zeros((N_EXPERTS,), jnp.int32).at[expert_flat].add(1)
    starts = jnp.cumsum(counts) - counts
    tok_sorted = order // TOP_K
    dest_sorted = (order % TOP_K) * N_TOK + tok_sorted
    tok_sorted = jnp.pad(tok_sorted, (0, MOE_ROWS))
    dest_sorted = jnp.pad(dest_sorted, (0, MOE_ROWS))
    chunks = (counts + MOE_ROWS - 1) // MOE_ROWS
    chunk_end = jnp.cumsum(chunks)
    n_items = chunk_end[-1]
    w = jnp.arange(MOE_ITEMS, dtype=jnp.int32)
    w_eff = jnp.minimum(w, n_items - 1)
    e = jnp.sum(w_eff[:, None] >= chunk_end[None, :], axis=1).astype(jnp.int32)
    c = w_eff - (chunk_end - chunks)[e]
    item_start = starts[e] + c * MOE_ROWS
    item_n = jnp.where(w < n_items, jnp.clip(counts[e] - c * MOE_ROWS, 0, MOE_ROWS), 0)
    return n_items, e, item_start.astype(jnp.int32), item_n.astype(jnp.int32), tok_sorted, dest_sorted


def kernel(x, meta_tokens, rel_bias_table, attn_norm, w_in, fox_forget_bias, lam_q1, lam_k1, lam_q2, lam_k2,
           diff_subln, w_branch_fox, w_branch_diff, w_out, ffn_norm, w_router, b_router, w_gate_up, b_gate_up,
           w_down, b_down, final_norm):
    x2d = x.reshape(N_TOK, D_MODEL)
    w_t = jnp.swapaxes(w_in[0], 0, 1)
    fb_pad = jnp.pad(fox_forget_bias[0], (0, LANES - FOX_HEADS)).reshape(1, LANES)

    meta_pad = jnp.pad(meta_tokens, ((0, N_META_PAD - N_META), (0, 0)))
    proj, proj_meta, ff_real, ff_meta = _inproj(x2d, meta_pad, attn_norm, w_t, tm=1024, tn=512)

    ck, ckm = _forget_cumsum(ff_real, ff_meta, fb_pad)
    o_fox = _fox_attention(proj, proj_meta, ck, ckm)

    near, mbias = _bias_tiles(rel_bias_table)
    lam_vecs = jnp.concatenate([lam_q1, lam_k1, lam_q2, lam_k2], axis=0)
    o_diff = _diff_attention(proj, proj_meta, near, mbias, lam_vecs, diff_subln)

    merged = _gated_merge(o_fox, o_diff, w_branch_fox[0].astype(bf16), w_branch_diff[0].astype(bf16), proj)
    w_router_pad = jnp.pad(w_router[0], ((0, 0), (0, LANES - N_EXPERTS)))
    w_router_hi = w_router_pad.astype(bf16)
    w_router_lo = (w_router_pad - w_router_hi.astype(f32)).astype(bf16)
    w_router_split = jnp.concatenate([w_router_hi, w_router_lo], axis=1)
    b_router_pad = jnp.pad(b_router[0], (0, LANES - N_EXPERTS)).reshape(1, LANES)
    h1, hn2, gates_t, idx_t = _outproj(merged, w_out[0].astype(bf16), x2d, ffn_norm, w_router_split, b_router_pad)
    gates = gates_t[:TOP_K].T

    n_items, item_e, item_start, item_n, tok_sorted, dest_sorted = _route(idx_t[:TOP_K].T)
    y = _moe(n_items, item_e, item_start, item_n, tok_sorted, dest_sorted, hn2,
             w_gate_up[0], b_gate_up[0].reshape(N_EXPERTS, 1, 2 * D_EXPERT),
             w_down[0], b_down[0].reshape(N_EXPERTS, 1, D_MODEL))
    out = _combine(y, h1, gates, final_norm.reshape(1, D_MODEL))
    return out.reshape(BATCH, SEQ, D_MODEL)
```

```python
import functools
import math

import jax
import jax.numpy as jnp
from jax import lax
from jax.experimental import pallas as pl
from jax.experimental.pallas import tpu as pltpu

D_MODEL = 2048
BATCH = 4
SEQ = 2048
N_TOK = BATCH * SEQ
N_META = 16
N_META_PAD = 128
HEAD_DIM = 128
FOX_HEADS = 8
DIFF_HEADS = 4
DIFF_V_DIM = 2 * HEAD_DIM
FOX_WIDTH = FOX_HEADS * HEAD_DIM
DIFF_QK_WIDTH = DIFF_HEADS * 2 * HEAD_DIM
DIFF_WIDTH = DIFF_HEADS * DIFF_V_DIM
N_BUCKETS = 32
MAX_DISTANCE = 128
N_EXPERTS = 32
TOP_K = 4
D_EXPERT = D_MODEL
SWIGLU_LIMIT = 7.0
SWIGLU_ALPHA = 1.702
RMS_EPS = 1e-5
NEG_INF = -1e30
LAMBDA_INIT = 0.8 - 0.6 * math.exp(-0.3 * 0)
ATTN_SCALE = HEAD_DIM ** -0.5
LOG2E = math.log2(math.e)
Q_PRESCALE = ATTN_SCALE * LOG2E

LANES = 128
SUBLANES = 8
VMEM_LIMIT_BYTES = 56 * 1024 * 1024
BIG_VMEM_LIMIT_BYTES = 60 * 1024 * 1024

COL_FQ = 0
COL_FK = COL_FQ + FOX_WIDTH
COL_FV = COL_FK + FOX_WIDTH
COL_DQ = COL_FV + FOX_WIDTH
COL_DK = COL_DQ + DIFF_QK_WIDTH
COL_DV = COL_DK + DIFF_QK_WIDTH
COL_GF = COL_DV + DIFF_WIDTH
COL_GD = COL_GF + D_MODEL
D_PROJ = COL_GD + D_MODEL

INPROJ_XT = 256
ATTN_TILE = 512
N_QBLK = SEQ // ATTN_TILE

MOE_ROWS = 1280
MOE_SUB = 256
MOE_HALF = MOE_SUB // 2
MOE_NSUB = MOE_ROWS // MOE_SUB
MOE_TF = 256
MOE_NF = D_EXPERT // MOE_TF
MOE_CHUNK = MOE_ROWS // (MOE_NF * MOE_NSUB)
N_ASSIGN = N_TOK * TOP_K
MOE_ITEMS = N_EXPERTS + N_ASSIGN // MOE_ROWS
Y_ROWS = N_ASSIGN + MOE_ROWS

f32 = jnp.float32
bf16 = jnp.bfloat16


def _params(sem, vmem=VMEM_LIMIT_BYTES):
    return pltpu.CompilerParams(dimension_semantics=sem, vmem_limit_bytes=vmem)


def _rms_bf16(x, g):
    return (x * lax.rsqrt(jnp.mean(x * x, axis=-1, keepdims=True) + RMS_EPS) * g).astype(bf16)


def _inproj_kernel(x_hbm, xm_ref, g_ref, wt_ref, wfft_ref, o_ref, om_ref, ff_ref, ffm_ref,
                   xn_ref, xnm_ref, xt_ref, wb_ref, wffb_ref, sem, *, tm, tn):
    j = pl.program_id(0)
    i = pl.program_id(1)
    rows = pl.ds(pl.multiple_of(i * tm, tm), tm)
    is_q = jnp.logical_or(j < COL_FK // tn, jnp.logical_and(j >= COL_DQ // tn, j < COL_DK // tn))
    qs = jnp.where(is_q, Q_PRESCALE, 1.0)

    @pl.when(jnp.logical_and(j == 0, i == 0))
    def _():
        wffb_ref[...] = wfft_ref[...].T.astype(bf16)
        xnm_ref[...] = _rms_bf16(xm_ref[...], g_ref[...])
        ffm_ref[...] = jnp.dot(xnm_ref[...], wffb_ref[...], preferred_element_type=f32)

    @pl.when(i == 0)
    def _():
        wb_ref[...] = wt_ref[...].T.astype(bf16)
        ym = jnp.dot(xnm_ref[...], wb_ref[...], preferred_element_type=f32)
        om_ref[...] = (ym * qs).astype(bf16)

    @pl.when(j == 0)
    def _():
        def piece(s):
            r0 = pl.multiple_of(i * tm + s * INPROJ_XT, INPROJ_XT)
            return pltpu.make_async_copy(x_hbm.at[pl.ds(r0, INPROJ_XT), :], xt_ref.at[s % 2], sem.at[s % 2])

        n_pieces = tm // INPROJ_XT
        piece(0).start()
        for s in range(n_pieces):
            if s + 1 < n_pieces:
                piece(s + 1).start()
            piece(s).wait()
            xn = _rms_bf16(xt_ref[s % 2], g_ref[...])
            xn_ref[pl.ds(pl.multiple_of(i * tm + s * INPROJ_XT, INPROJ_XT), INPROJ_XT), :] = xn
            ff_ref[s * INPROJ_XT:(s + 1) * INPROJ_XT, :] = jnp.dot(xn, wffb_ref[...], preferred_element_type=f32)

    y = jnp.dot(xn_ref[rows, :], wb_ref[...], preferred_element_type=f32)
    o_ref[...] = (y * qs).astype(bf16)


def _inproj(x2d, meta_pad, gain, w_t, tm, tn):
    n = x2d.shape[0]
    n_i = n // tm
    w_row = lambda j: SUBLANES * (j * (tn // SUBLANES) + jnp.where(j * tn >= COL_DQ, FOX_HEADS // SUBLANES, 0))
    first_pass_row = lambda j, i: jnp.where(j == 0, i, n_i - 1)
    return pl.pallas_call(
        functools.partial(_inproj_kernel, tm=tm, tn=tn),
        grid=(D_PROJ // tn, n_i),
        in_specs=[
            pl.BlockSpec(memory_space=pl.ANY),
            pl.BlockSpec((N_META_PAD, D_MODEL), lambda j, i: (0, 0)),
            pl.BlockSpec((1, D_MODEL), lambda j, i: (0, 0)),
            pl.BlockSpec((pl.Element(tn), pl.Element(D_MODEL)), lambda j, i: (w_row(j), 0)),
            pl.BlockSpec((pl.Element(LANES), pl.Element(D_MODEL)), lambda j, i: (COL_DQ, 0)),
        ],
        out_specs=[
            pl.BlockSpec((tm, tn), lambda j, i: (i, j)),
            pl.BlockSpec((N_META_PAD, tn), lambda j, i: (0, j)),
            pl.BlockSpec((tm, LANES), lambda j, i: (first_pass_row(j, i), 0)),
            pl.BlockSpec((N_META_PAD, LANES), lambda j, i: (0, 0)),
        ],
        out_shape=[
            jax.ShapeDtypeStruct((n, D_PROJ), bf16),
            jax.ShapeDtypeStruct((N_META_PAD, D_PROJ), bf16),
            jax.ShapeDtypeStruct((n, LANES), f32),
            jax.ShapeDtypeStruct((N_META_PAD, LANES), f32),
        ],
        scratch_shapes=[
            pltpu.VMEM((n, D_MODEL), bf16),
            pltpu.VMEM((N_META_PAD, D_MODEL), bf16),
            pltpu.VMEM((2, INPROJ_XT, D_MODEL), f32),
            pltpu.VMEM((D_MODEL, tn), bf16),
            pltpu.VMEM((D_MODEL, LANES), bf16),
            pltpu.SemaphoreType.DMA((2,)),
        ],
        compiler_params=_params(("arbitrary", "arbitrary"), BIG_VMEM_LIMIT_BYTES),
        name="inproj",
    )(x2d, meta_pad, gain, w_t, w_t)


def _log_sigmoid(x):
    return jnp.minimum(x, 0.0) - jnp.log(1.0 + jnp.exp(-jnp.abs(x)))


def _lane_cumsum(x):
    n = x.shape[-1]
    lane = lax.broadcasted_iota(jnp.int32, x.shape, x.ndim - 1)
    s = 1
    while s < n:
        x = x + jnp.where(lane >= s, pltpu.roll(x, s, x.ndim - 1), 0.0)
        s *= 2
    return x


def _cum_kernel(ff_ref, ffm_ref, fb_ref, ck_ref, ckm_ref):
    fb = fb_ref[...]
    row = lax.broadcasted_iota(jnp.int32, (LANES, LANES), 0)
    lfm = jnp.where(row < N_META, _log_sigmoid(ffm_ref[...] + fb), 0.0)
    cm = _lane_cumsum(lfm.T)
    m_total = cm[:, N_META - 1:N_META]
    lf = _log_sigmoid(ff_ref[...] + fb)
    cr = _lane_cumsum(lf.T) + m_total
    ck_ref[...] = cr[:SUBLANES] * LOG2E
    ckm_ref[...] = cm[:SUBLANES] * LOG2E


def _forget_cumsum(ff_real, ff_meta_pad, fb_pad):
    return pl.pallas_call(
        _cum_kernel,
        grid=(BATCH,),
        in_specs=[
            pl.BlockSpec((SEQ, LANES), lambda b: (b, 0)),
            pl.BlockSpec((LANES, LANES), lambda b: (0, 0)),
            pl.BlockSpec((1, LANES), lambda b: (0, 0)),
        ],
        out_specs=[
            pl.BlockSpec((None, SUBLANES, SEQ), lambda b: (b, 0, 0)),
            pl.BlockSpec((None, SUBLANES, LANES), lambda b: (b, 0, 0)),
        ],
        out_shape=[
            jax.ShapeDtypeStruct((BATCH, FOX_HEADS, SEQ), f32),
            jax.ShapeDtypeStruct((BATCH, FOX_HEADS, LANES), f32),
        ],
        compiler_params=_params(("parallel",)),
        name="forget_cumsum",
    )(ff_real, ff_meta_pad, fb_pad)


def _qk(q, k):
    return lax.dot_general(q, k, (((1,), (1,)), ((), ())), preferred_element_type=f32)


def _row_to_col(row):
    t = row.shape[-1]
    r = lax.broadcasted_iota(jnp.int32, (t, t), 0)
    c = lax.broadcasted_iota(jnp.int32, (t, t), 1)
    return jnp.sum(jnp.where(r == c, row, 0.0), axis=1, keepdims=True)


def _softmax_pieces(pieces, row_shift=None):
    m = None
    for s, _ in pieces:
        mi = jnp.max(s, axis=-1, keepdims=True)
        m = mi if m is None else jnp.maximum(m, mi)
    if row_shift is not None:
        m = (m + row_shift) - row_shift
    l = acc = None
    for s, v in pieces:
        p = jnp.exp2(s - m)
        li = jnp.sum(p, axis=-1, keepdims=True)
        ai = jnp.dot(p.astype(bf16), v, preferred_element_type=f32)
        l = li if l is None else l + li
        acc = ai if acc is None else acc + ai
    return acc / l


def _switch_qblock(i, branches):
    def pick(lo, hi):
        if lo == hi:
            return branches[lo]
        mid = (lo + hi) // 2
        return lambda: lax.cond(i <= mid, pick(lo, mid), pick(mid + 1, hi))
    pick(0, len(branches) - 1)()


FOX_GROUP = 4


def _fox_kernel(q_ref, k_ref, v_ref, km_ref, vm_ref, ck_ref, ckm_ref, o_ref):
    hg = pl.program_id(1)
    i = pl.program_id(2)
    t = ATTN_TILE
    q0 = pl.multiple_of(i * t, t)
    meta_ok = lax.broadcasted_iota(jnp.int32, (t, N_META_PAD), 1) < N_META
    r = lax.broadcasted_iota(jnp.int32, (t, t), 0)
    c = lax.broadcasted_iota(jnp.int32, (t, t), 1)

    def attend(n_far):
        def body():
            for g in range(FOX_GROUP):
                col = slice(g * HEAD_DIM, (g + 1) * HEAD_DIM)
                head = pl.ds(hg * FOX_GROUP + g, 1)
                q = q_ref[:, col]
                cq = _row_to_col(ck_ref[head, pl.ds(q0, t)])
                u_meta = jnp.where(meta_ok, _qk(q, km_ref[:, col]) - ckm_ref[head, :], NEG_INF)
                pieces = [(u_meta, vm_ref[:, col])]
                if n_far:
                    pieces.append((_qk(q, k_ref[0:n_far, col]) - ck_ref[head, 0:n_far], v_ref[0:n_far, col]))
                u_diag = _qk(q, k_ref[n_far:n_far + t, col]) - ck_ref[head, n_far:n_far + t]
                pieces.append((jnp.where(c <= r, u_diag, NEG_INF), v_ref[n_far:n_far + t, col]))
                o_ref[:, col] = _softmax_pieces(pieces, row_shift=cq).astype(bf16)
        return body

    _switch_qblock(i, [attend(ib * t) for ib in range(N_QBLK)])


def _fox_attention(proj, proj_meta, ck, ckm):
    t = ATTN_TILE
    w = FOX_GROUP * HEAD_DIM
    cb = lambda col: col // w
    return pl.pallas_call(
        _fox_kernel,
        grid=(BATCH, FOX_HEADS // FOX_GROUP, N_QBLK),
        in_specs=[
            pl.BlockSpec((t, w), lambda b, h, i: (b * N_QBLK + i, cb(COL_FQ) + h)),
            pl.BlockSpec((SEQ, w), lambda b, h, i: (b, cb(COL_FK) + h)),
            pl.BlockSpec((SEQ, w), lambda b, h, i: (b, cb(COL_FV) + h)),
            pl.BlockSpec((N_META_PAD, w), lambda b, h, i: (0, cb(COL_FK) + h)),
            pl.BlockSpec((N_META_PAD, w), lambda b, h, i: (0, cb(COL_FV) + h)),
            pl.BlockSpec((None, FOX_HEADS, SEQ), lambda b, h, i: (b, 0, 0)),
            pl.BlockSpec((None, FOX_HEADS, LANES), lambda b, h, i: (b, 0, 0)),
        ],
        out_specs=pl.BlockSpec((t, w), lambda b, h, i: (b * N_QBLK + i, h)),
        out_shape=jax.ShapeDtypeStruct((N_TOK, FOX_WIDTH), bf16),
        compiler_params=_params(("parallel", "parallel", "arbitrary")),
        name="fox_attention",
    )(proj, proj, proj, proj_meta, proj_meta, ck, ckm)


def _t5_bias(dist, table_ref, h):
    n = jnp.maximum(dist, 0)
    max_exact = N_BUCKETS // 2
    log_part = jnp.log(jnp.maximum(n, 1).astype(f32) / max_exact) / math.log(MAX_DISTANCE / max_exact)
    v = log_part * (N_BUCKETS - max_exact)
    far = table_ref[N_BUCKETS - 1, h]
    val = lambda b: (table_ref[b, h] - far) * LOG2E
    large = jnp.zeros(dist.shape, f32)
    for b in range(N_BUCKETS - 2, max_exact - 1, -1):
        large = jnp.where(v < b + 1 - max_exact, val(b), large)
    out = large
    for b in range(max_exact):
        out = jnp.where(n == b, val(b), out)
    return out


def _bias_kernel(table_ref, near_ref, meta_ref):
    h = pl.program_id(0)
    t = ATTN_TILE
    r = lax.broadcasted_iota(jnp.int32, (t, 2 * t), 0)
    c = lax.broadcasted_iota(jnp.int32, (t, 2 * t), 1)
    near_ref[...] = _t5_bias(r + t - c, table_ref, h)
    r = lax.broadcasted_iota(jnp.int32, (t, LANES), 0)
    c = lax.broadcasted_iota(jnp.int32, (t, LANES), 1)
    meta_ref[...] = _t5_bias(N_META + r - c, table_ref, h)


def _bias_tiles(table):
    t = ATTN_TILE
    return pl.pallas_call(
        _bias_kernel,
        grid=(DIFF_HEADS,),
        in_specs=[pl.BlockSpec(memory_space=pltpu.SMEM)],
        out_specs=[
            pl.BlockSpec((None, t, 2 * t), lambda h: (h, 0, 0)),
            pl.BlockSpec((None, t, LANES), lambda h: (h, 0, 0)),
        ],
        out_shape=[
            jax.ShapeDtypeStruct((DIFF_HEADS, t, 2 * t), f32),
            jax.ShapeDtypeStruct((DIFF_HEADS, t, LANES), f32),
        ],
        compiler_params=_params(("arbitrary",)),
        name="t5_bias_tiles",
    )(table)


def _diff_kernel(q1_ref, q2_ref, k1_ref, k2_ref, v_ref, k1m_ref, k2m_ref, vm_ref,
                 near_ref, mbias_ref, lam_ref, subln_ref, o_ref):
    i = pl.program_id(2)
    t = ATTN_TILE
    q1 = q1_ref[...]
    q2 = q2_ref[...]
    lam = (jnp.exp(jnp.sum(lam_ref[0:1, :] * lam_ref[1:2, :], axis=-1, keepdims=True))
           - jnp.exp(jnp.sum(lam_ref[2:3, :] * lam_ref[3:4, :], axis=-1, keepdims=True))
           + LAMBDA_INIT)

    mb = jnp.where(i == 0, mbias_ref[...], 0.0)
    meta_ok = lax.broadcasted_iota(jnp.int32, (t, N_META_PAD), 1) < N_META
    vm = vm_ref[...]

    def attend(n_far, n_near):
        def body():
            r = lax.broadcasted_iota(jnp.int32, (t, n_near), 0)
            c = lax.broadcasted_iota(jnp.int32, (t, n_near), 1)
            mask = c <= r + (n_near - t)
            bias = near_ref[:, 2 * t - n_near:]
            outs = []
            for q, k_ref, km_ref in ((q1, k1_ref, k1m_ref), (q2, k2_ref, k2m_ref)):
                pieces = [(jnp.where(meta_ok, _qk(q, km_ref[...]) + mb, NEG_INF), vm)]
                if n_far:
                    pieces.append((_qk(q, k_ref[0:n_far, :]), v_ref[0:n_far, :]))
                s_near = jnp.where(mask, _qk(q, k_ref[n_far:n_far + n_near, :]) + bias, NEG_INF)
                pieces.append((s_near, v_ref[n_far:n_far + n_near, :]))
                outs.append(_softmax_pieces(pieces))
            o = outs[0] - lam * outs[1]
            y = o * lax.rsqrt(jnp.mean(o * o, axis=-1, keepdims=True) + RMS_EPS) * subln_ref[...]
            o_ref[...] = (y * (1.0 - LAMBDA_INIT)).astype(bf16)
        return body

    branches = []
    for ib in range(N_QBLK):
        n_near = min(2 * t, (ib + 1) * t)
        branches.append(attend((ib + 1) * t - n_near, n_near))
    _switch_qblock(i, branches)


def _diff_attention(proj, proj_meta, near, mbias, lam_vecs, subln):
    t = ATTN_TILE
    cb = lambda col: col // HEAD_DIM
    vb = lambda col: col // DIFF_V_DIM
    row = lambda b, h, i: b * N_QBLK + i
    return pl.pallas_call(
        _diff_kernel,
        grid=(BATCH, DIFF_HEADS, N_QBLK),
        in_specs=[
            pl.BlockSpec((t, HEAD_DIM), lambda b, h, i: (row(b, h, i), cb(COL_DQ) + 2 * h)),
            pl.BlockSpec((t, HEAD_DIM), lambda b, h, i: (row(b, h, i), cb(COL_DQ) + 2 * h + 1)),
            pl.BlockSpec((SEQ, HEAD_DIM), lambda b, h, i: (b, cb(COL_DK) + 2 * h)),
            pl.BlockSpec((SEQ, HEAD_DIM), lambda b, h, i: (b, cb(COL_DK) + 2 * h + 1)),
            pl.BlockSpec((SEQ, DIFF_V_DIM), lambda b, h, i: (b, vb(COL_DV) + h)),
            pl.BlockSpec((N_META_PAD, HEAD_DIM), lambda b, h, i: (0, cb(COL_DK) + 2 * h)),
            pl.BlockSpec((N_META_PAD, HEAD_DIM), lambda b, h, i: (0, cb(COL_DK) + 2 * h + 1)),
            pl.BlockSpec((N_META_PAD, DIFF_V_DIM), lambda b, h, i: (0, vb(COL_DV) + h)),
            pl.BlockSpec((None, t, 2 * t), lambda b, h, i: (h, 0, 0)),
            pl.BlockSpec((None, t, LANES), lambda b, h, i: (h, 0, 0)),
            pl.BlockSpec((4, HEAD_DIM), lambda b, h, i: (0, 0)),
            pl.BlockSpec((1, DIFF_V_DIM), lambda b, h, i: (0, 0)),
        ],
        out_specs=pl.BlockSpec((t, DIFF_V_DIM), lambda b, h, i: (row(b, h, i), h)),
        out_shape=jax.ShapeDtypeStruct((N_TOK, DIFF_WIDTH), bf16),
        compiler_params=_params(("parallel", "parallel", "arbitrary")),
        name="diff_attention",
    )(proj, proj, proj, proj, proj, proj_meta, proj_meta, proj_meta, near, mbias, lam_vecs, subln)


def _merge_kernel(of_ref, od_ref, wf_ref, wd_ref, gf_ref, gd_ref, o_ref):
    yf = jnp.dot(of_ref[...], wf_ref[...], preferred_element_type=f32)
    yd = jnp.dot(od_ref[...], wd_ref[...], preferred_element_type=f32)
    gf = jax.nn.sigmoid(gf_ref[...].astype(f32))
    gd = jax.nn.sigmoid(gd_ref[...].astype(f32))
    o_ref[...] = (gf * yf + gd * yd).astype(bf16)


def _gated_merge(o_fox, o_diff, w_bf, w_bd, proj, tm=1024, tn=1024):
    return pl.pallas_call(
        _merge_kernel,
        grid=(D_MODEL // tn, N_TOK // tm),
        in_specs=[
            pl.BlockSpec((tm, FOX_WIDTH), lambda j, i: (i, 0)),
            pl.BlockSpec((tm, DIFF_WIDTH), lambda j, i: (i, 0)),
            pl.BlockSpec((FOX_WIDTH, tn), lambda j, i: (0, j)),
            pl.BlockSpec((DIFF_WIDTH, tn), lambda j, i: (0, j)),
            pl.BlockSpec((tm, tn), lambda j, i: (i, COL_GF // tn + j)),
            pl.BlockSpec((tm, tn), lambda j, i: (i, COL_GD // tn + j)),
        ],
        out_specs=pl.BlockSpec((tm, tn), lambda j, i: (i, j)),
        out_shape=jax.ShapeDtypeStruct((N_TOK, D_MODEL), bf16),
        compiler_params=_params(("parallel", "arbitrary")),
        name="gated_merge",
    )(o_fox, o_diff, w_bf, w_bd, proj, proj)


def _outproj_kernel(m_ref, w_ref, x_ref, g_ref, wr_ref, br_ref, h_ref, hn_ref, gate_ref, idx_ref):
    h1 = x_ref[...] + jnp.dot(m_ref[...], w_ref[...], preferred_element_type=f32)
    h_ref[...] = h1
    hn = h1 * lax.rsqrt(jnp.mean(h1 * h1, axis=-1, keepdims=True) + RMS_EPS) * g_ref[...]
    hn_ref[...] = hn
    hn_hi = hn.astype(bf16)
    hn_lo = (hn - hn_hi.astype(f32)).astype(bf16)
    a = jnp.dot(hn_hi, wr_ref[...], preferred_element_type=f32)
    b = jnp.dot(hn_lo, wr_ref[:, :LANES], preferred_element_type=f32)
    lg = a[:, :LANES] + a[:, LANES:] + b + br_ref[...]

    lgt = lg.T[:N_EXPERTS]
    row = lax.broadcasted_iota(jnp.int32, lgt.shape, 0)
    cur = lgt
    vals, idxs = [], []
    for _ in range(TOP_K):
        m = jnp.max(cur, axis=0, keepdims=True)
        idx = jnp.min(jnp.where(cur == m, row, N_EXPERTS), axis=0, keepdims=True)
        vals.append(m)
        idxs.append(idx)
        cur = jnp.where(row == idx, -jnp.inf, cur)
    es = [jnp.exp(v - vals[0]) for v in vals]
    den = es[0] + es[1] + es[2] + es[3]
    slot = lax.broadcasted_iota(jnp.int32, gate_ref.shape, 0)
    gate_out = jnp.zeros(gate_ref.shape, f32)
    idx_out = jnp.zeros(idx_ref.shape, jnp.int32)
    for k in range(TOP_K):
        gate_out = jnp.where(slot == k, es[k] / den, gate_out)
        idx_out = jnp.where(slot == k, idxs[k], idx_out)
    gate_ref[...] = gate_out
    idx_ref[...] = idx_out


def _outproj(merged, w_out, x2d, gain, w_router_split, b_router_pad, tm=512):
    return pl.pallas_call(
        _outproj_kernel,
        grid=(N_TOK // tm,),
        in_specs=[
            pl.BlockSpec((tm, D_MODEL), lambda i: (i, 0)),
            pl.BlockSpec((D_MODEL, D_MODEL), lambda i: (0, 0)),
            pl.BlockSpec((tm, D_MODEL), lambda i: (i, 0)),
            pl.BlockSpec((1, D_MODEL), lambda i: (0, 0)),
            pl.BlockSpec((D_MODEL, 2 * LANES), lambda i: (0, 0)),
            pl.BlockSpec((1, LANES), lambda i: (0, 0)),
        ],
        out_specs=[
            pl.BlockSpec((tm, D_MODEL), lambda i: (i, 0)),
            pl.BlockSpec((tm, D_MODEL), lambda i: (i, 0)),
            pl.BlockSpec((SUBLANES, tm), lambda i: (0, i)),
            pl.BlockSpec((SUBLANES, tm), lambda i: (0, i)),
        ],
        out_shape=[
            jax.ShapeDtypeStruct((N_TOK, D_MODEL), f32),
            jax.ShapeDtypeStruct((N_TOK, D_MODEL), f32),
            jax.ShapeDtypeStruct((SUBLANES, N_TOK), f32),
            jax.ShapeDtypeStruct((SUBLANES, N_TOK), jnp.int32),
        ],
        compiler_params=_params(("parallel",)),
        name="outproj_router",
    )(merged, w_out, x2d, gain, w_router_split, b_router_pad)


def _moe_kernel(item_e_ref, item_start_ref, item_n_ref, tok_ref, dest_ref,
                hn_hbm, wg_ref, wl_ref, bg_ref, bl_ref, wd_ref, bd_ref, y_hbm,
                acc_ref, xg_ref, xb_ref, wgb_ref, wlb_ref, wdb_ref, gsem, ssem):
    w = pl.program_id(0)
    t = pl.program_id(1)
    last_w = pl.num_programs(0) - 1
    n = item_n_ref[w]
    start = item_start_ref[w]
    slot = lax.rem(w, 2)
    w_next = jnp.minimum(w + 1, last_w)
    start_next = item_start_ref[w_next]
    w_prev = jnp.maximum(w - 1, 0)
    start_prev = item_start_ref[w_prev]
    n_prev = jnp.where(w > 0, item_n_ref[w_prev], 0)
    is_last_live = jnp.logical_or(w == last_w, item_n_ref[w_next] == 0)

    def gather_row(row, base):
        tk = tok_ref[base + row]
        pltpu.make_async_copy(hn_hbm.at[pl.ds(tk, 1), :], xg_ref.at[pl.ds(row, 1), :], gsem).start()

    def gather_wait():
        pltpu.make_async_copy(hn_hbm.at[pl.ds(0, MOE_ROWS), :], xg_ref, gsem).wait()

    def scatter_row(row, base, cnt, sl, priority=0):
        d = jnp.where(row < cnt, dest_ref[base + row], N_ASSIGN + row)
        pltpu.make_async_copy(acc_ref.at[sl, pl.ds(row, 1), :], y_hbm.at[pl.ds(d, 1), :],
                              ssem).start(priority=priority)

    def scatter_wait(sl):
        pltpu.make_async_copy(acc_ref.at[sl], y_hbm.at[pl.ds(0, MOE_ROWS), :], ssem).wait()

    @pl.when(jnp.logical_and(w == 0, t == 0))
    def _():
        acc_ref[1] = jnp.zeros((MOE_ROWS, D_MODEL), f32)

        def issue(r, c):
            gather_row(r, start)
            return c
        lax.fori_loop(0, MOE_ROWS, issue, 0, unroll=8)

    @pl.when(jnp.logical_and(t == 0, n > 0))
    def _():
        gather_wait()
        xb_ref[...] = xg_ref[...].astype(bf16)
        acc_ref[slot] = jnp.broadcast_to(bd_ref[...], (MOE_ROWS, D_MODEL))

    def dma_chunk(r):
        row0 = t * (MOE_NSUB * MOE_CHUNK) + r * MOE_CHUNK
        for k in range(MOE_CHUNK):
            gather_row(row0 + k, start_next)
        for k in range(MOE_CHUNK):
            scatter_row(row0 + k, start_prev, n_prev, 1 - slot, priority=k % 2)

    def ffn_up(r0, rows):
        xs = xb_ref[r0:r0 + rows, :]
        hg = jnp.dot(xs, wgb_ref[...], preferred_element_type=f32) + bg_ref[...]
        hl = jnp.dot(xs, wlb_ref[...], preferred_element_type=f32) + bl_ref[...]
        return hg, hl

    def ffn_down(r0, rows, hg, hl):
        hg = jnp.minimum(hg, SWIGLU_LIMIT)
        hl = jnp.clip(hl, -SWIGLU_LIMIT, SWIGLU_LIMIT)
        act = hg * jax.nn.sigmoid(SWIGLU_ALPHA * hg) * (hl + 1.0)
        acc_ref[slot, r0:r0 + rows, :] += jnp.dot(act.astype(bf16), wdb_ref[...], preferred_element_type=f32)

    def ffn_rows(r0, rows):
        ffn_down(r0, rows, *ffn_up(r0, rows))

    @pl.when(n > 0)
    def _():
        nb_full = n // MOE_SUB
        rem = n - nb_full * MOE_SUB
        ext = jnp.logical_and(jnp.logical_and(rem > 0, rem <= MOE_HALF), nb_full >= 1)
        own_block = jnp.logical_or(rem > MOE_HALF, jnp.logical_and(rem > 0, nb_full == 0))
        nb = nb_full + jnp.where(own_block, 1, 0)
        tiny = jnp.logical_and(nb_full == 0, rem <= MOE_HALF)

        def cast_weights(r):
            if r == 0:
                wgb_ref[...] = wg_ref[...].astype(bf16)
                wlb_ref[...] = wl_ref[...].astype(bf16)
                wdb_ref[...] = wd_ref[...].astype(bf16)

        def single(r, rows, chunks=1):
            def body():
                for c in range(chunks):
                    dma_chunk(r + c)
                cast_weights(r)
                ffn_rows(r * MOE_SUB, rows)
            return body

        def pair(r, rows_b):
            def body():
                dma_chunk(r)
                dma_chunk(r + 1)
                cast_weights(r)
                ra, rb = r * MOE_SUB, (r + 1) * MOE_SUB
                ha = ffn_up(ra, MOE_SUB)
                hb = ffn_up(rb, rows_b)
                ffn_down(ra, MOE_SUB, *ha)
                ffn_down(rb, rows_b, *hb)
            return body

        def idle(*rs):
            def body():
                for r in rs:
                    dma_chunk(r)
            return body

        def last_single(r):
            return lambda: lax.cond(
                ext, single(r, MOE_SUB + MOE_HALF, 2),
                lambda: lax.cond(tiny, single(r, MOE_HALF, 2), single(r, MOE_SUB, 2)))

        for r in range(0, MOE_NSUB - 1, 2):
            is_last_pair = jnp.logical_and(ext, nb == r + 2)
            lax.cond(
                nb >= r + 2,
                lambda r=r, is_last_pair=is_last_pair: lax.cond(
                    is_last_pair, pair(r, MOE_SUB + MOE_HALF), pair(r, MOE_SUB)),
                lambda r=r: lax.cond(nb == r + 1, last_single(r), idle(r, r + 1)))
        for r in range(MOE_NSUB - MOE_NSUB % 2, MOE_NSUB):
            lax.cond(nb == r + 1, single(r, MOE_SUB), idle(r))

    @pl.when(jnp.logical_and(t == MOE_NF - 1, n > 0))
    def _():
        scatter_wait(1 - slot)

        @pl.when(is_last_live)
        def _():
            def issue(r, c):
                scatter_row(r, start, n, slot)
                return c
            lax.fori_loop(0, MOE_ROWS, issue, 0, unroll=8)
            scatter_wait(slot)
            gather_wait()


def _moe(n_items, item_e, item_start, item_n, tok_sorted, dest_sorted, hn2, w_gate_up, b_gate_up, w_down, b_down):
    tf = MOE_TF
    live_t = lambda w, t, n: jnp.where(n[w] > 0, t, MOE_NF - 1)
    grid_spec = pltpu.PrefetchScalarGridSpec(
        num_scalar_prefetch=5,
        grid=(n_items, MOE_NF),
        in_specs=[
            pl.BlockSpec(memory_space=pl.ANY),
            pl.BlockSpec((None, D_MODEL, tf), lambda w, t, e, s, n, tk, ds: (e[w], 0, live_t(w, t, n))),
            pl.BlockSpec((None, D_MODEL, tf), lambda w, t, e, s, n, tk, ds: (e[w], 0, MOE_NF + live_t(w, t, n))),
            pl.BlockSpec((None, 1, tf), lambda w, t, e, s, n, tk, ds: (e[w], 0, live_t(w, t, n))),
            pl.BlockSpec((None, 1, tf), lambda w, t, e, s, n, tk, ds: (e[w], 0, MOE_NF + live_t(w, t, n))),
            pl.BlockSpec((None, tf, D_MODEL), lambda w, t, e, s, n, tk, ds: (e[w], live_t(w, t, n), 0)),
            pl.BlockSpec((None, 1, D_MODEL), lambda w, t, e, s, n, tk, ds: (e[w], 0, 0)),
        ],
        out_specs=pl.BlockSpec(memory_space=pl.ANY),
        scratch_shapes=[
            pltpu.VMEM((2, MOE_ROWS, D_MODEL), f32),
            pltpu.VMEM((MOE_ROWS, D_MODEL), f32),
            pltpu.VMEM((MOE_ROWS, D_MODEL), bf16),
            pltpu.VMEM((D_MODEL, tf), bf16),
            pltpu.VMEM((D_MODEL, tf), bf16),
            pltpu.VMEM((tf, D_MODEL), bf16),
            pltpu.SemaphoreType.DMA,
            pltpu.SemaphoreType.DMA,
        ],
    )
    return pl.pallas_call(
        _moe_kernel,
        grid_spec=grid_spec,
        out_shape=jax.ShapeDtypeStruct((Y_ROWS, D_MODEL), f32),
        compiler_params=_params(("arbitrary", "arbitrary"), BIG_VMEM_LIMIT_BYTES),
        name="moe_experts",
    )(item_e, item_start, item_n, tok_sorted, dest_sorted,
      hn2, w_gate_up, w_gate_up, b_gate_up, b_gate_up, w_down, b_down)


def _combine_kernel(y0_ref, y1_ref, y2_ref, y3_ref, h_ref, gate_ref, g_ref, o_ref):
    h = h_ref[...]
    gates = gate_ref[...]
    for k, y_ref in enumerate((y0_ref, y1_ref, y2_ref, y3_ref)):
        h = h + y_ref[...] * gates[:, k:k + 1]
    o_ref[...] = h * lax.rsqrt(jnp.mean(h * h, axis=-1, keepdims=True) + RMS_EPS) * g_ref[...]


def _combine(y, h1, gates, gain, tm=256):
    slot_spec = lambda k: pl.BlockSpec((tm, D_MODEL), lambda i, k=k: (k * (N_TOK // tm) + i, 0))
    return pl.pallas_call(
        _combine_kernel,
        grid=(N_TOK // tm,),
        in_specs=[slot_spec(k) for k in range(TOP_K)] + [
            pl.BlockSpec((tm, D_MODEL), lambda i: (i, 0)),
            pl.BlockSpec((tm, TOP_K), lambda i: (i, 0)),
            pl.BlockSpec((1, D_MODEL), lambda i: (0, 0)),
        ],
        out_specs=pl.BlockSpec((tm, D_MODEL), lambda i: (i, 0)),
        out_shape=jax.ShapeDtypeStruct((N_TOK, D_MODEL), f32),
        compiler_params=_params(("parallel",)),
        name="combine_norm",
    )(y, y, y, y, h1, gates, gain)


def _route(top_idx):
    expert_flat = top_idx.reshape(-1)
    order = jnp.argsort(expert_flat, stable=True).astype(jnp.int32)
    counts = jnp.zeros((N_EXPERTS,), jnp.int32).at[expert_flat].add(1)
    starts = jnp.cumsum(counts) - counts
    tok_sorted = order // TOP_K
    dest_sorted = (order % TOP_K) * N_TOK + tok_sorted
    tok_sorted = jnp.pad(tok_sorted, (0, MOE_ROWS))
    dest_sorted = jnp.pad(dest_sorted, (0, MOE_ROWS))
    chunks = (counts + MOE_ROWS - 1) // MOE_ROWS
    chunk_end = jnp.cumsum(chunks)
    n_items = chunk_end[-1]
    w = jnp.arange(MOE_ITEMS, dtype=jnp.int32)
    w_eff = jnp.minimum(w, n_items - 1)
    e = jnp.sum(w_eff[:, None] >= chunk_end[None, :], axis=1).astype(jnp.int32)
    c = w_eff - (chunk_end - chunks)[e]
    item_start = starts[e] + c * MOE_ROWS
    item_n = jnp.where(w < n_items, jnp.clip(counts[e] - c * MOE_ROWS, 0, MOE_ROWS), 0)
    return n_items, e, item_start.astype(jnp.int32), item_n.astype(jnp.int32), tok_sorted, dest_sorted


def kernel(x, meta_tokens, rel_bias_table, attn_norm, w_in, fox_forget_bias, lam_q1, lam_k1, lam_q2, lam_k2,
           diff_subln, w_branch_fox, w_branch_diff, w_out, ffn_norm, w_router, b_router, w_gate_up, b_gate_up,
           w_down, b_down, final_norm):
    x2d = x.reshape(N_TOK, D_MODEL)
    w_t = jnp.swapaxes(w_in[0], 0, 1)
    fb_pad = jnp.pad(fox_forget_bias[0], (0, LANES - FOX_HEADS)).reshape(1, LANES)

    meta_pad = jnp.pad(meta_tokens, ((0, N_META_PAD - N_META), (0, 0)))
    proj, proj_meta, ff_real, ff_meta = _inproj(x2d, meta_pad, attn_norm, w_t, tm=1024, tn=512)

    ck, ckm = _forget_cumsum(ff_real, ff_meta, fb_pad)
    o_fox = _fox_attention(proj, proj_meta, ck, ckm)

    near, mbias = _bias_tiles(rel_bias_table)
    lam_vecs = jnp.concatenate([lam_q1, lam_k1, lam_q2, lam_k2], axis=0)
    o_diff = _diff_attention(proj, proj_meta, near, mbias, lam_vecs, diff_subln)

    merged = _gated_merge(o_fox, o_diff, w_branch_fox[0].astype(bf16), w_branch_diff[0].astype(bf16), proj)
    w_router_pad = jnp.pad(w_router[0], ((0, 0), (0, LANES - N_EXPERTS)))
    w_router_hi = w_router_pad.astype(bf16)
    w_router_lo = (w_router_pad - w_router_hi.astype(f32)).astype(bf16)
    w_router_split = jnp.concatenate([w_router_hi, w_router_lo], axis=1)
    b_router_pad = jnp.pad(b_router[0], (0, LANES - N_EXPERTS)).reshape(1, LANES)
    h1, hn2, gates_t, idx_t = _outproj(merged, w_out[0].astype(bf16), x2d, ffn_norm, w_router_split, b_router_pad)
    gates = gates_t[:TOP_K].T

    n_items, item_e, item_start, item_n, tok_sorted, dest_sorted = _route(idx_t[:TOP_K].T)
    y = _moe(n_items, item_e, item_start, item_n, tok_sorted, dest_sorted, hn2,
             w_gate_up[0], b_gate_up[0].reshape(N_EXPERTS, 1, 2 * D_EXPERT),
             w_down[0], b_down[0].reshape(N_EXPERTS, 1, D_MODEL))
    out = _combine(y, h1, gates, final_norm.reshape(1, D_MODEL))
    return out.reshape(BATCH, SEQ, D_MODEL)
```

```python
import functools
import math

import jax
import jax.numpy as jnp
from jax import lax
from jax.experimental import pallas as pl
from jax.experimental.pallas import tpu as pltpu

D_MODEL = 2048
BATCH = 4
SEQ = 2048
N_TOK = BATCH * SEQ
N_META = 16
N_META_PAD = 128
HEAD_DIM = 128
FOX_HEADS = 8
DIFF_HEADS = 4
DIFF_V_DIM = 2 * HEAD_DIM
FOX_WIDTH = FOX_HEADS * HEAD_DIM
DIFF_QK_WIDTH = DIFF_HEADS * 2 * HEAD_DIM
DIFF_WIDTH = DIFF_HEADS * DIFF_V_DIM
N_BUCKETS = 32
MAX_DISTANCE = 128
N_EXPERTS = 32
TOP_K = 4
D_EXPERT = D_MODEL
SWIGLU_LIMIT = 7.0
SWIGLU_ALPHA = 1.702
RMS_EPS = 1e-5
NEG_INF = -1e30
LAMBDA_INIT = 0.8 - 0.6 * math.exp(-0.3 * 0)
ATTN_SCALE = HEAD_DIM ** -0.5
LOG2E = math.log2(math.e)
Q_PRESCALE = ATTN_SCALE * LOG2E

LANES = 128
SUBLANES = 8
VMEM_LIMIT_BYTES = 56 * 1024 * 1024
BIG_VMEM_LIMIT_BYTES = 60 * 1024 * 1024

COL_FQ = 0
COL_FK = COL_FQ + FOX_WIDTH
COL_FV = COL_FK + FOX_WIDTH
COL_DQ = COL_FV + FOX_WIDTH
COL_DK = COL_DQ + DIFF_QK_WIDTH
COL_DV = COL_DK + DIFF_QK_WIDTH
COL_GF = COL_DV + DIFF_WIDTH
COL_GD = COL_GF + D_MODEL
D_PROJ = COL_GD + D_MODEL

INPROJ_XT = 256
ATTN_TILE = 512
N_QBLK = SEQ // ATTN_TILE

MOE_ROWS = 1280
MOE_SUB = 256
MOE_HALF = MOE_SUB // 2
MOE_NSUB = MOE_ROWS // MOE_SUB
MOE_TF = 256
MOE_NF = D_EXPERT // MOE_TF
MOE_CHUNK = MOE_ROWS // (MOE_NF * MOE_NSUB)
N_ASSIGN = N_TOK * TOP_K
MOE_ITEMS = N_EXPERTS + N_ASSIGN // MOE_ROWS
Y_ROWS = N_ASSIGN + MOE_ROWS

f32 = jnp.float32
bf16 = jnp.bfloat16


def _params(sem, vmem=VMEM_LIMIT_BYTES):
    return pltpu.CompilerParams(dimension_semantics=sem, vmem_limit_bytes=vmem)


def _rms_bf16(x, g):
    return (x * lax.rsqrt(jnp.mean(x * x, axis=-1, keepdims=True) + RMS_EPS) * g).astype(bf16)


def _inproj_kernel(x_hbm, xm_ref, g_ref, wt_ref, wfft_ref, o_ref, om_ref, ff_ref, ffm_ref,
                   xn_ref, xnm_ref, xt_ref, wb_ref, wffb_ref, sem, *, tm, tn):
    j = pl.program_id(0)
    i = pl.program_id(1)
    rows = pl.ds(pl.multiple_of(i * tm, tm), tm)
    is_q = jnp.logical_or(j < COL_FK // tn, jnp.logical_and(j >= COL_DQ // tn, j < COL_DK // tn))
    qs = jnp.where(is_q, Q_PRESCALE, 1.0)

    @pl.when(jnp.logical_and(j == 0, i == 0))
    def _():
        wffb_ref[...] = wfft_ref[...].T.astype(bf16)
        xnm_ref[...] = _rms_bf16(xm_ref[...], g_ref[...])
        ffm_ref[...] = jnp.dot(xnm_ref[...], wffb_ref[...], preferred_element_type=f32)

    @pl.when(i == 0)
    def _():
        wb_ref[...] = wt_ref[...].T.astype(bf16)
        ym = jnp.dot(xnm_ref[...], wb_ref[...], preferred_element_type=f32)
        om_ref[...] = (ym * qs).astype(bf16)

    @pl.when(j == 0)
    def _():
        def piece(s):
            r0 = pl.multiple_of(i * tm + s * INPROJ_XT, INPROJ_XT)
            return pltpu.make_async_copy(x_hbm.at[pl.ds(r0, INPROJ_XT), :], xt_ref.at[s % 2], sem.at[s % 2])

        n_pieces = tm // INPROJ_XT
        piece(0).start()
        for s in range(n_pieces):
            if s + 1 < n_pieces:
                piece(s + 1).start()
            piece(s).wait()
            xn = _rms_bf16(xt_ref[s % 2], g_ref[...])
            xn_ref[pl.ds(pl.multiple_of(i * tm + s * INPROJ_XT, INPROJ_XT), INPROJ_XT), :] = xn
            ff_ref[s * INPROJ_XT:(s + 1) * INPROJ_XT, :] = jnp.dot(xn, wffb_ref[...], preferred_element_type=f32)

    y = jnp.dot(xn_ref[rows, :], wb_ref[...], preferred_element_type=f32)
    o_ref[...] = (y * qs).astype(bf16)


def _inproj(x2d, meta_pad, gain, w_t, tm, tn):
    n = x2d.shape[0]
    n_i = n // tm
    w_row = lambda j: SUBLANES * (j * (tn // SUBLANES) + jnp.where(j * tn >= COL_DQ, FOX_HEADS // SUBLANES, 0))
    first_pass_row = lambda j, i: jnp.where(j == 0, i, n_i - 1)
    return pl.pallas_call(
        functools.partial(_inproj_kernel, tm=tm, tn=tn),
        grid=(D_PROJ // tn, n_i),
        in_specs=[
            pl.BlockSpec(memory_space=pl.ANY),
            pl.BlockSpec((N_META_PAD, D_MODEL), lambda j, i: (0, 0)),
            pl.BlockSpec((1, D_MODEL), lambda j, i: (0, 0)),
            pl.BlockSpec((pl.Element(tn), pl.Element(D_MODEL)), lambda j, i: (w_row(j), 0)),
            pl.BlockSpec((pl.Element(LANES), pl.Element(D_MODEL)), lambda j, i: (COL_DQ, 0)),
        ],
        out_specs=[
            pl.BlockSpec((tm, tn), lambda j, i: (i, j)),
            pl.BlockSpec((N_META_PAD, tn), lambda j, i: (0, j)),
            pl.BlockSpec((tm, LANES), lambda j, i: (first_pass_row(j, i), 0)),
            pl.BlockSpec((N_META_PAD, LANES), lambda j, i: (0, 0)),
        ],
        out_shape=[
            jax.ShapeDtypeStruct((n, D_PROJ), bf16),
            jax.ShapeDtypeStruct((N_META_PAD, D_PROJ), bf16),
            jax.ShapeDtypeStruct((n, LANES), f32),
            jax.ShapeDtypeStruct((N_META_PAD, LANES), f32),
        ],
        scratch_shapes=[
            pltpu.VMEM((n, D_MODEL), bf16),
            pltpu.VMEM((N_META_PAD, D_MODEL), bf16),
            pltpu.VMEM((2, INPROJ_XT, D_MODEL), f32),
            pltpu.VMEM((D_MODEL, tn), bf16),
            pltpu.VMEM((D_MODEL, LANES), bf16),
            pltpu.SemaphoreType.DMA((2,)),
        ],
        compiler_params=_params(("arbitrary", "arbitrary"), BIG_VMEM_LIMIT_BYTES),
        name="inproj",
    )(x2d, meta_pad, gain, w_t, w_t)


def _log_sigmoid(x):
    return jnp.minimum(x, 0.0) - jnp.log(1.0 + jnp.exp(-jnp.abs(x)))


def _lane_cumsum(x):
    n = x.shape[-1]
    lane = lax.broadcasted_iota(jnp.int32, x.shape, x.ndim - 1)
    s = 1
    while s < n:
        x = x + jnp.where(lane >= s, pltpu.roll(x, s, x.ndim - 1), 0.0)
        s *= 2
    return x


def _cum_kernel(ff_ref, ffm_ref, fb_ref, ck_ref, ckm_ref):
    fb = fb_ref[...]
    row = lax.broadcasted_iota(jnp.int32, (LANES, LANES), 0)
    lfm = jnp.where(row < N_META, _log_sigmoid(ffm_ref[...] + fb), 0.0)
    cm = _lane_cumsum(lfm.T)
    m_total = cm[:, N_META - 1:N_META]
    lf = _log_sigmoid(ff_ref[...] + fb)
    cr = _lane_cumsum(lf.T) + m_total
    ck_ref[...] = cr[:SUBLANES] * LOG2E
    ckm_ref[...] = cm[:SUBLANES] * LOG2E


def _forget_cumsum(ff_real, ff_meta_pad, fb_pad):
    return pl.pallas_call(
        _cum_kernel,
        grid=(BATCH,),
        in_specs=[
            pl.BlockSpec((SEQ, LANES), lambda b: (b, 0)),
            pl.BlockSpec((LANES, LANES), lambda b: (0, 0)),
            pl.BlockSpec((1, LANES), lambda b: (0, 0)),
        ],
        out_specs=[
            pl.BlockSpec((None, SUBLANES, SEQ), lambda b: (b, 0, 0)),
            pl.BlockSpec((None, SUBLANES, LANES), lambda b: (b, 0, 0)),
        ],
        out_shape=[
            jax.ShapeDtypeStruct((BATCH, FOX_HEADS, SEQ), f32),
            jax.ShapeDtypeStruct((BATCH, FOX_HEADS, LANES), f32),
        ],
        compiler_params=_params(("parallel",)),
        name="forget_cumsum",
    )(ff_real, ff_meta_pad, fb_pad)


def _qk(q, k):
    return lax.dot_general(q, k, (((1,), (1,)), ((), ())), preferred_element_type=f32)


def _row_to_col(row):
    t = row.shape[-1]
    r = lax.broadcasted_iota(jnp.int32, (t, t), 0)
    c = lax.broadcasted_iota(jnp.int32, (t, t), 1)
    return jnp.sum(jnp.where(r == c, row, 0.0), axis=1, keepdims=True)


def _softmax_pieces(pieces, row_shift=None):
    m = None
    for s, _ in pieces:
        mi = jnp.max(s, axis=-1, keepdims=True)
        m = mi if m is None else jnp.maximum(m, mi)
    if row_shift is not None:
        m = (m + row_shift) - row_shift
    l = acc = None
    for s, v in pieces:
        p = jnp.exp2(s - m)
        li = jnp.sum(p, axis=-1, keepdims=True)
        ai = jnp.dot(p.astype(bf16), v, preferred_element_type=f32)
        l = li if l is None else l + li
        acc = ai if acc is None else acc + ai
    return acc / l


def _switch_qblock(i, branches):
    def pick(lo, hi):
        if lo == hi:
            return branches[lo]
        mid = (lo + hi) // 2
        return lambda: lax.cond(i <= mid, pick(lo, mid), pick(mid + 1, hi))
    pick(0, len(branches) - 1)()


FOX_GROUP = 4


def _fox_kernel(q_ref, k_ref, v_ref, km_ref, vm_ref, ck_ref, ckm_ref, o_ref):
    hg = pl.program_id(1)
    i = pl.program_id(2)
    t = ATTN_TILE
    q0 = pl.multiple_of(i * t, t)
    meta_ok = lax.broadcasted_iota(jnp.int32, (t, N_META_PAD), 1) < N_META
    r = lax.broadcasted_iota(jnp.int32, (t, t), 0)
    c = lax.broadcasted_iota(jnp.int32, (t, t), 1)

    def attend(n_far):
        def body():
            for g in range(FOX_GROUP):
                col = slice(g * HEAD_DIM, (g + 1) * HEAD_DIM)
                head = pl.ds(hg * FOX_GROUP + g, 1)
                q = q_ref[:, col]
                cq = _row_to_col(ck_ref[head, pl.ds(q0, t)])
                u_meta = jnp.where(meta_ok, _qk(q, km_ref[:, col]) - ckm_ref[head, :], NEG_INF)
                pieces = [(u_meta, vm_ref[:, col])]
                if n_far:
                    pieces.append((_qk(q, k_ref[0:n_far, col]) - ck_ref[head, 0:n_far], v_ref[0:n_far, col]))
                u_diag = _qk(q, k_ref[n_far:n_far + t, col]) - ck_ref[head, n_far:n_far + t]
                pieces.append((jnp.where(c <= r, u_diag, NEG_INF), v_ref[n_far:n_far + t, col]))
                o_ref[:, col] = _softmax_pieces(pieces, row_shift=cq).astype(bf16)
        return body

    _switch_qblock(i, [attend(ib * t) for ib in range(N_QBLK)])


def _fox_attention(proj, proj_meta, ck, ckm):
    t = ATTN_TILE
    w = FOX_GROUP * HEAD_DIM
    cb = lambda col: col // w
    return pl.pallas_call(
        _fox_kernel,
        grid=(BATCH, FOX_HEADS // FOX_GROUP, N_QBLK),
        in_specs=[
            pl.BlockSpec((t, w), lambda b, h, i: (b * N_QBLK + i, cb(COL_FQ) + h)),
            pl.BlockSpec((SEQ, w), lambda b, h, i: (b, cb(COL_FK) + h)),
            pl.BlockSpec((SEQ, w), lambda b, h, i: (b, cb(COL_FV) + h)),
            pl.BlockSpec((N_META_PAD, w), lambda b, h, i: (0, cb(COL_FK) + h)),
            pl.BlockSpec((N_META_PAD, w), lambda b, h, i: (0, cb(COL_FV) + h)),
            pl.BlockSpec((None, FOX_HEADS, SEQ), lambda b, h, i: (b, 0, 0)),
            pl.BlockSpec((None, FOX_HEADS, LANES), lambda b, h, i: (b, 0, 0)),
        ],
        out_specs=pl.BlockSpec((t, w), lambda b, h, i: (b * N_QBLK + i, h)),
        out_shape=jax.ShapeDtypeStruct((N_TOK, FOX_WIDTH), bf16),
        compiler_params=_params(("parallel", "parallel", "arbitrary")),
        name="fox_attention",
    )(proj, proj, proj, proj_meta, proj_meta, ck, ckm)


def _t5_bias(dist, table_ref, h):
    n = jnp.maximum(dist, 0)
    max_exact = N_BUCKETS // 2
    log_part = jnp.log(jnp.maximum(n, 1).astype(f32) / max_exact) / math.log(MAX_DISTANCE / max_exact)
    v = log_part * (N_BUCKETS - max_exact)
    far = table_ref[N_BUCKETS - 1, h]
    val = lambda b: (table_ref[b, h] - far) * LOG2E
    large = jnp.zeros(dist.shape, f32)
    for b in range(N_BUCKETS - 2, max_exact - 1, -1):
        large = jnp.where(v < b + 1 - max_exact, val(b), large)
    out = large
    for b in range(max_exact):
        out = jnp.where(n == b, val(b), out)
    return out


def _bias_kernel(table_ref, near_ref, meta_ref):
    h = pl.program_id(0)
    t = ATTN_TILE
    r = lax.broadcasted_iota(jnp.int32, (t, 2 * t), 0)
    c = lax.broadcasted_iota(jnp.int32, (t, 2 * t), 1)
    near_ref[...] = _t5_bias(r + t - c, table_ref, h)
    r = lax.broadcasted_iota(jnp.int32, (t, LANES), 0)
    c = lax.broadcasted_iota(jnp.int32, (t, LANES), 1)
    meta_ref[...] = _t5_bias(N_META + r - c, table_ref, h)


def _bias_tiles(table):
    t = ATTN_TILE
    return pl.pallas_call(
        _bias_kernel,
        grid=(DIFF_HEADS,),
        in_specs=[pl.BlockSpec(memory_space=pltpu.SMEM)],
        out_specs=[
            pl.BlockSpec((None, t, 2 * t), lambda h: (h, 0, 0)),
            pl.BlockSpec((None, t, LANES), lambda h: (h, 0, 0)),
        ],
        out_shape=[
            jax.ShapeDtypeStruct((DIFF_HEADS, t, 2 * t), f32),
            jax.ShapeDtypeStruct((DIFF_HEADS, t, LANES), f32),
        ],
        compiler_params=_params(("arbitrary",)),
        name="t5_bias_tiles",
    )(table)


def _diff_kernel(q1_ref, q2_ref, k1_ref, k2_ref, v_ref, k1m_ref, k2m_ref, vm_ref,
                 near_ref, mbias_ref, lam_ref, subln_ref, o_ref):
    i = pl.program_id(2)
    t = ATTN_TILE
    q1 = q1_ref[...]
    q2 = q2_ref[...]
    lam = (jnp.exp(jnp.sum(lam_ref[0:1, :] * lam_ref[1:2, :], axis=-1, keepdims=True))
           - jnp.exp(jnp.sum(lam_ref[2:3, :] * lam_ref[3:4, :], axis=-1, keepdims=True))
           + LAMBDA_INIT)

    mb = jnp.where(i == 0, mbias_ref[...], 0.0)
    meta_ok = lax.broadcasted_iota(jnp.int32, (t, N_META_PAD), 1) < N_META
    vm = vm_ref[...]

    def attend(n_far, n_near):
        def body():
            r = lax.broadcasted_iota(jnp.int32, (t, n_near), 0)
            c = lax.broadcasted_iota(jnp.int32, (t, n_near), 1)
            mask = c <= r + (n_near - t)
            bias = near_ref[:, 2 * t - n_near:]
            outs = []
            for q, k_ref, km_ref in ((q1, k1_ref, k1m_ref), (q2, k2_ref, k2m_ref)):
                pieces = [(jnp.where(meta_ok, _qk(q, km_ref[...]) + mb, NEG_INF), vm)]
                if n_far:
                    pieces.append((_qk(q, k_ref[0:n_far, :]), v_ref[0:n_far, :]))
                s_near = jnp.where(mask, _qk(q, k_ref[n_far:n_far + n_near, :]) + bias, NEG_INF)
                pieces.append((s_near, v_ref[n_far:n_far + n_near, :]))
                outs.append(_softmax_pieces(pieces))
            o = outs[0] - lam * outs[1]
            y = o * lax.rsqrt(jnp.mean(o * o, axis=-1, keepdims=True) + RMS_EPS) * subln_ref[...]
            o_ref[...] = (y * (1.0 - LAMBDA_INIT)).astype(bf16)
        return body

    branches = []
    for ib in range(N_QBLK):
        n_near = min(2 * t, (ib + 1) * t)
        branches.append(attend((ib + 1) * t - n_near, n_near))
    _switch_qblock(i, branches)


def _diff_attention(proj, proj_meta, near, mbias, lam_vecs, subln):
    t = ATTN_TILE
    cb = lambda col: col // HEAD_DIM
    vb = lambda col: col // DIFF_V_DIM
    row = lambda b, h, i: b * N_QBLK + i
    return pl.pallas_call(
        _diff_kernel,
        grid=(BATCH, DIFF_HEADS, N_QBLK),
        in_specs=[
            pl.BlockSpec((t, HEAD_DIM), lambda b, h, i: (row(b, h, i), cb(COL_DQ) + 2 * h)),
            pl.BlockSpec((t, HEAD_DIM), lambda b, h, i: (row(b, h, i), cb(COL_DQ) + 2 * h + 1)),
            pl.BlockSpec((SEQ, HEAD_DIM), lambda b, h, i: (b, cb(COL_DK) + 2 * h)),
            pl.BlockSpec((SEQ, HEAD_DIM), lambda b, h, i: (b, cb(COL_DK) + 2 * h + 1)),
            pl.BlockSpec((SEQ, DIFF_V_DIM), lambda b, h, i: (b, vb(COL_DV) + h)),
            pl.BlockSpec((N_META_PAD, HEAD_DIM), lambda b, h, i: (0, cb(COL_DK) + 2 * h)),
            pl.BlockSpec((N_META_PAD, HEAD_DIM), lambda b, h, i: (0, cb(COL_DK) + 2 * h + 1)),
            pl.BlockSpec((N_META_PAD, DIFF_V_DIM), lambda b, h, i: (0, vb(COL_DV) + h)),
            pl.BlockSpec((None, t, 2 * t), lambda b, h, i: (h, 0, 0)),
            pl.BlockSpec((None, t, LANES), lambda b, h, i: (h, 0, 0)),
            pl.BlockSpec((4, HEAD_DIM), lambda b, h, i: (0, 0)),
            pl.BlockSpec((1, DIFF_V_DIM), lambda b, h, i: (0, 0)),
        ],
        out_specs=pl.BlockSpec((t, DIFF_V_DIM), lambda b, h, i: (row(b, h, i), h)),
        out_shape=jax.ShapeDtypeStruct((N_TOK, DIFF_WIDTH), bf16),
        compiler_params=_params(("parallel", "parallel", "arbitrary")),
        name="diff_attention",
    )(proj, proj, proj, proj, proj, proj_meta, proj_meta, proj_meta, near, mbias, lam_vecs, subln)


def _merge_kernel(of_ref, od_ref, wf_ref, wd_ref, gf_ref, gd_ref, o_ref):
    yf = jnp.dot(of_ref[...], wf_ref[...], preferred_element_type=f32)
    yd = jnp.dot(od_ref[...], wd_ref[...], preferred_element_type=f32)
    gf = jax.nn.sigmoid(gf_ref[...].astype(f32))
    gd = jax.nn.sigmoid(gd_ref[...].astype(f32))
    o_ref[...] = (gf * yf + gd * yd).astype(bf16)


def _gated_merge(o_fox, o_diff, w_bf, w_bd, proj, tm=1024, tn=1024):
    return pl.pallas_call(
        _merge_kernel,
        grid=(D_MODEL // tn, N_TOK // tm),
        in_specs=[
            pl.BlockSpec((tm, FOX_WIDTH), lambda j, i: (i, 0)),
            pl.BlockSpec((tm, DIFF_WIDTH), lambda j, i: (i, 0)),
            pl.BlockSpec((FOX_WIDTH, tn), lambda j, i: (0, j)),
            pl.BlockSpec((DIFF_WIDTH, tn), lambda j, i: (0, j)),
            pl.BlockSpec((tm, tn), lambda j, i: (i, COL_GF // tn + j)),
            pl.BlockSpec((tm, tn), lambda j, i: (i, COL_GD // tn + j)),
        ],
        out_specs=pl.BlockSpec((tm, tn), lambda j, i: (i, j)),
        out_shape=jax.ShapeDtypeStruct((N_TOK, D_MODEL), bf16),
        compiler_params=_params(("parallel", "arbitrary")),
        name="gated_merge",
    )(o_fox, o_diff, w_bf, w_bd, proj, proj)


def _outproj_kernel(m_ref, w_ref, x_ref, g_ref, wr_ref, br_ref, h_ref, hn_ref, gate_ref, idx_ref):
    h1 = x_ref[...] + jnp.dot(m_ref[...], w_ref[...], preferred_element_type=f32)
    h_ref[...] = h1
    hn = h1 * lax.rsqrt(jnp.mean(h1 * h1, axis=-1, keepdims=True) + RMS_EPS) * g_ref[...]
    hn_ref[...] = hn
    hn_hi = hn.astype(bf16)
    hn_lo = (hn - hn_hi.astype(f32)).astype(bf16)
    a = jnp.dot(hn_hi, wr_ref[...], preferred_element_type=f32)
    b = jnp.dot(hn_lo, wr_ref[:, :LANES], preferred_element_type=f32)
    lg = a[:, :LANES] + a[:, LANES:] + b + br_ref[...]

    lgt = lg.T[:N_EXPERTS]
    row = lax.broadcasted_iota(jnp.int32, lgt.shape, 0)
    cur = lgt
    vals, idxs = [], []
    for _ in range(TOP_K):
        m = jnp.max(cur, axis=0, keepdims=True)
        idx = jnp.min(jnp.where(cur == m, row, N_EXPERTS), axis=0, keepdims=True)
        vals.append(m)
        idxs.append(idx)
        cur = jnp.where(row == idx, -jnp.inf, cur)
    es = [jnp.exp(v - vals[0]) for v in vals]
    den = es[0] + es[1] + es[2] + es[3]
    slot = lax.broadcasted_iota(jnp.int32, gate_ref.shape, 0)
    gate_out = jnp.zeros(gate_ref.shape, f32)
    idx_out = jnp.zeros(idx_ref.shape, jnp.int32)
    for k in range(TOP_K):
        gate_out = jnp.where(slot == k, es[k] / den, gate_out)
        idx_out = jnp.where(slot == k, idxs[k], idx_out)
    gate_ref[...] = gate_out
    idx_ref[...] = idx_out


def _outproj(merged, w_out, x2d, gain, w_router_split, b_router_pad, tm=512):
    return pl.pallas_call(
        _outproj_kernel,
        grid=(N_TOK // tm,),
        in_specs=[
            pl.BlockSpec((tm, D_MODEL), lambda i: (i, 0)),
            pl.BlockSpec((D_MODEL, D_MODEL), lambda i: (0, 0)),
            pl.BlockSpec((tm, D_MODEL), lambda i: (i, 0)),
            pl.BlockSpec((1, D_MODEL), lambda i: (0, 0)),
            pl.BlockSpec((D_MODEL, 2 * LANES), lambda i: (0, 0)),
            pl.BlockSpec((1, LANES), lambda i: (0, 0)),
        ],
        out_specs=[
            pl.BlockSpec((tm, D_MODEL), lambda i: (i, 0)),
            pl.BlockSpec((tm, D_MODEL), lambda i: (i, 0)),
            pl.BlockSpec((SUBLANES, tm), lambda i: (0, i)),
            pl.BlockSpec((SUBLANES, tm), lambda i: (0, i)),
        ],
        out_shape=[
            jax.ShapeDtypeStruct((N_TOK, D_MODEL), f32),
            jax.ShapeDtypeStruct((N_TOK, D_MODEL), f32),
            jax.ShapeDtypeStruct((SUBLANES, N_TOK), f32),
            jax.ShapeDtypeStruct((SUBLANES, N_TOK), jnp.int32),
        ],
        compiler_params=_params(("parallel",)),
        name="outproj_router",
    )(merged, w_out, x2d, gain, w_router_split, b_router_pad)


def _moe_kernel(item_e_ref, item_start_ref, item_n_ref, tok_ref, dest_ref,
                hn_hbm, wg_ref, wl_ref, bg_ref, bl_ref, wd_ref, bd_ref, y_hbm,
                acc_ref, xg_ref, xb_ref, wgb_ref, wlb_ref, wdb_ref, gsem, ssem):
    w = pl.program_id(0)
    t = pl.program_id(1)
    last_w = pl.num_programs(0) - 1
    n = item_n_ref[w]
    start = item_start_ref[w]
    slot = lax.rem(w, 2)
    w_next = jnp.minimum(w + 1, last_w)
    start_next = item_start_ref[w_next]
    w_prev = jnp.maximum(w - 1, 0)
    start_prev = item_start_ref[w_prev]
    n_prev = jnp.where(w > 0, item_n_ref[w_prev], 0)
    is_last_live = jnp.logical_or(w == last_w, item_n_ref[w_next] == 0)

    def gather_row(row, base, priority=0):
        tk = tok_ref[base + row]
        pltpu.make_async_copy(hn_hbm.at[pl.ds(tk, 1), :], xg_ref.at[pl.ds(row, 1), :], gsem).start(priority=priority)

    def gather_wait():
        pltpu.make_async_copy(hn_hbm.at[pl.ds(0, MOE_ROWS), :], xg_ref, gsem).wait()

    def scatter_row(row, base, cnt, sl, priority=0):
        d = jnp.where(row < cnt, dest_ref[base + row], N_ASSIGN + row)
        pltpu.make_async_copy(acc_ref.at[sl, pl.ds(row, 1), :], y_hbm.at[pl.ds(d, 1), :],
                              ssem).start(priority=priority)

    def scatter_wait(sl):
        pltpu.make_async_copy(acc_ref.at[sl], y_hbm.at[pl.ds(0, MOE_ROWS), :], ssem).wait()

    @pl.when(jnp.logical_and(w == 0, t == 0))
    def _():
        acc_ref[1] = jnp.zeros((MOE_ROWS, D_MODEL), f32)

        def issue(r, c):
            gather_row(r, start)
            return c
        lax.fori_loop(0, MOE_ROWS, issue, 0, unroll=8)

    @pl.when(jnp.logical_and(t == 0, n > 0))
    def _():
        gather_wait()
        xb_ref[...] = xg_ref[...].astype(bf16)
        acc_ref[slot] = jnp.broadcast_to(bd_ref[...], (MOE_ROWS, D_MODEL))

    def dma_chunk(r):
        row0 = t * (MOE_NSUB * MOE_CHUNK) + r * MOE_CHUNK
        for k in range(MOE_CHUNK):
            gather_row(row0 + k, start_next, priority=k % 2)
        for k in range(MOE_CHUNK):
            scatter_row(row0 + k, start_prev, n_prev, 1 - slot, priority=k % 2)

    def ffn_up(r0, rows):
        xs = xb_ref[r0:r0 + rows, :]
        hg = jnp.dot(xs, wgb_ref[...], preferred_element_type=f32) + bg_ref[...]
        hl = jnp.dot(xs, wlb_ref[...], preferred_element_type=f32) + bl_ref[...]
        return hg, hl

    def ffn_down(r0, rows, hg, hl):
        hg = jnp.minimum(hg, SWIGLU_LIMIT)
        hl = jnp.clip(hl, -SWIGLU_LIMIT, SWIGLU_LIMIT)
        act = hg * jax.nn.sigmoid(SWIGLU_ALPHA * hg) * (hl + 1.0)
        acc_ref[slot, r0:r0 + rows, :] += jnp.dot(act.astype(bf16), wdb_ref[...], preferred_element_type=f32)

    def ffn_rows(r0, rows):
        ffn_down(r0, rows, *ffn_up(r0, rows))

    @pl.when(n > 0)
    def _():
        nb_full = n // MOE_SUB
        rem = n - nb_full * MOE_SUB
        ext = jnp.logical_and(jnp.logical_and(rem > 0, rem <= MOE_HALF), nb_full >= 1)
        own_block = jnp.logical_or(rem > MOE_HALF, jnp.logical_and(rem > 0, nb_full == 0))
        nb = nb_full + jnp.where(own_block, 1, 0)
        tiny = jnp.logical_and(nb_full == 0, rem <= MOE_HALF)

        def cast_weights(r):
            if r == 0:
                wgb_ref[...] = wg_ref[...].astype(bf16)
                wlb_ref[...] = wl_ref[...].astype(bf16)
                wdb_ref[...] = wd_ref[...].astype(bf16)

        def single(r, rows, chunks=1):
            def body():
                for c in range(chunks):
                    dma_chunk(r + c)
                cast_weights(r)
                ffn_rows(r * MOE_SUB, rows)
            return body

        def pair(r, rows_b):
            def body():
                dma_chunk(r)
                dma_chunk(r + 1)
                cast_weights(r)
                ra, rb = r * MOE_SUB, (r + 1) * MOE_SUB
                ha = ffn_up(ra, MOE_SUB)
                hb = ffn_up(rb, rows_b)
                ffn_down(ra, MOE_SUB, *ha)
                ffn_down(rb, rows_b, *hb)
            return body

        def idle(*rs):
            def body():
                for r in rs:
                    dma_chunk(r)
            return body

        def last_single(r):
            return lambda: lax.cond(
                ext, single(r, MOE_SUB + MOE_HALF, 2),
                lambda: lax.cond(tiny, single(r, MOE_HALF, 2), single(r, MOE_SUB, 2)))

        for r in range(0, MOE_NSUB - 1, 2):
            is_last_pair = jnp.logical_and(ext, nb == r + 2)
            lax.cond(
                nb >= r + 2,
                lambda r=r, is_last_pair=is_last_pair: lax.cond(
                    is_last_pair, pair(r, MOE_SUB + MOE_HALF), pair(r, MOE_SUB)),
                lambda r=r: lax.cond(nb == r + 1, last_single(r), idle(r, r + 1)))
        for r in range(MOE_NSUB - MOE_NSUB % 2, MOE_NSUB):
            lax.cond(nb == r + 1, single(r, MOE_SUB), idle(r))

    @pl.when(jnp.logical_and(t == MOE_NF - 1, n > 0))
    def _():
        scatter_wait(1 - slot)

        @pl.when(is_last_live)
        def _():
            def issue(r, c):
                scatter_row(r, start, n, slot)
                return c
            lax.fori_loop(0, MOE_ROWS, issue, 0, unroll=8)
            scatter_wait(slot)
            gather_wait()


def _moe(n_items, item_e, item_start, item_n, tok_sorted, dest_sorted, hn2, w_gate_up, b_gate_up, w_down, b_down):
    tf = MOE_TF
    live_t = lambda w, t, n: jnp.where(n[w] > 0, t, MOE_NF - 1)
    grid_spec = pltpu.PrefetchScalarGridSpec(
        num_scalar_prefetch=5,
        grid=(n_items, MOE_NF),
        in_specs=[
            pl.BlockSpec(memory_space=pl.ANY),
            pl.BlockSpec((None, D_MODEL, tf), lambda w, t, e, s, n, tk, ds: (e[w], 0, live_t(w, t, n))),
            pl.BlockSpec((None, D_MODEL, tf), lambda w, t, e, s, n, tk, ds: (e[w], 0, MOE_NF + live_t(w, t, n))),
            pl.BlockSpec((None, 1, tf), lambda w, t, e, s, n, tk, ds: (e[w], 0, live_t(w, t, n))),
            pl.BlockSpec((None, 1, tf), lambda w, t, e, s, n, tk, ds: (e[w], 0, MOE_NF + live_t(w, t, n))),
            pl.BlockSpec((None, tf, D_MODEL), lambda w, t, e, s, n, tk, ds: (e[w], live_t(w, t, n), 0)),
            pl.BlockSpec((None, 1, D_MODEL), lambda w, t, e, s, n, tk, ds: (e[w], 0, 0)),
        ],
        out_specs=pl.BlockSpec(memory_space=pl.ANY),
        scratch_shapes=[
            pltpu.VMEM((2, MOE_ROWS, D_MODEL), f32),
            pltpu.VMEM((MOE_ROWS, D_MODEL), f32),
            pltpu.VMEM((MOE_ROWS, D_MODEL), bf16),
            pltpu.VMEM((D_MODEL, tf), bf16),
            pltpu.VMEM((D_MODEL, tf), bf16),
            pltpu.VMEM((tf, D_MODEL), bf16),
            pltpu.SemaphoreType.DMA,
            pltpu.SemaphoreType.DMA,
        ],
    )
    return pl.pallas_call(
        _moe_kernel,
        grid_spec=grid_spec,
        out_shape=jax.ShapeDtypeStruct((Y_ROWS, D_MODEL), f32),
        compiler_params=_params(("arbitrary", "arbitrary"), BIG_VMEM_LIMIT_BYTES),
        name="moe_experts",
    )(item_e, item_start, item_n, tok_sorted, dest_sorted,
      hn2, w_gate_up, w_gate_up, b_gate_up, b_gate_up, w_down, b_down)


def _combine_kernel(y0_ref, y1_ref, y2_ref, y3_ref, h_ref, gate_ref, g_ref, o_ref):
    h = h_ref[...]
    gates = gate_ref[...]
    for k, y_ref in enumerate((y0_ref, y1_ref, y2_ref, y3_ref)):
        h = h + y_ref[...] * gates[:, k:k + 1]
    o_ref[...] = h * lax.rsqrt(jnp.mean(h * h, axis=-1, keepdims=True) + RMS_EPS) * g_ref[...]


def _combine(y, h1, gates, gain, tm=256):
    slot_spec = lambda k: pl.BlockSpec((tm, D_MODEL), lambda i, k=k: (k * (N_TOK // tm) + i, 0))
    return pl.pallas_call(
        _combine_kernel,
        grid=(N_TOK // tm,),
        in_specs=[slot_spec(k) for k in range(TOP_K)] + [
            pl.BlockSpec((tm, D_MODEL), lambda i: (i, 0)),
            pl.BlockSpec((tm, TOP_K), lambda i: (i, 0)),
            pl.BlockSpec((1, D_MODEL), lambda i: (0, 0)),
        ],
        out_specs=pl.BlockSpec((tm, D_MODEL), lambda i: (i, 0)),
        out_shape=jax.ShapeDtypeStruct((N_TOK, D_MODEL), f32),
        compiler_params=_params(("parallel",)),
        name="combine_norm",
    )(y, y, y, y, h1, gates, gain)


def _route(top_idx):
    expert_flat = top_idx.reshape(-1)
    order = jnp.argsort(expert_flat, stable=True).astype(jnp.int32)
    counts = jnp.zeros((N_EXPERTS,), jnp.int32).at[expert_flat].add(1)
    starts = jnp.cumsum(counts) - counts
    tok_sorted = order // TOP_K
    dest_sorted = (order % TOP_K) * N_TOK + tok_sorted
    tok_sorted = jnp.pad(tok_sorted, (0, MOE_ROWS))
    dest_sorted = jnp.pad(dest_sorted, (0, MOE_ROWS))
    chunks = (counts + MOE_ROWS - 1) // MOE_ROWS
    chunk_end = jnp.cumsum(chunks)
    n_items = chunk_end[-1]
    w = jnp.arange(MOE_ITEMS, dtype=jnp.int32)
    w_eff = jnp.minimum(w, n_items - 1)
    e = jnp.sum(w_eff[:, None] >= chunk_end[None, :], axis=1).astype(jnp.int32)
    c = w_eff - (chunk_end - chunks)[e]
    item_start = starts[e] + c * MOE_ROWS
    item_n = jnp.where(w < n_items, jnp.clip(counts[e] - c * MOE_ROWS, 0, MOE_ROWS), 0)
    return n_items, e, item_start.astype(jnp.int32), item_n.astype(jnp.int32), tok_sorted, dest_sorted


def kernel(x, meta_tokens, rel_bias_table, attn_norm, w_in, fox_forget_bias, lam_q1, lam_k1, lam_q2, lam_k2,
           diff_subln, w_branch_fox, w_branch_diff, w_out, ffn_norm, w_router, b_router, w_gate_up, b_gate_up,
           w_down, b_down, final_norm):
    x2d = x.reshape(N_TOK, D_MODEL)
    w_t = jnp.swapaxes(w_in[0], 0, 1)
    fb_pad = jnp.pad(fox_forget_bias[0], (0, LANES - FOX_HEADS)).reshape(1, LANES)

    meta_pad = jnp.pad(meta_tokens, ((0, N_META_PAD - N_META), (0, 0)))
    proj, proj_meta, ff_real, ff_meta = _inproj(x2d, meta_pad, attn_norm, w_t, tm=1024, tn=512)

    ck, ckm = _forget_cumsum(ff_real, ff_meta, fb_pad)
    o_fox = _fox_attention(proj, proj_meta, ck, ckm)

    near, mbias = _bias_tiles(rel_bias_table)
    lam_vecs = jnp.concatenate([lam_q1, lam_k1, lam_q2, lam_k2], axis=0)
    o_diff = _diff_attention(proj, proj_meta, near, mbias, lam_vecs, diff_subln)

    merged = _gated_merge(o_fox, o_diff, w_branch_fox[0].astype(bf16), w_branch_diff[0].astype(bf16), proj)
    w_router_pad = jnp.pad(w_router[0], ((0, 0), (0, LANES - N_EXPERTS)))
    w_router_hi = w_router_pad.astype(bf16)
    w_router_lo = (w_router_pad - w_router_hi.astype(f32)).astype(bf16)
    w_router_split = jnp.concatenate([w_router_hi, w_router_lo], axis=1)
    b_router_pad = jnp.pad(b_router[0], (0, LANES - N_EXPERTS)).reshape(1, LANES)
    h1, hn2, gates_t, idx_t = _outproj(merged, w_out[0].astype(bf16), x2d, ffn_norm, w_router_split, b_router_pad)
    gates = gates_t[:TOP_K].T

    n_items, item_e, item_start, item_n, tok_sorted, dest_sorted = _route(idx_t[:TOP_K].T)
    y = _moe(n_items, item_e, item_start, item_n, tok_sorted, dest_sorted, hn2,
             w_gate_up[0], b_gate_up[0].reshape(N_EXPERTS, 1, 2 * D_EXPERT),
             w_down[0], b_down[0].reshape(N_EXPERTS, 1, D_MODEL))
    out = _combine(y, h1, gates, final_norm.reshape(1, D_MODEL))
    return out.reshape(BATCH, SEQ, D_MODEL)
```
